```python
import math
import jax, jax.numpy as jnp
from jax import lax
import numpy as np

D_MODEL = 1024
BATCH = 16
SEQ = 256
DEPTH = 2
DEC_BATCH = 4
DEC_SEQ = 2048
PAST_LEN = 256

GRID_W = 64
N_LAYERS_AB = (DEPTH + 1) // 2
N_LAYERS_C = DEPTH // 2
EPS = 1e-6
LRU_WIDTH = D_MODEL
LRU_BLOCKS = 8
LRU_C = 8.0
CONV_W = 4
CONV_PAD = (2, 1)
GDN_HEADS = 8
GDN_DK = 128
GDN_DV = 128
GDN_QKV = 2 * GDN_HEADS * GDN_DK + GDN_HEADS * GDN_DV
GDN_CHUNK = 64
GLA_HEADS = 4
GLA_DK = D_MODEL // 2 // GLA_HEADS
GLA_DV = D_MODEL // GLA_HEADS
GLA_RANK = 16
GLA_TAU = 16.0
GLA_CHUNK = 32
N_EXPERTS = 16
N_GROUPS = 4
TOP_K = 2
D_EXPERT = 512
AB_IN = 2 * LRU_WIDTH + GDN_QKV + GDN_HEADS * GDN_DV + 4 * GDN_HEADS
AB_MIX = LRU_WIDTH + GDN_HEADS * GDN_DV
C_IN = 2 * GLA_HEADS * GLA_DK + 2 * GLA_HEADS * GLA_DV + 2 * GLA_RANK
C_MIX = GLA_HEADS * GLA_DV

kernel_name = "hybrid_rglru_gdn_gla_moe_prefix_diffusion_step"

F32 = jnp.float32


def _split(x, sizes):
    return jnp.split(x, np.cumsum(sizes)[:-1].tolist(), axis=-1)


def _rms_norm(x, g):
    xf = x.astype(F32)
    y = xf * lax.rsqrt(jnp.mean(xf * xf, axis=-1, keepdims=True) + EPS)
    return (y * g.astype(F32)).astype(x.dtype)


def _gated_rms_norm(o, z, g):
    B, T, H, V = o.shape
    y = o * lax.rsqrt(jnp.mean(o * o, axis=-1, keepdims=True) + EPS) * g.astype(F32)
    return (y.reshape(B, T, H * V) * jax.nn.silu(z.astype(F32))).astype(z.dtype)


def _l2norm(x):
    xf = x.astype(F32)
    return xf * lax.rsqrt(jnp.sum(xf * xf, axis=-1, keepdims=True) + EPS)


def _dwconv(x, w, b):
    C = x.shape[-1]
    y = lax.conv_general_dilated(x, w[:, None, :].astype(x.dtype), window_strides=(1,), padding=[CONV_PAD],
                                 dimension_numbers=('NWC', 'WIO', 'NWC'), feature_group_count=C)
    return y + b.astype(x.dtype)


def _lin_combine(e1, e2):
    a1, b1 = e1
    a2, b2 = e2
    return a1 * a2, a2 * b1 + b2


def _rglru_scan(x, h0, wa, ba, wx, bx, lam):
    B, T, W = x.shape
    xf = x.astype(F32)
    xb = xf.reshape(B, T, LRU_BLOCKS, W // LRU_BLOCKS)
    r = jax.nn.sigmoid(jnp.einsum('btni,nij->btnj', xb, wa.astype(F32)).reshape(B, T, W) + ba.astype(F32))
    ig = jax.nn.sigmoid(jnp.einsum('btni,nij->btnj', xb, wx.astype(F32)).reshape(B, T, W) + bx.astype(F32))
    log_a = -LRU_C * r * jax.nn.softplus(-lam.astype(F32))
    a = jnp.exp(log_a)
    b = jnp.sqrt(-jnp.expm1(2.0 * log_a)) * ig * xf
    b = b.at[:, 0].add(a[:, 0] * h0.astype(F32))
    _, h = lax.associative_scan(_lin_combine, (a, b), axis=1)
    return h


def _gated_delta_rule(q, k, v, g, beta, s0):
    B, T, H, _ = q.shape
    C = GDN_CHUNK
    N = T // C
    chunk = lambda t: t.astype(F32).reshape(B, N, C, *t.shape[2:])
    q, k, v, g, beta = chunk(q), chunk(k), chunk(v), chunk(g), chunk(beta)
    gc = jnp.cumsum(g, axis=2)
    idx = jnp.arange(C)
    causal = idx[:, None] >= idx[None, :]
    strict = idx[:, None] > idx[None, :]
    gh = jnp.moveaxis(gc, 3, 2)
    diff = gh[..., :, None] - gh[..., None, :]
    decay = jnp.where(causal, jnp.exp(jnp.where(causal, diff, 0.0)), 0.0)
    kb = k * beta[..., None]
    A = jnp.where(strict, jnp.einsum('bnihk,bnjhk->bnhij', kb, k) * decay, 0.0)
    eye = jnp.broadcast_to(jnp.eye(C, dtype=F32), A.shape)
    tmat = lax.linalg.triangular_solve(eye + A, eye, left_side=True, lower=True, unit_diagonal=True)
    u = jnp.einsum('bnhij,bnjhv->bnihv', tmat, v * beta[..., None])
    w = jnp.einsum('bnhij,bnjhk->bnihk', tmat, kb * jnp.exp(gc)[..., None])
    qk = jnp.einsum('bnihk,bnjhk->bnhij', q, k) * decay

    def step(s, xs):
        q_c, k_c, u_c, w_c, gc_c, qk_c = xs
        v_new = u_c - jnp.einsum('bihk,bhkv->bihv', w_c, s)
        o = (jnp.einsum('bihk,bhkv->bihv', q_c * jnp.exp(gc_c)[..., None], s)
             + jnp.einsum('bhij,bjhv->bihv', qk_c, v_new))
        g_last = gc_c[:, -1]
        k_dec = k_c * jnp.exp(g_last[:, None] - gc_c)[..., None]
        s = s * jnp.exp(g_last)[..., None, None] + jnp.einsum('bihk,bihv->bhkv', k_dec, v_new)
        return s, o

    xs = tuple(jnp.moveaxis(t, 1, 0) for t in (q, k, u, w, gc, qk))
    s, o = lax.scan(step, s0.astype(F32), xs)
    return jnp.moveaxis(o, 0, 1).reshape(B, T, H, -1), s


def _gla_chunked(q, k, v, glog, s0):
    B, T, H, _ = q.shape
    C = GLA_CHUNK
    N = T // C
    chunk = lambda t: t.astype(F32).reshape(B, N, C, *t.shape[2:])
    q, k, v, glog = chunk(q), chunk(k), chunk(v), chunk(glog)
    gc = jnp.cumsum(glog, axis=2)
    idx = jnp.arange(C)
    mask = (idx[:, None] >= idx[None, :])[:, :, None, None]
    diff = gc[:, :, :, None] - gc[:, :, None, :]
    decay = jnp.where(mask, jnp.exp(jnp.where(mask, diff, 0.0)), 0.0)
    A = jnp.einsum('bnihk,bnjhk,bnijhk->bnhij', q, k, decay)

    def step(s, xs):
        q_c, k_c, v_c, gc_c, a_c = xs
        o = (jnp.einsum('bihk,bhkv->bihv', q_c * jnp.exp(gc_c), s)
             + jnp.einsum('bhij,bjhv->bihv', a_c, v_c))
        g_last = gc_c[:, -1]
        k_dec = k_c * jnp.exp(g_last[:, None] - gc_c)
        s = s * jnp.exp(g_last)[..., None] + jnp.einsum('bihk,bihv->bhkv', k_dec, v_c)
        return s, o

    xs = tuple(jnp.moveaxis(t, 1, 0) for t in (q, k, v, gc, A))
    s, o = lax.scan(step, s0.astype(F32), xs)
    return jnp.moveaxis(o, 0, 1).reshape(B, T, H, -1), s


def _rev(t):
    return jnp.flip(t, 1)


def _mixer_ab(h, lru_h0, gdn_s0, with_state, w_in, lru_conv_w, lru_conv_b, lru_wa, lru_ba, lru_wx, lru_bx,
              lru_lam, gdn_conv_w, gdn_conv_b, gdn_a_log, gdn_dt_bias, gdn_norm_g, w_out):
    B, T, _ = h.shape
    H = GDN_HEADS
    y_br, x_br, qkv, z, a_raw, b_raw = _split(h @ w_in, [LRU_WIDTH, LRU_WIDTH, GDN_QKV, H * GDN_DV, 2 * H, 2 * H])
    xc = _dwconv(x_br, lru_conv_w, lru_conv_b)
    h_f = _rglru_scan(xc, lru_h0[:, 0], lru_wa[0], lru_ba[0], lru_wx[0], lru_bx[0], lru_lam[0])
    h_b = _rev(_rglru_scan(_rev(xc), lru_h0[:, 1], lru_wa[1], lru_ba[1], lru_wx[1], lru_bx[1], lru_lam[1]))
    lru_out = (jax.nn.gelu(y_br.astype(F32)) * (h_f + h_b)).astype(h.dtype)
    qkv = jax.nn.silu(_dwconv(qkv, gdn_conv_w, gdn_conv_b))
    q, k, v = _split(qkv, [H * GDN_DK, H * GDN_DK, H * GDN_DV])
    q = _l2norm(q.reshape(B, T, H, GDN_DK)) * (GDN_DK ** -0.5)
    k = _l2norm(k.reshape(B, T, H, GDN_DK))
    v = v.reshape(B, T, H, GDN_DV).astype(F32)
    g = -jnp.exp(gdn_a_log.astype(F32)) * jax.nn.softplus(a_raw.astype(F32).reshape(B, T, 2, H)
                                                          + gdn_dt_bias.astype(F32))
    beta = jax.nn.sigmoid(b_raw.astype(F32).reshape(B, T, 2, H))
    o_f, s_f = _gated_delta_rule(q, k, v, g[:, :, 0], beta[:, :, 0], gdn_s0[:, 0])
    o_b, s_b = _gated_delta_rule(_rev(q), _rev(k), _rev(v), _rev(g[:, :, 1]), _rev(beta[:, :, 1]), gdn_s0[:, 1])
    gdn_out = _gated_rms_norm(o_f + _rev(o_b), z, gdn_norm_g)
    out = jnp.concatenate([lru_out, gdn_out], axis=-1) @ w_out
    if with_state:
        lru_state = jnp.stack([h_f[:, -1], h_b[:, 0]], axis=1)
        gdn_state = jnp.stack([s_f, s_b], axis=1)
        return out, lru_state, gdn_state
    return out


def _mixer_c(h, s0, with_state, w_in, w2, b2, norm_g, w_out):
    B, T, _ = h.shape
    H = GLA_HEADS
    q, k, v, gz, lr = _split(h @ w_in, [H * GLA_DK, H * GLA_DK, H * GLA_DV, H * GLA_DV, 2 * GLA_RANK])
    q = q.astype(F32).reshape(B, T, H, GLA_DK) * (GLA_DK ** -0.5)
    k = k.astype(F32).reshape(B, T, H, GLA_DK)
    v = v.astype(F32).reshape(B, T, H, GLA_DV)
    glog = jax.nn.log_sigmoid(jnp.einsum('btdr,drk->btdk', lr.astype(F32).reshape(B, T, 2, GLA_RANK), w2.astype(F32))
                              + b2.astype(F32)) / GLA_TAU
    glog = glog.reshape(B, T, 2, H, GLA_DK)
    o_f, s_f = _gla_chunked(q, k, v, glog[:, :, 0], s0[:, 0])
    o_b, s_b = _gla_chunked(_rev(q), _rev(k), _rev(v), _rev(glog[:, :, 1]), s0[:, 1])
    out = _gated_rms_norm(o_f + _rev(o_b), gz, norm_g) @ w_out
    if with_state:
        return out, jnp.stack([s_f, s_b], axis=1)
    return out


def _moe(h, router_w, router_b, w_gate, w_up, w_down):
    s = jax.nn.sigmoid(jnp.einsum('btd,de->bte', h.astype(F32), router_w.astype(F32)))
    sel = s + router_b.astype(F32)
    grouped = sel.reshape(*sel.shape[:-1], N_GROUPS, N_EXPERTS // N_GROUPS)
    group_score = jnp.sum(lax.top_k(grouped, TOP_K)[0], axis=-1)
    g_best = jnp.argmax(group_score, axis=-1)
    in_group = (jnp.arange(N_EXPERTS) // (N_EXPERTS // N_GROUPS)) == g_best[..., None]
    _, top_idx = lax.top_k(jnp.where(in_group, sel, -jnp.inf), TOP_K)
    top_w = jnp.take_along_axis(s, top_idx, axis=-1)
    top_w = top_w / jnp.sum(top_w, axis=-1, keepdims=True)
    gates = jnp.sum(jax.nn.one_hot(top_idx, N_EXPERTS, dtype=F32) * top_w[..., None], axis=-2)
    hid = jax.nn.silu(jnp.einsum('btd,edf->btef', h, w_gate)) * jnp.einsum('btd,edf->btef', h, w_up)
    hid = hid * gates[..., None].astype(h.dtype)
    return jnp.einsum('btef,efd->btd', hid, w_down)


def _to_col_major(x):
    B, T, D = x.shape
    rows = T // GRID_W
    return x.reshape(B, rows, GRID_W, D).transpose(0, 2, 1, 3).reshape(B, T, D)


def _to_row_major(x):
    B, T, D = x.shape
    rows = T // GRID_W
    return x.reshape(B, GRID_W, rows, D).transpose(0, 2, 1, 3).reshape(B, T, D)


def setup_inputs(seed: int = 0) -> dict:
    key = jax.random.key(seed)
    ks = iter(jax.random.split(key, 48))
    nrm = lambda shape, scale: scale * jax.random.normal(next(ks), shape, F32)
    uni = lambda shape, lo, hi: jax.random.uniform(next(ks), shape, F32, lo, hi)
    D = D_MODEL
    bw = LRU_WIDTH // LRU_BLOCKS
    p = uni((N_LAYERS_AB, 2, LRU_WIDTH), 0.9, 0.999) ** (1.0 / LRU_C)
    dt = jnp.exp(uni((N_LAYERS_AB, 2, GDN_HEADS), math.log(1e-3), math.log(1e-1)))
    return {
        "x_prompt": nrm((BATCH, SEQ, D), 1.0),
        "x_sample": nrm((DEC_BATCH, DEC_SEQ, D), 1.0),
        "state_lru": nrm((DEC_BATCH, N_LAYERS_AB, 2, LRU_WIDTH), 0.5),
        "state_gdn": nrm((DEC_BATCH, N_LAYERS_AB, 2, GDN_HEADS, GDN_DK, GDN_DV), 0.1),
        "state_gla": nrm((DEC_BATCH, N_LAYERS_C, 2, GLA_HEADS, GLA_DK, GLA_DV), 0.1),
        "c": nrm((DEC_BATCH, D), 1.0),
        "c_ctx": nrm((D,), 1.0),
        "ada_w": nrm((DEPTH, D, 6 * D), 0.5 * D ** -0.5),
        "ada_b": nrm((DEPTH, 6 * D), 0.01),
        "norm1_g": 1.0 + nrm((DEPTH, D), 0.01),
        "norm2_g": 1.0 + nrm((DEPTH, D), 0.01),
        "final_norm_g": 1.0 + nrm((D,), 0.01),
        "ab_w_in": nrm((N_LAYERS_AB, D, AB_IN), D ** -0.5),
        "lru_conv_w": nrm((N_LAYERS_AB, CONV_W, LRU_WIDTH), CONV_W ** -0.5),
        "lru_conv_b": nrm((N_LAYERS_AB, LRU_WIDTH), 0.01),
        "lru_wa": nrm((N_LAYERS_AB, 2, LRU_BLOCKS, bw, bw), bw ** -0.5),
        "lru_ba": nrm((N_LAYERS_AB, 2, LRU_WIDTH), 0.01),
        "lru_wx": nrm((N_LAYERS_AB, 2, LRU_BLOCKS, bw, bw), bw ** -0.5),
        "lru_bx": nrm((N_LAYERS_AB, 2, LRU_WIDTH), 0.01),
        "lru_lam": jnp.log(p) - jnp.log1p(-p),
        "gdn_conv_w": nrm((N_LAYERS_AB, CONV_W, GDN_QKV), CONV_W ** -0.5),
        "gdn_conv_b": nrm((N_LAYERS_AB, GDN_QKV), 0.01),
        "gdn_a_log": jnp.log(uni((N_LAYERS_AB, 2, GDN_HEADS), 1.0, 16.0)),
        "gdn_dt_bias": dt + jnp.log(-jnp.expm1(-dt)),
        "gdn_norm_g": 1.0 + nrm((N_LAYERS_AB, GDN_DV), 0.01),
        "ab_w_out": nrm((N_LAYERS_AB, AB_MIX, D), AB_MIX ** -0.5),
        "gla_w_in": nrm((N_LAYERS_C, D, C_IN), D ** -0.5),
        "gla_w2": nrm((N_LAYERS_C, 2, GLA_RANK, GLA_HEADS * GLA_DK), GLA_RANK ** -0.5),
        "gla_b2": nrm((N_LAYERS_C, 2, GLA_HEADS * GLA_DK), 0.1),
        "gla_norm_g": 1.0 + nrm((N_LAYERS_C, GLA_DV), 0.01),
        "gla_w_out": nrm((N_LAYERS_C, C_MIX, D), C_MIX ** -0.5),
        "router_w": nrm((D, N_EXPERTS), D ** -0.5),
        "router_b": nrm((N_EXPERTS,), 0.01),
        "moe_w_gate": nrm((DEPTH, N_EXPERTS, D, D_EXPERT), D ** -0.5),
        "moe_w_up": nrm((DEPTH, N_EXPERTS, D, D_EXPERT), D ** -0.5),
        "moe_w_down": nrm((DEPTH, N_EXPERTS, D_EXPERT, D), D_EXPERT ** -0.5),
    }


def reference(x_prompt, x_sample, state_lru, state_gdn, state_gla, c, c_ctx, ada_w, ada_b, norm1_g, norm2_g,
              final_norm_g, ab_w_in, lru_conv_w, lru_conv_b, lru_wa, lru_ba, lru_wx, lru_bx, lru_lam, gdn_conv_w,
              gdn_conv_b, gdn_a_log, gdn_dt_bias, gdn_norm_g, ab_w_out, gla_w_in, gla_w2, gla_b2, gla_norm_g,
              gla_w_out, router_w, router_b, moe_w_gate, moe_w_up, moe_w_down):
    xp, xs = x_prompt, x_sample
    bp = xp.shape[0]
    new_lru, new_gdn, new_gla = [], [], []
    for i in range(DEPTH):
        j = i // 2
        mod_p = (jax.nn.silu(c_ctx) @ ada_w[i] + ada_b[i])[None, None, :]
        mod_s = (jax.nn.silu(c) @ ada_w[i] + ada_b[i])[:, None, :]
        sh1p, sc1p, g1p, sh2p, sc2p, g2p = jnp.split(mod_p, 6, axis=-1)
        sh1s, sc1s, g1s, sh2s, sc2s, g2s = jnp.split(mod_s, 6, axis=-1)
        hp = _rms_norm(xp, norm1_g[i]) * (1.0 + sc1p) + sh1p
        hs = _rms_norm(xs, norm1_g[i]) * (1.0 + sc1s) + sh1s
        if i % 2 == 0:
            ab = (ab_w_in[j], lru_conv_w[j], lru_conv_b[j], lru_wa[j], lru_ba[j], lru_wx[j], lru_bx[j], lru_lam[j],
                  gdn_conv_w[j], gdn_conv_b[j], gdn_a_log[j], gdn_dt_bias[j], gdn_norm_g[j], ab_w_out[j])
            zero_lru = jnp.zeros((bp, 2, LRU_WIDTH), F32)
            zero_gdn = jnp.zeros((bp, 2, GDN_HEADS, GDN_DK, GDN_DV), F32)
            op, lru_st, gdn_st = _mixer_ab(hp, zero_lru, zero_gdn, True, *ab)
            os_ = _mixer_ab(hs, state_lru[:, j], state_gdn[:, j], False, *ab)
            new_lru.append(lru_st)
            new_gdn.append(gdn_st)
        else:
            cp = (gla_w_in[j], gla_w2[j], gla_b2[j], gla_norm_g[j], gla_w_out[j])
            zero_gla = jnp.zeros((bp, 2, GLA_HEADS, GLA_DK, GLA_DV), F32)
            op, gla_st = _mixer_c(hp, zero_gla, True, *cp)
            os_ = _to_row_major(_mixer_c(_to_col_major(hs), state_gla[:, j], False, *cp))
            new_gla.append(gla_st)
        xp = xp + g1p * op
        xs = xs + g1s * os_
        hp = _rms_norm(xp, norm2_g[i]) * (1.0 + sc2p) + sh2p
        hs = _rms_norm(xs, norm2_g[i]) * (1.0 + sc2s) + sh2s
        xp = xp + g2p * _moe(hp, router_w, router_b, moe_w_gate[i], moe_w_up[i], moe_w_down[i])
        xs = xs + g2s * _moe(hs, router_w, router_b, moe_w_gate[i], moe_w_up[i], moe_w_down[i])
    y_prompt = _rms_norm(xp, final_norm_g)
    y_sample = _rms_norm(xs, final_norm_g)
    return (y_prompt, y_sample, jnp.stack(new_lru, axis=1), jnp.stack(new_gdn, axis=1), jnp.stack(new_gla, axis=1))
```

```python
import functools
import math

import jax
import jax.numpy as jnp
from jax import lax
from jax.experimental import pallas as pl
from jax.experimental.pallas import tpu as pltpu

F32 = jnp.float32
BF16 = jnp.bfloat16

D = 1024
N_PROMPT_SEQ, T_PROMPT = 16, 256
N_SAMPLE_SEQ, T_SAMPLE = 4, 2048
N_PROMPT = N_PROMPT_SEQ * T_PROMPT
N_SAMPLE = N_SAMPLE_SEQ * T_SAMPLE
N_TOK = N_PROMPT + N_SAMPLE
GRID_W = 64
GRID_H = T_SAMPLE // GRID_W
EPS = 1e-6
LANES = 128
SUBLANES = 8

LRU_C = 8.0
LRU_BLOCK = 128
CONV_LEFT = 2
CONV_W = 4
GDN_H, GDN_DK, GDN_DV, GDN_CHUNK = 8, 128, 128, 64
GLA_H, GLA_DK, GLA_DV, GLA_CHUNK, GLA_RANK, GLA_TAU = 4, 128, 256, 32, 16, 16.0
N_EXPERTS, N_GROUPS, D_EXPERT = 16, 4, 512
GROUP_SZ = N_EXPERTS // N_GROUPS

TM = 256
TM_MOE = 1024
VMEM_LIMIT = 56 * 1024 * 1024

_NT = (((1,), (1,)), ((), ()))
_TN = (((0,), (0,)), ((), ()))


def _dot(a, b):
    return jnp.dot(a, b, preferred_element_type=F32)


def _dot_nt(a, b):
    return lax.dot_general(a, b, _NT, preferred_element_type=F32)


def _dot_tn(a, b):
    return lax.dot_general(a, b, _TN, preferred_element_type=F32)


def _split2(x):
    hi = x.astype(BF16)
    lo = (x - hi.astype(F32)).astype(BF16)
    return hi, lo


def _split3(x):
    hi = x.astype(BF16)
    r = x - hi.astype(F32)
    mid = r.astype(BF16)
    lo = (r - mid.astype(F32)).astype(BF16)
    return hi, mid, lo


def _dot_sel(sel_bf16, x):
    hi, mid, lo = _split3(x)
    return _dot(sel_bf16, hi) + _dot(sel_bf16, mid) + _dot(sel_bf16, lo)


def _dot_hp(a, b, dot=_dot):
    ah, al = _split2(a)
    bh, bl = _split2(b)
    return dot(ah, bh) + dot(ah, bl) + dot(al, bh)


def _sigmoid(x):
    return 1.0 / (1.0 + jnp.exp(-x))


def _silu(x):
    return x * _sigmoid(x)


def _softplus(x):
    return jnp.maximum(x, 0.0) + jnp.log1p(jnp.exp(-jnp.abs(x)))


def _gelu_tanh(x):
    return 0.5 * x * (1.0 + jnp.tanh(math.sqrt(2.0 / math.pi) * (x + 0.044715 * (x * x * x))))


def _params(*sem):
    return pltpu.CompilerParams(dimension_semantics=sem, vmem_limit_bytes=VMEM_LIMIT)


def _mod_row(i, tm):
    n_prompt_tiles = N_PROMPT // tm
    per_seq = T_SAMPLE // tm
    return jnp.where(i < n_prompt_tiles, 0, 1 + (i - n_prompt_tiles) // per_seq)


def _mod_kernel(c_ref, w_ref, b_ref, o_ref):
    c = c_ref[...]
    o_ref[0] = _dot_hp(_silu(c), w_ref[0]) + b_ref[0]


def _modulation(cvec, ada_w, ada_b):
    depth = ada_w.shape[0]
    n6 = ada_w.shape[2]
    return pl.pallas_call(
        _mod_kernel,
        grid=(depth, n6 // D),
        in_specs=[pl.BlockSpec((SUBLANES, D), lambda l, j: (0, 0)),
                  pl.BlockSpec((1, D, D), lambda l, j: (l, 0, j)),
                  pl.BlockSpec((1, 1, D), lambda l, j: (l, 0, j))],
        out_specs=pl.BlockSpec((1, SUBLANES, D), lambda l, j: (l, 0, j)),
        out_shape=jax.ShapeDtypeStruct((depth, SUBLANES, n6), F32),
        compiler_params=_params("arbitrary", "arbitrary"),
    )(cvec, ada_w, ada_b.reshape(depth, 1, n6))


def _inproj_kernel(x_ref, mod_ref, g_ref, *refs, n_w, has_t):
    w_refs = refs[:n_w]
    o_refs = refs[n_w + has_t:2 * n_w + has_t]
    x = x_ref[...]
    y = x * lax.rsqrt(jnp.mean(x * x, axis=-1, keepdims=True) + EPS) * g_ref[...]
    shift = mod_ref[0, :, 0:D]
    scale = mod_ref[0, :, D:2 * D]
    h = (y * (1.0 + scale) + shift).astype(BF16)
    for w_ref, o_ref in zip(w_refs, o_refs):
        o_ref[...] = _dot(h, w_ref[...]).astype(o_ref.dtype)
    if has_t:
        wt_ref = refs[n_w]
        ot_ref = refs[2 * n_w + 1]
        ot_ref[...] = _dot_nt(wt_ref[...], h)


def _inproj(x, mod_l, gain, weights, out_dtypes, w_t=None):
    n_w = len(weights)
    in_specs = [pl.BlockSpec((TM, D), lambda i: (i, 0)),
                pl.BlockSpec((1, 1, 6 * D), lambda i: (_mod_row(i, TM), 0, 0)),
                pl.BlockSpec((1, D), lambda i: (0, 0))]
    in_specs += [pl.BlockSpec(w.shape, lambda i: (0, 0)) for w in weights]
    out_specs = [pl.BlockSpec((TM, w.shape[1]), lambda i: (i, 0)) for w in weights]
    out_shape = [jax.ShapeDtypeStruct((N_TOK, w.shape[1]), dt) for w, dt in zip(weights, out_dtypes)]
    args = [x, mod_l.reshape(SUBLANES, 1, 6 * D), gain.reshape(1, D)] + list(weights)
    if w_t is not None:
        in_specs.append(pl.BlockSpec(w_t.shape, lambda i: (0, 0)))
        out_specs.append(pl.BlockSpec((w_t.shape[0], TM), lambda i: (0, i)))
        out_shape.append(jax.ShapeDtypeStruct((w_t.shape[0], N_TOK), F32))
        args.append(w_t)
    return pl.pallas_call(
        functools.partial(_inproj_kernel, n_w=n_w, has_t=int(w_t is not None)),
        grid=(N_TOK // TM,),
        in_specs=in_specs, out_specs=out_specs, out_shape=out_shape,
        compiler_params=_params("arbitrary"),
    )(*args)


def _conv_from_padded(xp_ref, t, cw, cb):
    acc = cb
    for j in range(CONV_W):
        acc = acc + cw[j:j + 1, :] * xp_ref[pl.ds(SUBLANES - CONV_LEFT + j, t), :]
    return acc


def _fill_padded(xp_ref, x, t):
    zeros = jnp.zeros((SUBLANES, xp_ref.shape[1]), F32)
    xp_ref[pl.ds(0, SUBLANES), :] = zeros
    xp_ref[pl.ds(SUBLANES + t, SUBLANES), :] = zeros
    xp_ref[pl.ds(SUBLANES, t), :] = x


_SCAN_ROWS = 32


def _scan_tile(a, b, carry, rev):
    row = lax.broadcasted_iota(jnp.int32, a.shape, 0)
    for s in (1, 2, 4):
        if rev:
            a_s = pltpu.roll(a, SUBLANES - s, 0)
            b_s = pltpu.roll(b, SUBLANES - s, 0)
            m = row < SUBLANES - s
        else:
            a_s = pltpu.roll(a, s, 0)
            b_s = pltpu.roll(b, s, 0)
            m = row >= s
        b = jnp.where(m, a * b_s + b, b)
        a = jnp.where(m, a * a_s, a)
    h = a * carry + b
    return h, (h[0:1] if rev else h[SUBLANES - 1:SUBLANES])


def _lru_kernel(*refs, t, with_h0):
    if with_h0:
        (x_ref, y_ref, cw_ref, cb_ref, wg_ref, bg_ref, lam_ref, h0_ref,
         o_ref, xp, a_f, b_f, a_b, b_b) = refs
    else:
        (x_ref, y_ref, cw_ref, cb_ref, wg_ref, bg_ref, lam_ref,
         o_ref, st_ref, xp, a_f, b_f, a_b, b_b) = refs
    _fill_padded(xp, x_ref[...].astype(F32), t)
    xc = _conv_from_padded(xp, t, cw_ref[...], cb_ref[...])
    gates = _sigmoid(_dot(xc.astype(BF16), wg_ref[0]) + bg_ref[0])
    sp = _softplus(-lam_ref[0])
    for d, (a_ref, b_ref) in enumerate(((a_f, b_f), (a_b, b_b))):
        r = gates[:, (2 * d) * LRU_BLOCK:(2 * d + 1) * LRU_BLOCK]
        ig = gates[:, (2 * d + 1) * LRU_BLOCK:(2 * d + 2) * LRU_BLOCK]
        log_a = -LRU_C * r * sp[d:d + 1, :]
        a = jnp.exp(log_a)
        a_ref[...] = a
        b_ref[...] = jnp.sqrt(1.0 - a * a) * ig * xc

    n_it = t // _SCAN_ROWS
    if with_h0:
        c0 = (h0_ref[0, 0:1, :], h0_ref[0, 1:2, :])
    else:
        c0 = (jnp.zeros((1, LRU_BLOCK), F32), jnp.zeros((1, LRU_BLOCK), F32))

    def body(i, carry):
        cf, cb_ = carry
        base_f = pl.multiple_of(i * _SCAN_ROWS, _SCAN_ROWS)
        base_b = pl.multiple_of((n_it - 1 - i) * _SCAN_ROWS, _SCAN_ROWS)
        for k in range(_SCAN_ROWS // SUBLANES):
            rf = pl.ds(base_f + k * SUBLANES, SUBLANES)
            h, cf = _scan_tile(a_f[rf, :], b_f[rf, :], cf, False)
            b_f[rf, :] = h
            rb = pl.ds(base_b + _SCAN_ROWS - (k + 1) * SUBLANES, SUBLANES)
            h, cb_ = _scan_tile(a_b[rb, :], b_b[rb, :], cb_, True)
            b_b[rb, :] = h
        return cf, cb_

    cf, cb_ = lax.fori_loop(0, n_it, body, c0)
    o_ref[...] = (_gelu_tanh(y_ref[...].astype(F32)) * (b_f[...] + b_b[...])).astype(o_ref.dtype)
    if not with_h0:
        st_ref[0, 0:1, :] = cf
        st_ref[0, 1:2, :] = cb_


def _lru_call(xbr, ybr, mix_prev, p, t, nseq, row0, h0=None):
    nb = D // LRU_BLOCK
    blk0 = row0 // t
    tok = lambda s, j: (blk0 + s, j)
    in_specs = [pl.BlockSpec((t, LRU_BLOCK), tok),
                pl.BlockSpec((t, LRU_BLOCK), tok),
                pl.BlockSpec((CONV_W, LRU_BLOCK), lambda s, j: (0, j)),
                pl.BlockSpec((1, LRU_BLOCK), lambda s, j: (0, j)),
                pl.BlockSpec((1, LRU_BLOCK, 4 * LRU_BLOCK), lambda s, j: (j, 0, 0)),
                pl.BlockSpec((1, 1, 4 * LRU_BLOCK), lambda s, j: (j, 0, 0)),
                pl.BlockSpec((1, 2, LRU_BLOCK), lambda s, j: (j, 0, 0))]
    args = [xbr, ybr, p["conv_w"], p["conv_b"], p["wg"], p["bg"], p["lam"]]
    scratch = [pltpu.VMEM((t + 2 * SUBLANES, LRU_BLOCK), F32)] + [pltpu.VMEM((t, LRU_BLOCK), F32)] * 4
    mix_shape = jax.ShapeDtypeStruct((N_TOK, 2 * D), BF16)
    out_mix_spec = pl.BlockSpec((t, LRU_BLOCK), tok)
    if h0 is not None:
        in_specs.append(pl.BlockSpec((1, 2, LRU_BLOCK), lambda s, j: (s, 0, j)))
        args.append(h0)
        in_specs.append(pl.BlockSpec(memory_space=pl.ANY))
        args.append(mix_prev)
        kern = lambda *r: _lru_kernel(*r[:8], *r[9:], t=t, with_h0=True)
        return pl.pallas_call(
            kern, grid=(nseq, nb), in_specs=in_specs, out_specs=out_mix_spec, out_shape=mix_shape,
            scratch_shapes=scratch, input_output_aliases={len(args) - 1: 0},
            compiler_params=_params("arbitrary", "arbitrary"),
        )(*args)
    return pl.pallas_call(
        functools.partial(_lru_kernel, t=t, with_h0=False),
        grid=(nseq, nb), in_specs=in_specs,
        out_specs=[out_mix_spec, pl.BlockSpec((1, 2, LRU_BLOCK), lambda s, j: (s, 0, j))],
        out_shape=[mix_shape, jax.ShapeDtypeStruct((nseq, 2, D), F32)],
        scratch_shapes=scratch,
        compiler_params=_params("arbitrary", "arbitrary"),
    )(*args)


GDN_INV_HP = True


def _tri_inverse(a_mat, eye):
    mm = _dot_hp if GDN_INV_HP else (lambda x, y: _dot(x.astype(BF16), y.astype(BF16)))
    n = -a_mat
    t_mat = eye + n
    steps = int(math.log2(a_mat.shape[0])) - 1
    for _ in range(steps):
        n = mm(n, n)
        t_mat = t_mat + mm(t_mat, n)
    return t_mat


def _gdn_chunk(q, k, v, gcb, gcrow, beta, gl, s, rev, eye):
    c = q.shape[0]
    diff = gcb[:, 0:c] - gcrow
    ii = lax.broadcasted_iota(jnp.int32, (c, c), 0)
    jj = lax.broadcasted_iota(jnp.int32, (c, c), 1)
    incl = (ii <= jj) if rev else (ii >= jj)
    strict = (ii < jj) if rev else (ii > jj)
    decay = jnp.where(incl, jnp.exp(jnp.where(incl, diff, 0.0)), 0.0)
    kb = k * beta
    kbf = k.astype(BF16)
    stacked = jnp.concatenate([kb, q], axis=0).astype(BF16)
    prod = _dot_nt(stacked, kbf)
    a_mat = jnp.where(strict, prod[0:c] * decay, 0.0)
    qk = prod[c:2 * c] * decay
    t_mat = _tri_inverse(a_mat, eye)
    eg = jnp.exp(gcb)
    rhs = jnp.concatenate([v * beta, kb * eg], axis=1).astype(BF16)
    uw = _dot(t_mat.astype(BF16), rhs)
    u, w = uw[:, 0:GDN_DV], uw[:, GDN_DV:]
    sb = s.astype(BF16)
    lhs = jnp.concatenate([w, q * eg], axis=0).astype(BF16)
    ws_qs = _dot(lhs, sb)
    v_new = u - ws_qs[0:c]
    o = ws_qs[c:2 * c] + _dot(qk.astype(BF16), v_new.astype(BF16))
    k_dec = k * jnp.exp(gl - gcb)
    s_new = s * jnp.exp(gl) + _dot_tn(k_dec.astype(BF16), v_new.astype(BF16))
    return o, s_new


def _gdn_kernel(*refs, t, with_s0):
    if with_s0:
        (q_ref, k_ref, v_ref, z_ref, ab_ref, abt_ref, cwq_ref, cwk_ref, cwv_ref, cbq_ref, cbk_ref, cbv_ref,
         alog_ref, dtb_ref, alogc_ref, dtbc_ref, ng_ref, s0_ref,
         o_ref, xp, qs, ks, vs, osum, gcb_s, beta_s, gcrow_s) = refs
    else:
        (q_ref, k_ref, v_ref, z_ref, ab_ref, abt_ref, cwq_ref, cwk_ref, cwv_ref, cbq_ref, cbk_ref, cbv_ref,
         alog_ref, dtb_ref, alogc_ref, dtbc_ref, ng_ref,
         o_ref, st_ref, xp, qs, ks, vs, osum, gcb_s, beta_s, gcrow_s) = refs
    c = GDN_CHUNK
    n_chunks = t // c
    head = pl.program_id(1)

    for x_ref, cw_ref, cb_ref, dst, kind in ((q_ref, cwq_ref, cbq_ref, qs, "q"), (k_ref, cwk_ref, cbk_ref, ks, "k"),
                                             (v_ref, cwv_ref, cbv_ref, vs, "v")):
        _fill_padded(xp, x_ref[...].astype(F32), t)
        y = _silu(_conv_from_padded(xp, t, cw_ref[...], cb_ref[...]))
        if kind != "v":
            y = y * lax.rsqrt(jnp.sum(y * y, axis=-1, keepdims=True) + EPS)
        if kind == "q":
            y = y * (GDN_DK ** -0.5)
        dst[...] = y

    ab = ab_ref[...]
    lane = lax.broadcasted_iota(jnp.int32, ab.shape, 1)
    gall = jnp.where(lane < 2 * GDN_H, -jnp.exp(alog_ref[...]) * _softplus(ab + dtb_ref[...]), _sigmoid(ab))
    pick = lambda idx: jnp.sum(jnp.where(lane == idx, gall, 0.0), axis=-1, keepdims=True)
    abt_rows = lambda idx: abt_ref[idx, 0]
    ri = lax.broadcasted_iota(jnp.int32, (c, c), 0)
    ci = lax.broadcasted_iota(jnp.int32, (c, c), 1)
    slab = 256
    bi = lax.broadcasted_iota(jnp.int32, (slab, slab), 0)
    bj = lax.broadcasted_iota(jnp.int32, (slab, slab), 1)
    same_chunk = (bi // c) == (bj // c)
    for d in range(2):
        rev = d == 1
        g_col = pick(d * GDN_H + head)
        beta_s[d] = jnp.broadcast_to(pick(2 * GDN_H + d * GDN_H + head), (t, LANES))
        tri = jnp.where(same_chunk & ((bi <= bj) if rev else (bi >= bj)), 1.0, 0.0).astype(BF16)
        for r in range(t // slab):
            gb = jnp.broadcast_to(g_col[r * slab:(r + 1) * slab], (slab, LANES))
            gcb_s[d, pl.ds(r * slab, slab), :] = _dot_sel(tri, gb)
        a_row = abt_rows(d * GDN_H + head)
        al = alogc_ref[pl.ds(d * GDN_H + head, 1), :]
        db = dtbc_ref[pl.ds(d * GDN_H + head, 1), :]
        g_rows = -jnp.exp(al) * _softplus(a_row + db)
        tri_c = jnp.where((ri >= ci) if rev else (ri <= ci), 1.0, 0.0).astype(BF16)
        gcrow_s[d] = _dot_sel_right(g_rows, tri_c)

    osum[...] = jnp.zeros_like(osum)
    eye = jnp.where(ri == ci, 1.0, 0.0)
    if with_s0:
        s_init = (s0_ref[0, 0, 0], s0_ref[0, 1, 0])
    else:
        s_init = (jnp.zeros((GDN_DK, GDN_DV), F32), jnp.zeros((GDN_DK, GDN_DV), F32))

    def body(i, carry):
        out = []
        for d in range(2):
            rev = d == 1
            n = (n_chunks - 1 - i) if rev else i
            rows = pl.ds(pl.multiple_of(n * c, c), c)
            gcb = gcb_s[d, rows, :]
            last = pl.multiple_of(n * c, c) if rev else pl.multiple_of(n * c, c) + c - 1
            gl = gcb_s[d, pl.ds(last, 1), :]
            o, s_new = _gdn_chunk(qs[rows, :], ks[rows, :], vs[rows, :], gcb, gcrow_s[d, pl.ds(n, 1), :],
                                  beta_s[d, rows, :], gl, carry[d], rev, eye)
            osum[rows, :] = osum[rows, :] + o
            out.append(s_new)
        return tuple(out)

    s_f, s_b = lax.fori_loop(0, n_chunks, body, s_init)
    o = osum[...]
    y = o * lax.rsqrt(jnp.mean(o * o, axis=-1, keepdims=True) + EPS) * ng_ref[...]
    o_ref[...] = (y * _silu(z_ref[...].astype(F32))).astype(o_ref.dtype)
    if not with_s0:
        st_ref[0, 0, 0] = s_f
        st_ref[0, 1, 0] = s_b


def _dot_sel_right(x, sel_bf16):
    hi, mid, lo = _split3(x)
    return _dot(hi, sel_bf16) + _dot(mid, sel_bf16) + _dot(lo, sel_bf16)


def _gdn_call(qkv, z, ab, abt, mix_prev, p, t, nseq, row0, s0=None):
    c = GDN_CHUNK
    n_chunks = t // c
    n_pad = max(n_chunks, SUBLANES)
    if n_pad != n_chunks:
        abt = jnp.pad(abt, ((0, 0), (0, 0), (0, n_pad - n_chunks), (0, 0)))
    blk0 = row0 // t
    nqk = GDN_H * GDN_DK // LANES
    in_specs = [pl.BlockSpec((t, LANES), lambda s, h: (blk0 + s, h)),
                pl.BlockSpec((t, LANES), lambda s, h: (blk0 + s, nqk + h)),
                pl.BlockSpec((t, LANES), lambda s, h: (blk0 + s, 2 * nqk + h)),
                pl.BlockSpec((t, LANES), lambda s, h: (blk0 + s, h)),
                pl.BlockSpec((t, LANES), lambda s, h: (blk0 + s, 0)),
                pl.BlockSpec((4 * GDN_H, 1, n_pad, c), lambda s, h: (0, s, 0, 0)),
                pl.BlockSpec((CONV_W, LANES), lambda s, h: (0, h)),
                pl.BlockSpec((CONV_W, LANES), lambda s, h: (0, nqk + h)),
                pl.BlockSpec((CONV_W, LANES), lambda s, h: (0, 2 * nqk + h)),
                pl.BlockSpec((1, LANES), lambda s, h: (0, h)),
                pl.BlockSpec((1, LANES), lambda s, h: (0, nqk + h)),
                pl.BlockSpec((1, LANES), lambda s, h: (0, 2 * nqk + h)),
                pl.BlockSpec((1, LANES), lambda s, h: (0, 0)),
                pl.BlockSpec((1, LANES), lambda s, h: (0, 0)),
                pl.BlockSpec((4 * GDN_H, 1), lambda s, h: (0, 0)),
                pl.BlockSpec((4 * GDN_H, 1), lambda s, h: (0, 0)),
                pl.BlockSpec((1, LANES), lambda s, h: (0, 0))]
    args = [qkv, qkv, qkv, z, ab, abt, p["conv_w"], p["conv_w"], p["conv_w"], p["conv_b"], p["conv_b"], p["conv_b"],
            p["alog_row"], p["dtb_row"], p["alog_col"], p["dtb_col"], p["norm_g"]]
    scratch = [pltpu.VMEM((t + 2 * SUBLANES, LANES), F32)] + [pltpu.VMEM((t, LANES), F32)] * 4 + [
        pltpu.VMEM((2, t, LANES), F32), pltpu.VMEM((2, t, LANES), F32), pltpu.VMEM((2, n_pad, c), F32)]
    mix_shape = jax.ShapeDtypeStruct((N_TOK, 2 * D), BF16)
    ncol0 = D // LANES
    out_mix_spec = pl.BlockSpec((t, LANES), lambda s, h: (blk0 + s, ncol0 + h))
    if s0 is not None:
        in_specs.append(pl.BlockSpec((1, 2, 1, GDN_DK, GDN_DV), lambda s, h: (s, 0, h, 0, 0)))
        args.append(s0)
        in_specs.append(pl.BlockSpec(memory_space=pl.ANY))
        args.append(mix_prev)
        kern = lambda *r: _gdn_kernel(*r[:18], *r[19:], t=t, with_s0=True)
        return pl.pallas_call(
            kern, grid=(nseq, GDN_H), in_specs=in_specs, out_specs=out_mix_spec, out_shape=mix_shape,
            scratch_shapes=scratch, input_output_aliases={len(args) - 1: 0},
            compiler_params=_params("arbitrary", "arbitrary"),
        )(*args)
    in_specs.append(pl.BlockSpec(memory_space=pl.ANY))
    args.append(mix_prev)
    kern = lambda *r: _gdn_kernel(*r[:17], *r[18:], t=t, with_s0=False)
    return pl.pallas_call(
        kern, grid=(nseq, GDN_H), in_specs=in_specs,
        out_specs=[out_mix_spec, pl.BlockSpec((1, 2, 1, GDN_DK, GDN_DV), lambda s, h: (s, 0, h, 0, 0))],
        out_shape=[mix_shape, jax.ShapeDtypeStruct((nseq, 2, GDN_H, GDN_DK, GDN_DV), F32)],
        scratch_shapes=scratch, input_output_aliases={len(args) - 1: 0},
        compiler_params=_params("arbitrary", "arbitrary"),
    )(*args)


def _gla_kernel(*refs, t, with_s0):
    if with_s0:
        (q_ref, k_ref, v_ref, z_ref, lr_ref, w2_ref, b2_ref, ng_ref, s0_ref,
         o_ref, osum, gc_s) = refs
    else:
        (q_ref, k_ref, v_ref, z_ref, lr_ref, w2_ref, b2_ref, ng_ref,
         o_ref, st_ref, osum, gc_s) = refs
    c = GLA_CHUNK
    n_chunks = t // c
    slab = 256
    bi = lax.broadcasted_iota(jnp.int32, (slab, slab), 0)
    bj = lax.broadcasted_iota(jnp.int32, (slab, slab), 1)
    same_chunk = (bi // c) == (bj // c)
    for d in range(2):
        rev = d == 1
        tri = jnp.where(same_chunk & ((bi <= bj) if rev else (bi >= bj)), 1.0, 0.0).astype(BF16)
        for r in range(t // slab):
            lr = lr_ref[pl.ds(r * slab, slab), :]
            pre = _dot_hp(lr, w2_ref[d, 0]) + b2_ref[d]
            glog = (jnp.minimum(pre, 0.0) - jnp.log1p(jnp.exp(-jnp.abs(pre)))) * (1.0 / GLA_TAU)
            gc_s[d, pl.ds(r * slab, slab), :] = _dot_sel(tri, glog)

    osum[...] = jnp.zeros_like(osum)
    ii = lax.broadcasted_iota(jnp.int32, (c, c), 0)
    jj = lax.broadcasted_iota(jnp.int32, (c, c), 1)
    if with_s0:
        s_init = (s0_ref[0, 0, 0].T, s0_ref[0, 1, 0].T)
    else:
        s_init = (jnp.zeros((GLA_DV, GLA_DK), F32), jnp.zeros((GLA_DV, GLA_DK), F32))

    def body(i, carry):
        out = []
        for d in range(2):
            rev = d == 1
            n = (n_chunks - 1 - i) if rev else i
            base = pl.multiple_of(n * c, c)
            rows = pl.ds(base, c)
            q = q_ref[rows, :].astype(F32) * (GLA_DK ** -0.5)
            k = k_ref[rows, :].astype(F32)
            v = v_ref[rows, :]
            gc = gc_s[d, rows, :]
            gl = gc_s[d, pl.ds(base if rev else base + c - 1, 1), :]
            gm = gc_s[d, pl.ds(base + c // 2, 1), :]
            incl = (ii <= jj) if rev else (ii >= jj)
            a_mat = jnp.where(incl, _dot_nt((q * jnp.exp(gc - gm)).astype(BF16),
                                            (k * jnp.exp(gm - gc)).astype(BF16)), 0.0)
            st = carry[d]
            o = _dot_nt((q * jnp.exp(gc)).astype(BF16), st.astype(BF16)) + _dot(a_mat.astype(BF16), v)
            k_dec = (k * jnp.exp(gl - gc)).astype(BF16)
            out.append(st * jnp.exp(gl) + _dot_tn(v, k_dec))
            osum[rows, :] = osum[rows, :] + o
        return tuple(out)

    s_f, s_b = lax.fori_loop(0, n_chunks, body, s_init)
    o = osum[...]
    y = o * lax.rsqrt(jnp.mean(o * o, axis=-1, keepdims=True) + EPS) * ng_ref[...]
    o_ref[...] = (y * _silu(z_ref[...].astype(F32))).astype(o_ref.dtype)
    if not with_s0:
        st_ref[0, 0, 0] = s_f.T
        st_ref[0, 1, 0] = s_b.T


def _gla_call(q, k, v, z, lr, mix_prev, p, t, nseq, row0, s0=None):
    blk0 = row0 // t
    vb = GLA_DV // LANES
    in_specs = [pl.BlockSpec((t, GLA_DK), lambda s, h: (blk0 + s, h)),
                pl.BlockSpec((t, GLA_DK), lambda s, h: (blk0 + s, h)),
                pl.BlockSpec((t, GLA_DV), lambda s, h: (blk0 + s, h)),
                pl.BlockSpec((t, GLA_DV), lambda s, h: (blk0 + s, h)),
                pl.BlockSpec((t, LANES), lambda s, h: (blk0 + s, 0)),
                pl.BlockSpec((2, 1, LANES, GLA_DK), lambda s, h: (0, h, 0, 0)),
                pl.BlockSpec((2, 1, GLA_DK), lambda s, h: (0, 0, h)),
                pl.BlockSpec((1, GLA_DV), lambda s, h: (0, 0))]
    args = [q, k, v, z, lr, p["w2"], p["b2"], p["norm_g"]]
    scratch = [pltpu.VMEM((t, GLA_DV), F32), pltpu.VMEM((2, t, GLA_DK), F32)]
    mix_shape = jax.ShapeDtypeStruct((N_TOK, D), BF16)
    out_mix_spec = pl.BlockSpec((t, GLA_DV), lambda s, h: (blk0 + s, h))
    st_spec = pl.BlockSpec((1, 2, 1, GLA_DK, GLA_DV), lambda s, h: (s, 0, h, 0, 0))
    if s0 is not None:
        in_specs += [st_spec, pl.BlockSpec(memory_space=pl.ANY)]
        args += [s0, mix_prev]
        kern = lambda *r: _gla_kernel(*r[:9], *r[10:], t=t, with_s0=True)
        return pl.pallas_call(
            kern, grid=(nseq, GLA_H), in_specs=in_specs, out_specs=out_mix_spec, out_shape=mix_shape,
            scratch_shapes=scratch, input_output_aliases={len(args) - 1: 0},
            compiler_params=_params("arbitrary", "arbitrary"),
        )(*args)
    return pl.pallas_call(
        functools.partial(_gla_kernel, t=t, with_s0=False),
        grid=(nseq, GLA_H), in_specs=in_specs,
        out_specs=[out_mix_spec, st_spec],
        out_shape=[mix_shape, jax.ShapeDtypeStruct((nseq, 2, GLA_H, GLA_DK, GLA_DV), F32)],
        scratch_shapes=scratch,
        compiler_params=_params("arbitrary", "arbitrary"),
    )(*args)


def _route(sel, s):
    scores = []
    for g in range(N_GROUPS):
        m = sel[g * GROUP_SZ:(g + 1) * GROUP_SZ]
        best = None
        for a in range(GROUP_SZ):
            for b in range(a + 1, GROUP_SZ):
                pair = m[a] + m[b]
                best = pair if best is None else jnp.maximum(best, pair)
        scores.append(best)
    gbest = jnp.zeros_like(scores[0], dtype=jnp.int32)
    top = scores[0]
    for g in range(1, N_GROUPS):
        better = scores[g] > top
        gbest = jnp.where(better, g, gbest)
        top = jnp.where(better, scores[g], top)
    picked = []
    for e in range(N_EXPERTS):
        g = e // GROUP_SZ
        rank = jnp.zeros_like(gbest)
        for m in range(g * GROUP_SZ, (g + 1) * GROUP_SZ):
            if m == e:
                continue
            ahead = (sel[m] > sel[e]) | (sel[m] == sel[e]) if m < e else (sel[m] > sel[e])
            rank = rank + ahead.astype(jnp.int32)
        picked.append(jnp.where((gbest == g) & (rank < 2), s[e], 0.0))
    total = picked[0]
    for e in range(1, N_EXPERTS):
        total = total + picked[e]
    return [p / total for p in picked]


def _outproj_kernel(mix_ref, x_ref, mod_ref, w_ref, g_ref, rwt_ref, rb_ref, x1_ref, h2_ref, gates_ref):
    gate1 = mod_ref[0, :, 2 * D:3 * D]
    x1 = x_ref[...] + gate1 * _dot(mix_ref[...], w_ref[...])
    x1_ref[...] = x1
    y = x1 * lax.rsqrt(jnp.mean(x1 * x1, axis=-1, keepdims=True) + EPS) * g_ref[...]
    h2 = y * (1.0 + mod_ref[0, :, 4 * D:5 * D]) + mod_ref[0, :, 3 * D:4 * D]
    h2_ref[...] = h2.astype(BF16)
    logits = _dot_hp(rwt_ref[...], h2, dot=_dot_nt)
    s_all = _sigmoid(logits)
    sel_all = s_all + rb_ref[...]
    s = [s_all[e:e + 1, :] for e in range(N_EXPERTS)]
    sel = [sel_all[e:e + 1, :] for e in range(N_EXPERTS)]
    gates_t = jnp.concatenate(_route(sel, s), axis=0)
    ri = lax.broadcasted_iota(jnp.int32, (N_EXPERTS, LANES), 0)
    ci = lax.broadcasted_iota(jnp.int32, (N_EXPERTS, LANES), 1)
    eye = jnp.where(ri == ci, 1.0, 0.0).astype(BF16)
    hi, mid, lo = _split3(gates_t)
    gates_ref[...] = _dot_tn(hi, eye) + _dot_tn(mid, eye) + _dot_tn(lo, eye)


def _outproj(mix, x, mod_l, w_out, gain2, router_wt, router_b):
    kdim = mix.shape[1]
    return pl.pallas_call(
        _outproj_kernel,
        grid=(N_TOK // TM,),
        in_specs=[pl.BlockSpec((TM, kdim), lambda i: (i, 0)),
                  pl.BlockSpec((TM, D), lambda i: (i, 0)),
                  pl.BlockSpec((1, 1, 6 * D), lambda i: (_mod_row(i, TM), 0, 0)),
                  pl.BlockSpec((kdim, D), lambda i: (0, 0)),
                  pl.BlockSpec((1, D), lambda i: (0, 0)),
                  pl.BlockSpec((N_EXPERTS, D), lambda i: (0, 0)),
                  pl.BlockSpec((N_EXPERTS, 1), lambda i: (0, 0))],
        out_specs=[pl.BlockSpec((TM, D), lambda i: (i, 0)),
                   pl.BlockSpec((TM, D), lambda i: (i, 0)),
                   pl.BlockSpec((TM, LANES), lambda i: (i, 0))],
        out_shape=[jax.ShapeDtypeStruct((N_TOK, D), F32),
                   jax.ShapeDtypeStruct((N_TOK, D), BF16),
                   jax.ShapeDtypeStruct((N_TOK, LANES), F32)],
        compiler_params=_params("arbitrary"),
    )(mix, x, mod_l.reshape(SUBLANES, 1, 6 * D), w_out, gain2.reshape(1, D), router_wt,
      router_b.reshape(N_EXPERTS, 1))


def _moe_kernel(h_ref, gates_ref, x1_ref, mod_ref, wg_ref, wu_ref, wd_ref, fg_ref, o_ref, acc, *, final_norm):
    e = pl.program_id(1)

    @pl.when(e == 0)
    def _():
        acc[...] = jnp.zeros_like(acc)

    h = h_ref[...]
    gates = gates_ref[...]
    lane = lax.broadcasted_iota(jnp.int32, gates.shape, 1)
    gate = jnp.sum(jnp.where(lane == e, gates, 0.0), axis=-1, keepdims=True)
    hid = _silu(_dot(h, wg_ref[0, 0].astype(BF16))) * _dot(h, wu_ref[0, 0].astype(BF16)) * gate
    acc[...] += _dot(hid.astype(BF16), wd_ref[0, 0].astype(BF16))

    @pl.when(e == N_EXPERTS - 1)
    def _():
        out = x1_ref[...] + mod_ref[0, :, 5 * D:6 * D] * acc[...]
        if final_norm:
            out = out * lax.rsqrt(jnp.mean(out * out, axis=-1, keepdims=True) + EPS) * fg_ref[...]
        o_ref[...] = out


def _moe(h2, gates, x1, mod_l, w_gate, w_up, w_down, layer, final_gain, final_norm):
    tm = TM_MOE
    return pl.pallas_call(
        functools.partial(_moe_kernel, final_norm=final_norm),
        grid=(N_TOK // tm, N_EXPERTS),
        in_specs=[pl.BlockSpec((tm, D), lambda i, e: (i, 0)),
                  pl.BlockSpec((tm, LANES), lambda i, e: (i, 0)),
                  pl.BlockSpec((tm, D), lambda i, e: (i, 0)),
                  pl.BlockSpec((1, 1, 6 * D), lambda i, e: (_mod_row(i, tm), 0, 0)),
                  pl.BlockSpec((1, 1, D, D_EXPERT), lambda i, e: (layer, e, 0, 0)),
                  pl.BlockSpec((1, 1, D, D_EXPERT), lambda i, e: (layer, e, 0, 0)),
                  pl.BlockSpec((1, 1, D_EXPERT, D), lambda i, e: (layer, e, 0, 0)),
                  pl.BlockSpec((1, D), lambda i, e: (0, 0))],
        out_specs=pl.BlockSpec((tm, D), lambda i, e: (i, 0)),
        out_shape=jax.ShapeDtypeStruct((N_TOK, D), F32),
        scratch_shapes=[pltpu.VMEM((tm, D), F32)],
        compiler_params=_params("arbitrary", "arbitrary"),
    )(h2, gates, x1, mod_l.reshape(SUBLANES, 1, 6 * D), w_gate, w_up, w_down, final_gain.reshape(1, D))


def _to_col_major(x):
    b, t, d = x.shape
    return x.reshape(b, t // GRID_W, GRID_W, d).transpose(0, 2, 1, 3).reshape(b, t, d)


def _to_row_major(x):
    b, t, d = x.shape
    return x.reshape(b, GRID_W, t // GRID_W, d).transpose(0, 2, 1, 3).reshape(b, t, d)


def _layer_ab(x, mod_l, state_lru, state_gdn, norm1_g, w_in, lru_conv_w, lru_conv_b, lru_wa, lru_ba, lru_wx,
              lru_bx, lru_lam, gdn_conv_w, gdn_conv_b, gdn_a_log, gdn_dt_bias, gdn_norm_g):
    w = w_in.astype(BF16)
    n_qkv = 2 * GDN_H * GDN_DK + GDN_H * GDN_DV
    o = 0
    w_y, o = w[:, o:o + D], o + D
    w_x, o = w[:, o:o + D], o + D
    w_qkv, o = w[:, o:o + n_qkv], o + n_qkv
    w_z, o = w[:, o:o + GDN_H * GDN_DV], o + GDN_H * GDN_DV
    w_ab = w[:, o:]
    w_ab_pad = jnp.pad(w_ab, ((0, 0), (0, LANES - w_ab.shape[1])))
    ybr, xbr, qkv, z, ab, abt = _inproj(x, mod_l, norm1_g, [w_y, w_x, w_qkv, w_z, w_ab_pad],
                                        [BF16, BF16, BF16, BF16, F32], w_t=w_ab.T)

    nb = D // LRU_BLOCK
    wg = jnp.stack([lru_wa[0], lru_wx[0], lru_wa[1], lru_wx[1]], axis=1)
    wg = wg.transpose(0, 2, 1, 3).reshape(nb, LRU_BLOCK, 4 * LRU_BLOCK).astype(BF16)
    bg = jnp.stack([lru_ba[0], lru_bx[0], lru_ba[1], lru_bx[1]], axis=0)
    bg = bg.reshape(4, nb, LRU_BLOCK).transpose(1, 0, 2).reshape(nb, 1, 4 * LRU_BLOCK)
    lam = lru_lam.reshape(2, nb, LRU_BLOCK).transpose(1, 0, 2)
    lru_p = dict(conv_w=lru_conv_w, conv_b=lru_conv_b.reshape(1, D), wg=wg, bg=bg, lam=lam)
    mix, lru_state = _lru_call(xbr, ybr, None, lru_p, T_PROMPT, N_PROMPT_SEQ, 0)
    mix = _lru_call(xbr, ybr, mix, lru_p, T_SAMPLE, N_SAMPLE_SEQ, N_PROMPT, h0=state_lru)

    pad16 = lambda v: jnp.pad(v.reshape(1, 2 * GDN_H), ((0, 0), (0, LANES - 2 * GDN_H)))
    col32 = lambda v: jnp.pad(v.reshape(2 * GDN_H, 1), ((0, 2 * GDN_H), (0, 0)))
    gdn_p = dict(conv_w=gdn_conv_w, conv_b=gdn_conv_b.reshape(1, n_qkv), alog_row=pad16(gdn_a_log),
                 dtb_row=pad16(gdn_dt_bias), alog_col=col32(gdn_a_log), dtb_col=col32(gdn_dt_bias),
                 norm_g=gdn_norm_g.reshape(1, GDN_DV))
    abt_p = abt[:, :N_PROMPT].reshape(4 * GDN_H, N_PROMPT_SEQ, T_PROMPT // GDN_CHUNK, GDN_CHUNK)
    abt_s = abt[:, N_PROMPT:].reshape(4 * GDN_H, N_SAMPLE_SEQ, T_SAMPLE // GDN_CHUNK, GDN_CHUNK)
    mix, gdn_state = _gdn_call(qkv, z, ab, abt_p, mix, gdn_p, T_PROMPT, N_PROMPT_SEQ, 0)
    mix = _gdn_call(qkv, z, ab, abt_s, mix, gdn_p, T_SAMPLE, N_SAMPLE_SEQ, N_PROMPT, s0=state_gdn)
    return mix, lru_state, gdn_state


def _layer_c(x, mod_l, state_gla, norm1_g, w_in, w2, b2, norm_g):
    w = w_in.astype(BF16)
    nk = GLA_H * GLA_DK
    nv = GLA_H * GLA_DV
    w_q, w_k, w_v, w_z = w[:, 0:nk], w[:, nk:2 * nk], w[:, 2 * nk:2 * nk + nv], w[:, 2 * nk + nv:2 * nk + 2 * nv]
    w_lr = jnp.pad(w[:, 2 * nk + 2 * nv:], ((0, 0), (0, LANES - 2 * GLA_RANK)))
    xs_cm = _to_col_major(x[N_PROMPT:].reshape(N_SAMPLE_SEQ, T_SAMPLE, D)).reshape(N_SAMPLE, D)
    x_cm = jnp.concatenate([x[:N_PROMPT], xs_cm], axis=0)
    q, k, v, z, lr = _inproj(x_cm, mod_l, norm1_g, [w_q, w_k, w_v, w_z, w_lr], [BF16, BF16, BF16, BF16, F32])
    w2h = w2.reshape(2, GLA_RANK, GLA_H, GLA_DK).transpose(0, 2, 1, 3)
    w2big = jnp.zeros((2, GLA_H, LANES, GLA_DK), F32)
    w2big = w2big.at[0, :, 0:GLA_RANK].set(w2h[0]).at[1, :, GLA_RANK:2 * GLA_RANK].set(w2h[1])
    gla_p = dict(w2=w2big, b2=b2.reshape(2, 1, nk), norm_g=norm_g.reshape(1, GLA_DV))
    mix, gla_state = _gla_call(q, k, v, z, lr, None, gla_p, T_PROMPT, N_PROMPT_SEQ, 0)
    mix = _gla_call(q, k, v, z, lr, mix, gla_p, T_SAMPLE, N_SAMPLE_SEQ, N_PROMPT, s0=state_gla)
    ms_rm = _to_row_major(mix[N_PROMPT:].reshape(N_SAMPLE_SEQ, T_SAMPLE, D)).reshape(N_SAMPLE, D)
    return jnp.concatenate([mix[:N_PROMPT], ms_rm], axis=0), gla_state


def kernel(x_prompt, x_sample, state_lru, state_gdn, state_gla, c, c_ctx, ada_w, ada_b, norm1_g, norm2_g,
           final_norm_g, ab_w_in, lru_conv_w, lru_conv_b, lru_wa, lru_ba, lru_wx, lru_bx, lru_lam, gdn_conv_w,
           gdn_conv_b, gdn_a_log, gdn_dt_bias, gdn_norm_g, ab_w_out, gla_w_in, gla_w2, gla_b2, gla_norm_g,
           gla_w_out, router_w, router_b, moe_w_gate, moe_w_up, moe_w_down):
    x = jnp.concatenate([x_prompt.reshape(N_PROMPT, D), x_sample.reshape(N_SAMPLE, D)], axis=0)
    cvec = jnp.concatenate([c_ctx.reshape(1, D), c, jnp.zeros((SUBLANES - 1 - N_SAMPLE_SEQ, D), F32)], axis=0)
    mod = _modulation(cvec, ada_w, ada_b)
    router_wt = router_w.T

    mix, lru_state, gdn_state = _layer_ab(
        x, mod[0], state_lru[:, 0], state_gdn[:, 0], norm1_g[0], ab_w_in[0], lru_conv_w[0], lru_conv_b[0],
        lru_wa[0], lru_ba[0], lru_wx[0], lru_bx[0], lru_lam[0], gdn_conv_w[0], gdn_conv_b[0], gdn_a_log[0],
        gdn_dt_bias[0], gdn_norm_g[0])
    x1, h2, gates = _outproj(mix, x, mod[0], ab_w_out[0].astype(BF16), norm2_g[0], router_wt, router_b)
    x = _moe(h2, gates, x1, mod[0], moe_w_gate, moe_w_up, moe_w_down, 0, final_norm_g, False)

    mix, gla_state = _layer_c(x, mod[1], state_gla[:, 0], norm1_g[1], gla_w_in[0], gla_w2[0], gla_b2[0],
                              gla_norm_g[0])
    x1, h2, gates = _outproj(mix, x, mod[1], gla_w_out[0].astype(BF16), norm2_g[1], router_wt, router_b)
    y = _moe(h2, gates, x1, mod[1], moe_w_gate, moe_w_up, moe_w_down, 1, final_norm_g, True)

    y_prompt = y[:N_PROMPT].reshape(N_PROMPT_SEQ, T_PROMPT, D)
    y_sample = y[N_PROMPT:].reshape(N_SAMPLE_SEQ, T_SAMPLE, D)
    return (y_prompt, y_sample, lru_state[:, None], gdn_state[:, None], gla_state[:, None])
```

```python
import functools
import math

import jax
import jax.numpy as jnp
from jax import lax
from jax.experimental import pallas as pl
from jax.experimental.pallas import tpu as pltpu

F32 = jnp.float32
BF16 = jnp.bfloat16

D = 1024
N_PROMPT_SEQ, T_PROMPT = 16, 256
N_SAMPLE_SEQ, T_SAMPLE = 4, 2048
N_PROMPT = N_PROMPT_SEQ * T_PROMPT
N_SAMPLE = N_SAMPLE_SEQ * T_SAMPLE
N_TOK = N_PROMPT + N_SAMPLE
GRID_W = 64
GRID_H = T_SAMPLE // GRID_W
EPS = 1e-6
LANES = 128
SUBLANES = 8

LRU_C = 8.0
LRU_BLOCK = 128
CONV_LEFT = 2
CONV_W = 4
GDN_H, GDN_DK, GDN_DV, GDN_CHUNK = 8, 128, 128, 64
GLA_H, GLA_DK, GLA_DV, GLA_CHUNK, GLA_RANK, GLA_TAU = 4, 128, 256, 32, 16, 16.0
N_EXPERTS, N_GROUPS, D_EXPERT = 16, 4, 512
GROUP_SZ = N_EXPERTS // N_GROUPS

TM = 256
TM_MOE = 1024
VMEM_LIMIT = 56 * 1024 * 1024

_NT = (((1,), (1,)), ((), ()))
_TN = (((0,), (0,)), ((), ()))


def _dot(a, b):
    return jnp.dot(a, b, preferred_element_type=F32)


def _dot_nt(a, b):
    return lax.dot_general(a, b, _NT, preferred_element_type=F32)


def _dot_tn(a, b):
    return lax.dot_general(a, b, _TN, preferred_element_type=F32)


def _split2(x):
    hi = x.astype(BF16)
    lo = (x - hi.astype(F32)).astype(BF16)
    return hi, lo


def _split3(x):
    hi = x.astype(BF16)
    r = x - hi.astype(F32)
    mid = r.astype(BF16)
    lo = (r - mid.astype(F32)).astype(BF16)
    return hi, mid, lo


def _dot_sel(sel_bf16, x):
    hi, mid, lo = _split3(x)
    return _dot(sel_bf16, hi) + _dot(sel_bf16, mid) + _dot(sel_bf16, lo)


def _dot_hp(a, b, dot=_dot):
    ah, al = _split2(a)
    bh, bl = _split2(b)
    return dot(ah, bh) + dot(ah, bl) + dot(al, bh)


def _sigmoid(x):
    return 1.0 / (1.0 + jnp.exp(-x))


def _silu(x):
    return x * _sigmoid(x)


def _softplus(x):
    return jnp.maximum(x, 0.0) + jnp.log1p(jnp.exp(-jnp.abs(x)))


def _gelu_tanh(x):
    return 0.5 * x * (1.0 + jnp.tanh(math.sqrt(2.0 / math.pi) * (x + 0.044715 * (x * x * x))))


def _params(*sem):
    return pltpu.CompilerParams(dimension_semantics=sem, vmem_limit_bytes=VMEM_LIMIT)


def _mod_row(i, tm):
    n_prompt_tiles = N_PROMPT // tm
    per_seq = T_SAMPLE // tm
    return jnp.where(i < n_prompt_tiles, 0, 1 + (i - n_prompt_tiles) // per_seq)


def _mod_kernel(c_ref, w_ref, b_ref, o_ref):
    c = c_ref[...]
    o_ref[0] = _dot_hp(_silu(c), w_ref[0]) + b_ref[0]


def _modulation(cvec, ada_w, ada_b):
    depth = ada_w.shape[0]
    n6 = ada_w.shape[2]
    return pl.pallas_call(
        _mod_kernel,
        grid=(depth, n6 // D),
        in_specs=[pl.BlockSpec((SUBLANES, D), lambda l, j: (0, 0)),
                  pl.BlockSpec((1, D, D), lambda l, j: (l, 0, j)),
                  pl.BlockSpec((1, 1, D), lambda l, j: (l, 0, j))],
        out_specs=pl.BlockSpec((1, SUBLANES, D), lambda l, j: (l, 0, j)),
        out_shape=jax.ShapeDtypeStruct((depth, SUBLANES, n6), F32),
        compiler_params=_params("arbitrary", "arbitrary"),
    )(cvec, ada_w, ada_b.reshape(depth, 1, n6))


def _inproj_kernel(x_ref, mod_ref, g_ref, *refs, n_w, has_t):
    w_refs = refs[:n_w]
    o_refs = refs[n_w + has_t:2 * n_w + has_t]
    x = x_ref[...]
    y = x * lax.rsqrt(jnp.mean(x * x, axis=-1, keepdims=True) + EPS) * g_ref[...]
    shift = mod_ref[0, :, 0:D]
    scale = mod_ref[0, :, D:2 * D]
    h = (y * (1.0 + scale) + shift).astype(BF16)
    for w_ref, o_ref in zip(w_refs, o_refs):
        o_ref[...] = _dot(h, w_ref[...]).astype(o_ref.dtype)
    if has_t:
        wt_ref = refs[n_w]
        ot_ref = refs[2 * n_w + 1]
        ot_ref[...] = _dot_nt(wt_ref[...], h)


def _inproj(x, mod_l, gain, weights, out_dtypes, w_t=None):
    n_w = len(weights)
    in_specs = [pl.BlockSpec((TM, D), lambda i: (i, 0)),
                pl.BlockSpec((1, 1, 6 * D), lambda i: (_mod_row(i, TM), 0, 0)),
                pl.BlockSpec((1, D), lambda i: (0, 0))]
    in_specs += [pl.BlockSpec(w.shape, lambda i: (0, 0)) for w in weights]
    out_specs = [pl.BlockSpec((TM, w.shape[1]), lambda i: (i, 0)) for w in weights]
    out_shape = [jax.ShapeDtypeStruct((N_TOK, w.shape[1]), dt) for w, dt in zip(weights, out_dtypes)]
    args = [x, mod_l.reshape(SUBLANES, 1, 6 * D), gain.reshape(1, D)] + list(weights)
    if w_t is not None:
        in_specs.append(pl.BlockSpec(w_t.shape, lambda i: (0, 0)))
        out_specs.append(pl.BlockSpec((w_t.shape[0], TM), lambda i: (0, i)))
        out_shape.append(jax.ShapeDtypeStruct((w_t.shape[0], N_TOK), F32))
        args.append(w_t)
    return pl.pallas_call(
        functools.partial(_inproj_kernel, n_w=n_w, has_t=int(w_t is not None)),
        grid=(N_TOK // TM,),
        in_specs=in_specs, out_specs=out_specs, out_shape=out_shape,
        compiler_params=_params("arbitrary"),
    )(*args)


def _conv_from_padded(xp_ref, t, cw, cb):
    acc = cb
    for j in range(CONV_W):
        acc = acc + cw[j:j + 1, :] * xp_ref[pl.ds(SUBLANES - CONV_LEFT + j, t), :]
    return acc


def _fill_padded(xp_ref, x, t):
    zeros = jnp.zeros((SUBLANES, xp_ref.shape[1]), F32)
    xp_ref[pl.ds(0, SUBLANES), :] = zeros
    xp_ref[pl.ds(SUBLANES + t, SUBLANES), :] = zeros
    xp_ref[pl.ds(SUBLANES, t), :] = x


_SCAN_ROWS = 32


def _scan_tile(a, b, carry, rev):
    row = lax.broadcasted_iota(jnp.int32, a.shape, 0)
    for s in (1, 2, 4):
        if rev:
            a_s = pltpu.roll(a, SUBLANES - s, 0)
            b_s = pltpu.roll(b, SUBLANES - s, 0)
            m = row < SUBLANES - s
        else:
            a_s = pltpu.roll(a, s, 0)
            b_s = pltpu.roll(b, s, 0)
            m = row >= s
        b = jnp.where(m, a * b_s + b, b)
        a = jnp.where(m, a * a_s, a)
    h = a * carry + b
    return h, (h[0:1] if rev else h[SUBLANES - 1:SUBLANES])


def _lru_kernel(*refs, t, with_h0):
    if with_h0:
        (x_ref, y_ref, cw_ref, cb_ref, wg_ref, bg_ref, lam_ref, h0_ref,
         o_ref, xp, a_f, b_f, a_b, b_b) = refs
    else:
        (x_ref, y_ref, cw_ref, cb_ref, wg_ref, bg_ref, lam_ref,
         o_ref, st_ref, xp, a_f, b_f, a_b, b_b) = refs
    _fill_padded(xp, x_ref[...].astype(F32), t)
    xc = _conv_from_padded(xp, t, cw_ref[...], cb_ref[...])
    gates = _sigmoid(_dot(xc.astype(BF16), wg_ref[0]) + bg_ref[0])
    sp = _softplus(-lam_ref[0])
    for d, (a_ref, b_ref) in enumerate(((a_f, b_f), (a_b, b_b))):
        r = gates[:, (2 * d) * LRU_BLOCK:(2 * d + 1) * LRU_BLOCK]
        ig = gates[:, (2 * d + 1) * LRU_BLOCK:(2 * d + 2) * LRU_BLOCK]
        log_a = -LRU_C * r * sp[d:d + 1, :]
        a = jnp.exp(log_a)
        a_ref[...] = a
        b_ref[...] = jnp.sqrt(1.0 - a * a) * ig * xc

    n_it = t // _SCAN_ROWS
    if with_h0:
        c0 = (h0_ref[0, 0:1, :], h0_ref[0, 1:2, :])
    else:
        c0 = (jnp.zeros((1, LRU_BLOCK), F32), jnp.zeros((1, LRU_BLOCK), F32))

    def body(i, carry):
        cf, cb_ = carry
        base_f = pl.multiple_of(i * _SCAN_ROWS, _SCAN_ROWS)
        base_b = pl.multiple_of((n_it - 1 - i) * _SCAN_ROWS, _SCAN_ROWS)
        for k in range(_SCAN_ROWS // SUBLANES):
            rf = pl.ds(base_f + k * SUBLANES, SUBLANES)
            h, cf = _scan_tile(a_f[rf, :], b_f[rf, :], cf, False)
            b_f[rf, :] = h
            rb = pl.ds(base_b + _SCAN_ROWS - (k + 1) * SUBLANES, SUBLANES)
            h, cb_ = _scan_tile(a_b[rb, :], b_b[rb, :], cb_, True)
            b_b[rb, :] = h
        return cf, cb_

    cf, cb_ = lax.fori_loop(0, n_it, body, c0)
    o_ref[...] = (_gelu_tanh(y_ref[...].astype(F32)) * (b_f[...] + b_b[...])).astype(o_ref.dtype)
    if not with_h0:
        st_ref[0, 0:1, :] = cf
        st_ref[0, 1:2, :] = cb_


def _lru_call(xbr, ybr, mix_prev, p, t, nseq, row0, h0=None):
    nb = D // LRU_BLOCK
    blk0 = row0 // t
    tok = lambda s, j: (blk0 + s, j)
    in_specs = [pl.BlockSpec((t, LRU_BLOCK), tok),
                pl.BlockSpec((t, LRU_BLOCK), tok),
                pl.BlockSpec((CONV_W, LRU_BLOCK), lambda s, j: (0, j)),
                pl.BlockSpec((1, LRU_BLOCK), lambda s, j: (0, j)),
                pl.BlockSpec((1, LRU_BLOCK, 4 * LRU_BLOCK), lambda s, j: (j, 0, 0)),
                pl.BlockSpec((1, 1, 4 * LRU_BLOCK), lambda s, j: (j, 0, 0)),
                pl.BlockSpec((1, 2, LRU_BLOCK), lambda s, j: (j, 0, 0))]
    args = [xbr, ybr, p["conv_w"], p["conv_b"], p["wg"], p["bg"], p["lam"]]
    scratch = [pltpu.VMEM((t + 2 * SUBLANES, LRU_BLOCK), F32)] + [pltpu.VMEM((t, LRU_BLOCK), F32)] * 4
    mix_shape = jax.ShapeDtypeStruct((N_TOK, 2 * D), BF16)
    out_mix_spec = pl.BlockSpec((t, LRU_BLOCK), tok)
    if h0 is not None:
        in_specs.append(pl.BlockSpec((1, 2, LRU_BLOCK), lambda s, j: (s, 0, j)))
        args.append(h0)
        in_specs.append(pl.BlockSpec(memory_space=pl.ANY))
        args.append(mix_prev)
        kern = lambda *r: _lru_kernel(*r[:8], *r[9:], t=t, with_h0=True)
        return pl.pallas_call(
            kern, grid=(nseq, nb), in_specs=in_specs, out_specs=out_mix_spec, out_shape=mix_shape,
            scratch_shapes=scratch, input_output_aliases={len(args) - 1: 0},
            compiler_params=_params("arbitrary", "arbitrary"),
        )(*args)
    return pl.pallas_call(
        functools.partial(_lru_kernel, t=t, with_h0=False),
        grid=(nseq, nb), in_specs=in_specs,
        out_specs=[out_mix_spec, pl.BlockSpec((1, 2, LRU_BLOCK), lambda s, j: (s, 0, j))],
        out_shape=[mix_shape, jax.ShapeDtypeStruct((nseq, 2, D), F32)],
        scratch_shapes=scratch,
        compiler_params=_params("arbitrary", "arbitrary"),
    )(*args)


GDN_GROUP = 4
GDN_SLAB = GDN_GROUP * GDN_CHUNK


def _compact_select(x, c):
    g = x.shape[0] // c
    lane_blk = lax.broadcasted_iota(jnp.int32, (c, g * c), 1) // c
    out = x[0:c]
    for p in range(1, g):
        out = jnp.where(lane_blk == p, x[p * c:(p + 1) * c], out)
    return out


def _block_diag(xc, same_block):
    g = xc.shape[1] // xc.shape[0]
    return jnp.where(same_block, jnp.concatenate([xc] * g, axis=0), jnp.zeros((), xc.dtype))


def _dot_hp_bd(a, bc, same_block):
    ah, al = _split2(a)
    bh, bl = _split2(bc)
    bdh = _block_diag(bh, same_block)
    return _dot(ah, bdh) + _dot(al, bdh) + _dot(ah, _block_diag(bl, same_block))


def _tri_inverse_compact(a_cs, eye_c, same_block):
    c = eye_c.shape[0]
    p_cs = [-a_c for a_c in a_cs]
    t_cs = [eye_c + p_c for p_c in p_cs]
    p_cs = [_dot_hp_bd(p_c, p_c, same_block) for p_c in p_cs]
    levels = int(math.log2(c)) - 1
    for lvl in range(1, levels):
        xs = [_dot_hp_bd(jnp.concatenate([t_c, p_c], axis=0), p_c, same_block) for t_c, p_c in zip(t_cs, p_cs)]
        t_cs = [t_c + x[0:c] for t_c, x in zip(t_cs, xs)]
        p_cs = [x[c:2 * c] for x in xs]
    return [t_c + _dot_hp_bd(t_c, p_c, same_block) for t_c, p_c in zip(t_cs, p_cs)]


def _gdn_prepare_group(q, k, v, gcb, gcrow, beta, gl, masks):
    c = GDN_CHUNK
    same_block, eye_c, incl_c, strict_c = masks
    kb = [k * beta[d] for d in range(2)]
    prod = _dot_nt(jnp.concatenate([kb[0], kb[1], q], axis=0).astype(BF16), k.astype(BF16))
    n = q.shape[0]
    qk_c = _compact_select(prod[2 * n:3 * n], c)
    low_half = lax.broadcasted_iota(jnp.int32, (c, LANES), 1) < c
    decay, a_cs = [], []
    for d in range(2):
        gi_c = jnp.concatenate(
            [jnp.where(low_half, gcb[d][(2 * h) * c:(2 * h + 1) * c], gcb[d][(2 * h + 1) * c:(2 * h + 2) * c])
             for h in range(GDN_GROUP // 2)], axis=1)
        diff = gi_c - gcrow[d]
        decay.append(jnp.where(incl_c[d], jnp.exp(jnp.where(incl_c[d], diff, 0.0)), 0.0))
        a_cs.append(jnp.where(strict_c[d], _compact_select(prod[d * n:(d + 1) * n], c) * decay[d], 0.0))
    t_cs = _tri_inverse_compact(a_cs, eye_c, same_block)
    out = []
    for d in range(2):
        eg = jnp.exp(gcb[d])
        rhs = jnp.concatenate([v * beta[d], kb[d] * eg], axis=1).astype(BF16)
        uw = _dot(_block_diag(t_cs[d].astype(BF16), same_block), rhs)
        k_dec_t = (k * jnp.exp(gl[d] - gcb[d])).T
        out.append((uw[:, 0:GDN_DV], uw[:, GDN_DV:].astype(BF16), (q * eg).astype(BF16), qk_c * decay[d], k_dec_t))
    return out


def _gdn_kernel(*refs, t, with_s0):
    if with_s0:
        (q_ref, k_ref, v_ref, z_ref, ab_ref, abt_ref, cwq_ref, cwk_ref, cwv_ref, cbq_ref, cbk_ref, cbv_ref,
         alog_ref, dtb_ref, alogc_ref, dtbc_ref, ng_ref, s0_ref,
         o_ref, xp, qs, ks, vs, osum, gcb_s, beta_s, gcrow_s, u_s, wq_s, qkkd_s) = refs
    else:
        (q_ref, k_ref, v_ref, z_ref, ab_ref, abt_ref, cwq_ref, cwk_ref, cwv_ref, cbq_ref, cbk_ref, cbv_ref,
         alog_ref, dtb_ref, alogc_ref, dtbc_ref, ng_ref,
         o_ref, st_ref, xp, qs, ks, vs, osum, gcb_s, beta_s, gcrow_s, u_s, wq_s, qkkd_s) = refs
    c = GDN_CHUNK
    n_chunks = t // c
    head = pl.program_id(1)

    for x_ref, cw_ref, cb_ref, dst, kind in ((q_ref, cwq_ref, cbq_ref, qs, "q"), (k_ref, cwk_ref, cbk_ref, ks, "k"),
                                             (v_ref, cwv_ref, cbv_ref, vs, "v")):
        _fill_padded(xp, x_ref[...].astype(F32), t)
        y = _silu(_conv_from_padded(xp, t, cw_ref[...], cb_ref[...]))
        if kind != "v":
            y = y * lax.rsqrt(jnp.sum(y * y, axis=-1, keepdims=True) + EPS)
        if kind == "q":
            y = y * (GDN_DK ** -0.5)
        dst[...] = y

    ab = ab_ref[...]
    lane = lax.broadcasted_iota(jnp.int32, ab.shape, 1)
    gall = jnp.where(lane < 2 * GDN_H, -jnp.exp(alog_ref[...]) * _softplus(ab + dtb_ref[...]), _sigmoid(ab))
    pick = lambda idx: jnp.sum(jnp.where(lane == idx, gall, 0.0), axis=-1, keepdims=True)
    abt_rows = lambda idx: abt_ref[idx, 0]
    slab = GDN_SLAB
    bi = lax.broadcasted_iota(jnp.int32, (slab, slab), 0)
    bj = lax.broadcasted_iota(jnp.int32, (slab, slab), 1)
    same_chunk = (bi // c) == (bj // c)
    tri = [jnp.where(same_chunk & ((bi <= bj) if d == 1 else (bi >= bj)), 1.0, 0.0).astype(BF16) for d in range(2)]
    for d in range(2):
        g_col = pick(d * GDN_H + head)
        beta_s[d] = jnp.broadcast_to(pick(2 * GDN_H + d * GDN_H + head), (t, LANES))
        for r in range(t // slab):
            gb = jnp.broadcast_to(g_col[r * slab:(r + 1) * slab], (slab, LANES))
            gcb_s[d, pl.ds(r * slab, slab), :] = _dot_sel(tri[d], gb)
        a_row = abt_rows(d * GDN_H + head)
        al = alogc_ref[pl.ds(d * GDN_H + head, 1), :]
        db = dtbc_ref[pl.ds(d * GDN_H + head, 1), :]
        g_rows = -jnp.exp(al) * _softplus(a_row + db)
        gcrow_s[d] = _dot_sel_right(g_rows, tri[1 - d])

    osum[...] = jnp.zeros_like(osum)
    ci = lax.broadcasted_iota(jnp.int32, (c, slab), 0)
    cj = lax.broadcasted_iota(jnp.int32, (c, slab), 1) % c
    masks = (same_chunk, jnp.where(ci == cj, 1.0, 0.0), (ci >= cj, ci <= cj), (ci > cj, ci < cj))
    if with_s0:
        s_init = (s0_ref[0, 0, 0], s0_ref[0, 1, 0])
    else:
        s_init = (jnp.zeros((GDN_DK, GDN_DV), F32), jnp.zeros((GDN_DK, GDN_DV), F32))

    def chunk_total(d, base):
        return gcb_s[d, pl.ds(base if d == 1 else base + c - 1, 1), :]

    def prepare(i, _):
        base = pl.multiple_of(i * slab, slab)
        rows = pl.ds(base, slab)
        gl = [jnp.concatenate([jnp.broadcast_to(chunk_total(d, base + p * c), (c, LANES)) for p in range(GDN_GROUP)],
                              axis=0) for d in range(2)]
        res = _gdn_prepare_group(
            qs[rows, :], ks[rows, :], vs[rows, :], [gcb_s[d, rows, :] for d in range(2)],
            [gcrow_s[d, pl.ds(i, 1), :] for d in range(2)], [beta_s[d, rows, :] for d in range(2)], gl, masks)
        for d, (u, w, qg, qk_c, k_dec_t) in enumerate(res):
            u_s[d, rows, :] = u
            for p in range(GDN_GROUP):
                n = i * GDN_GROUP + p
                dst = pl.multiple_of(base * 2 + p * 2 * c, 2 * c)
                wq_s[d, pl.ds(dst, c), :] = w[p * c:(p + 1) * c]
                wq_s[d, pl.ds(dst + c, c), :] = qg[p * c:(p + 1) * c]
                qkkd_s[d, n, pl.ds(0, c), :] = qk_c[:, p * c:(p + 1) * c].astype(BF16)
                qkkd_s[d, n, pl.ds(c, GDN_DK), :] = k_dec_t[:, p * c:(p + 1) * c].astype(BF16)
        return 0

    lax.fori_loop(0, t // slab, prepare, 0)

    def body(i, carry):
        out = []
        for d in range(2):
            n = (n_chunks - 1 - i) if d == 1 else i
            base = pl.multiple_of(n * c, c)
            rows = pl.ds(base, c)
            s = carry[d]
            ws_qs = _dot(wq_s[d, pl.ds(pl.multiple_of(n * 2 * c, 2 * c), 2 * c), :], s.astype(BF16))
            v_new = (u_s[d, rows, :] - ws_qs[0:c]).astype(BF16)
            upd = _dot(qkkd_s[d, n], v_new)
            out.append(s * jnp.exp(chunk_total(d, base)) + upd[c:c + GDN_DK])
            osum[rows, :] = osum[rows, :] + ws_qs[c:2 * c] + upd[0:c]
        return tuple(out)

    s_f, s_b = lax.fori_loop(0, n_chunks, body, s_init)
    o = osum[...]
    y = o * lax.rsqrt(jnp.mean(o * o, axis=-1, keepdims=True) + EPS) * ng_ref[...]
    o_ref[...] = (y * _silu(z_ref[...].astype(F32))).astype(o_ref.dtype)
    if not with_s0:
        st_ref[0, 0, 0] = s_f
        st_ref[0, 1, 0] = s_b


def _dot_sel_right(x, sel_bf16):
    hi, mid, lo = _split3(x)
    return _dot(hi, sel_bf16) + _dot(mid, sel_bf16) + _dot(lo, sel_bf16)


def _gdn_call(qkv, z, ab, abt, mix_prev, p, t, nseq, row0, s0=None):
    c = GDN_SLAB
    n_chunks = t // c
    n_pad = max(n_chunks, SUBLANES)
    if n_pad != n_chunks:
        abt = jnp.pad(abt, ((0, 0), (0, 0), (0, n_pad - n_chunks), (0, 0)))
    blk0 = row0 // t
    nqk = GDN_H * GDN_DK // LANES
    in_specs = [pl.BlockSpec((t, LANES), lambda s, h: (blk0 + s, h)),
                pl.BlockSpec((t, LANES), lambda s, h: (blk0 + s, nqk + h)),
                pl.BlockSpec((t, LANES), lambda s, h: (blk0 + s, 2 * nqk + h)),
                pl.BlockSpec((t, LANES), lambda s, h: (blk0 + s, h)),
                pl.BlockSpec((t, LANES), lambda s, h: (blk0 + s, 0)),
                pl.BlockSpec((4 * GDN_H, 1, n_pad, c), lambda s, h: (0, s, 0, 0)),
                pl.BlockSpec((CONV_W, LANES), lambda s, h: (0, h)),
                pl.BlockSpec((CONV_W, LANES), lambda s, h: (0, nqk + h)),
                pl.BlockSpec((CONV_W, LANES), lambda s, h: (0, 2 * nqk + h)),
                pl.BlockSpec((1, LANES), lambda s, h: (0, h)),
                pl.BlockSpec((1, LANES), lambda s, h: (0, nqk + h)),
                pl.BlockSpec((1, LANES), lambda s, h: (0, 2 * nqk + h)),
                pl.BlockSpec((1, LANES), lambda s, h: (0, 0)),
                pl.BlockSpec((1, LANES), lambda s, h: (0, 0)),
                pl.BlockSpec((4 * GDN_H, 1), lambda s, h: (0, 0)),
                pl.BlockSpec((4 * GDN_H, 1), lambda s, h: (0, 0)),
                pl.BlockSpec((1, LANES), lambda s, h: (0, 0))]
    args = [qkv, qkv, qkv, z, ab, abt, p["conv_w"], p["conv_w"], p["conv_w"], p["conv_b"], p["conv_b"], p["conv_b"],
            p["alog_row"], p["dtb_row"], p["alog_col"], p["dtb_col"], p["norm_g"]]
    scratch = [pltpu.VMEM((t + 2 * SUBLANES, LANES), F32)] + [pltpu.VMEM((t, LANES), F32)] * 4 + [
        pltpu.VMEM((2, t, LANES), F32), pltpu.VMEM((2, t, LANES), F32), pltpu.VMEM((2, n_pad, c), F32),
        pltpu.VMEM((2, t, GDN_DV), F32), pltpu.VMEM((2, 2 * t, GDN_DK), BF16),
        pltpu.VMEM((2, t // GDN_CHUNK, GDN_CHUNK + GDN_DK, GDN_CHUNK), BF16)]
    mix_shape = jax.ShapeDtypeStruct((N_TOK, 2 * D), BF16)
    ncol0 = D // LANES
    out_mix_spec = pl.BlockSpec((t, LANES), lambda s, h: (blk0 + s, ncol0 + h))
    if s0 is not None:
        in_specs.append(pl.BlockSpec((1, 2, 1, GDN_DK, GDN_DV), lambda s, h: (s, 0, h, 0, 0)))
        args.append(s0)
        in_specs.append(pl.BlockSpec(memory_space=pl.ANY))
        args.append(mix_prev)
        kern = lambda *r: _gdn_kernel(*r[:18], *r[19:], t=t, with_s0=True)
        return pl.pallas_call(
            kern, grid=(nseq, GDN_H), in_specs=in_specs, out_specs=out_mix_spec, out_shape=mix_shape,
            scratch_shapes=scratch, input_output_aliases={len(args) - 1: 0},
            compiler_params=_params("arbitrary", "arbitrary"),
        )(*args)
    in_specs.append(pl.BlockSpec(memory_space=pl.ANY))
    args.append(mix_prev)
    kern = lambda *r: _gdn_kernel(*r[:17], *r[18:], t=t, with_s0=False)
    return pl.pallas_call(
        kern, grid=(nseq, GDN_H), in_specs=in_specs,
        out_specs=[out_mix_spec, pl.BlockSpec((1, 2, 1, GDN_DK, GDN_DV), lambda s, h: (s, 0, h, 0, 0))],
        out_shape=[mix_shape, jax.ShapeDtypeStruct((nseq, 2, GDN_H, GDN_DK, GDN_DV), F32)],
        scratch_shapes=scratch, input_output_aliases={len(args) - 1: 0},
        compiler_params=_params("arbitrary", "arbitrary"),
    )(*args)


def _gla_kernel(*refs, t, with_s0):
    if with_s0:
        (q_ref, k_ref, v_ref, z_ref, lr_ref, w2_ref, b2_ref, ng_ref, s0_ref,
         o_ref, osum, gc_s) = refs
    else:
        (q_ref, k_ref, v_ref, z_ref, lr_ref, w2_ref, b2_ref, ng_ref,
         o_ref, st_ref, osum, gc_s) = refs
    c = GLA_CHUNK
    n_chunks = t // c
    slab = 256
    bi = lax.broadcasted_iota(jnp.int32, (slab, slab), 0)
    bj = lax.broadcasted_iota(jnp.int32, (slab, slab), 1)
    same_chunk = (bi // c) == (bj // c)
    for d in range(2):
        rev = d == 1
        tri = jnp.where(same_chunk & ((bi <= bj) if rev else (bi >= bj)), 1.0, 0.0).astype(BF16)
        for r in range(t // slab):
            lr = lr_ref[pl.ds(r * slab, slab), :]
            pre = _dot_hp(lr, w2_ref[d, 0]) + b2_ref[d]
            glog = (jnp.minimum(pre, 0.0) - jnp.log1p(jnp.exp(-jnp.abs(pre)))) * (1.0 / GLA_TAU)
            gc_s[d, pl.ds(r * slab, slab), :] = _dot_sel(tri, glog)

    osum[...] = jnp.zeros_like(osum)
    ii = lax.broadcasted_iota(jnp.int32, (c, c), 0)
    jj = lax.broadcasted_iota(jnp.int32, (c, c), 1)
    if with_s0:
        s_init = (s0_ref[0, 0, 0].T, s0_ref[0, 1, 0].T)
    else:
        s_init = (jnp.zeros((GLA_DV, GLA_DK), F32), jnp.zeros((GLA_DV, GLA_DK), F32))

    def body(i, carry):
        out = []
        for d in range(2):
            rev = d == 1
            n = (n_chunks - 1 - i) if rev else i
            base = pl.multiple_of(n * c, c)
            rows = pl.ds(base, c)
            q = q_ref[rows, :].astype(F32) * (GLA_DK ** -0.5)
            k = k_ref[rows, :].astype(F32)
            v = v_ref[rows, :]
            gc = gc_s[d, rows, :]
            gl = gc_s[d, pl.ds(base if rev else base + c - 1, 1), :]
            gm = gc_s[d, pl.ds(base + c // 2, 1), :]
            incl = (ii <= jj) if rev else (ii >= jj)
            a_mat = jnp.where(incl, _dot_nt((q * jnp.exp(gc - gm)).astype(BF16),
                                            (k * jnp.exp(gm - gc)).astype(BF16)), 0.0)
            st = carry[d]
            o = _dot_nt((q * jnp.exp(gc)).astype(BF16), st.astype(BF16)) + _dot(a_mat.astype(BF16), v)
            k_dec = (k * jnp.exp(gl - gc)).astype(BF16)
            out.append(st * jnp.exp(gl) + _dot_tn(v, k_dec))
            osum[rows, :] = osum[rows, :] + o
        return tuple(out)

    s_f, s_b = lax.fori_loop(0, n_chunks, body, s_init)
    o = osum[...]
    y = o * lax.rsqrt(jnp.mean(o * o, axis=-1, keepdims=True) + EPS) * ng_ref[...]
    o_ref[...] = (y * _silu(z_ref[...].astype(F32))).astype(o_ref.dtype)
    if not with_s0:
        st_ref[0, 0, 0] = s_f.T
        st_ref[0, 1, 0] = s_b.T


def _gla_call(q, k, v, z, lr, mix_prev, p, t, nseq, row0, s0=None):
    blk0 = row0 // t
    vb = GLA_DV // LANES
    in_specs = [pl.BlockSpec((t, GLA_DK), lambda s, h: (blk0 + s, h)),
                pl.BlockSpec((t, GLA_DK), lambda s, h: (blk0 + s, h)),
                pl.BlockSpec((t, GLA_DV), lambda s, h: (blk0 + s, h)),
                pl.BlockSpec((t, GLA_DV), lambda s, h: (blk0 + s, h)),
                pl.BlockSpec((t, LANES), lambda s, h: (blk0 + s, 0)),
                pl.BlockSpec((2, 1, LANES, GLA_DK), lambda s, h: (0, h, 0, 0)),
                pl.BlockSpec((2, 1, GLA_DK), lambda s, h: (0, 0, h)),
                pl.BlockSpec((1, GLA_DV), lambda s, h: (0, 0))]
    args = [q, k, v, z, lr, p["w2"], p["b2"], p["norm_g"]]
    scratch = [pltpu.VMEM((t, GLA_DV), F32), pltpu.VMEM((2, t, GLA_DK), F32)]
    mix_shape = jax.ShapeDtypeStruct((N_TOK, D), BF16)
    out_mix_spec = pl.BlockSpec((t, GLA_DV), lambda s, h: (blk0 + s, h))
    st_spec = pl.BlockSpec((1, 2, 1, GLA_DK, GLA_DV), lambda s, h: (s, 0, h, 0, 0))
    if s0 is not None:
        in_specs += [st_spec, pl.BlockSpec(memory_space=pl.ANY)]
        args += [s0, mix_prev]
        kern = lambda *r: _gla_kernel(*r[:9], *r[10:], t=t, with_s0=True)
        return pl.pallas_call(
            kern, grid=(nseq, GLA_H), in_specs=in_specs, out_specs=out_mix_spec, out_shape=mix_shape,
            scratch_shapes=scratch, input_output_aliases={len(args) - 1: 0},
            compiler_params=_params("arbitrary", "arbitrary"),
        )(*args)
    return pl.pallas_call(
        functools.partial(_gla_kernel, t=t, with_s0=False),
        grid=(nseq, GLA_H), in_specs=in_specs,
        out_specs=[out_mix_spec, st_spec],
        out_shape=[mix_shape, jax.ShapeDtypeStruct((nseq, 2, GLA_H, GLA_DK, GLA_DV), F32)],
        scratch_shapes=scratch,
        compiler_params=_params("arbitrary", "arbitrary"),
    )(*args)


def _route(sel, s):
    scores = []
    for g in range(N_GROUPS):
        m = sel[g * GROUP_SZ:(g + 1) * GROUP_SZ]
        best = None
        for a in range(GROUP_SZ):
            for b in range(a + 1, GROUP_SZ):
                pair = m[a] + m[b]
                best = pair if best is None else jnp.maximum(best, pair)
        scores.append(best)
    gbest = jnp.zeros_like(scores[0], dtype=jnp.int32)
    top = scores[0]
    for g in range(1, N_GROUPS):
        better = scores[g] > top
        gbest = jnp.where(better, g, gbest)
        top = jnp.where(better, scores[g], top)
    picked = []
    for e in range(N_EXPERTS):
        g = e // GROUP_SZ
        rank = jnp.zeros_like(gbest)
        for m in range(g * GROUP_SZ, (g + 1) * GROUP_SZ):
            if m == e:
                continue
            ahead = (sel[m] > sel[e]) | (sel[m] == sel[e]) if m < e else (sel[m] > sel[e])
            rank = rank + ahead.astype(jnp.int32)
        picked.append(jnp.where((gbest == g) & (rank < 2), s[e], 0.0))
    total = picked[0]
    for e in range(1, N_EXPERTS):
        total = total + picked[e]
    return [p / total for p in picked]


def _outproj_kernel(mix_ref, x_ref, mod_ref, w_ref, g_ref, rwt_ref, rb_ref, x1_ref, h2_ref, gates_ref):
    gate1 = mod_ref[0, :, 2 * D:3 * D]
    x1 = x_ref[...] + gate1 * _dot(mix_ref[...], w_ref[...])
    x1_ref[...] = x1
    y = x1 * lax.rsqrt(jnp.mean(x1 * x1, axis=-1, keepdims=True) + EPS) * g_ref[...]
    h2 = y * (1.0 + mod_ref[0, :, 4 * D:5 * D]) + mod_ref[0, :, 3 * D:4 * D]
    h2_ref[...] = h2.astype(BF16)
    logits = _dot_hp(rwt_ref[...], h2, dot=_dot_nt)
    s_all = _sigmoid(logits)
    sel_all = s_all + rb_ref[...]
    s = [s_all[e:e + 1, :] for e in range(N_EXPERTS)]
    sel = [sel_all[e:e + 1, :] for e in range(N_EXPERTS)]
    gates_t = jnp.concatenate(_route(sel, s), axis=0)
    ri = lax.broadcasted_iota(jnp.int32, (N_EXPERTS, LANES), 0)
    ci = lax.broadcasted_iota(jnp.int32, (N_EXPERTS, LANES), 1)
    eye = jnp.where(ri == ci, 1.0, 0.0).astype(BF16)
    hi, mid, lo = _split3(gates_t)
    gates_ref[...] = _dot_tn(hi, eye) + _dot_tn(mid, eye) + _dot_tn(lo, eye)


def _outproj(mix, x, mod_l, w_out, gain2, router_wt, router_b):
    kdim = mix.shape[1]
    return pl.pallas_call(
        _outproj_kernel,
        grid=(N_TOK // TM,),
        in_specs=[pl.BlockSpec((TM, kdim), lambda i: (i, 0)),
                  pl.BlockSpec((TM, D), lambda i: (i, 0)),
                  pl.BlockSpec((1, 1, 6 * D), lambda i: (_mod_row(i, TM), 0, 0)),
                  pl.BlockSpec((kdim, D), lambda i: (0, 0)),
                  pl.BlockSpec((1, D), lambda i: (0, 0)),
                  pl.BlockSpec((N_EXPERTS, D), lambda i: (0, 0)),
                  pl.BlockSpec((N_EXPERTS, 1), lambda i: (0, 0))],
        out_specs=[pl.BlockSpec((TM, D), lambda i: (i, 0)),
                   pl.BlockSpec((TM, D), lambda i: (i, 0)),
                   pl.BlockSpec((TM, LANES), lambda i: (i, 0))],
        out_shape=[jax.ShapeDtypeStruct((N_TOK, D), F32),
                   jax.ShapeDtypeStruct((N_TOK, D), BF16),
                   jax.ShapeDtypeStruct((N_TOK, LANES), F32)],
        compiler_params=_params("arbitrary"),
    )(mix, x, mod_l.reshape(SUBLANES, 1, 6 * D), w_out, gain2.reshape(1, D), router_wt,
      router_b.reshape(N_EXPERTS, 1))


def _moe_kernel(h_ref, gates_ref, x1_ref, mod_ref, wg_ref, wu_ref, wd_ref, fg_ref, o_ref, acc, *, final_norm):
    e = pl.program_id(1)

    @pl.when(e == 0)
    def _():
        acc[...] = jnp.zeros_like(acc)

    h = h_ref[...]
    gates = gates_ref[...]
    lane = lax.broadcasted_iota(jnp.int32, gates.shape, 1)
    gate = jnp.sum(jnp.where(lane == e, gates, 0.0), axis=-1, keepdims=True)
    hid = _silu(_dot(h, wg_ref[0, 0].astype(BF16))) * _dot(h, wu_ref[0, 0].astype(BF16)) * gate
    acc[...] += _dot(hid.astype(BF16), wd_ref[0, 0].astype(BF16))

    @pl.when(e == N_EXPERTS - 1)
    def _():
        out = x1_ref[...] + mod_ref[0, :, 5 * D:6 * D] * acc[...]
        if final_norm:
            out = out * lax.rsqrt(jnp.mean(out * out, axis=-1, keepdims=True) + EPS) * fg_ref[...]
        o_ref[...] = out


def _moe(h2, gates, x1, mod_l, w_gate, w_up, w_down, layer, final_gain, final_norm):
    tm = TM_MOE
    return pl.pallas_call(
        functools.partial(_moe_kernel, final_norm=final_norm),
        grid=(N_TOK // tm, N_EXPERTS),
        in_specs=[pl.BlockSpec((tm, D), lambda i, e: (i, 0)),
                  pl.BlockSpec((tm, LANES), lambda i, e: (i, 0)),
                  pl.BlockSpec((tm, D), lambda i, e: (i, 0)),
                  pl.BlockSpec((1, 1, 6 * D), lambda i, e: (_mod_row(i, tm), 0, 0)),
                  pl.BlockSpec((1, 1, D, D_EXPERT), lambda i, e: (layer, e, 0, 0)),
                  pl.BlockSpec((1, 1, D, D_EXPERT), lambda i, e: (layer, e, 0, 0)),
                  pl.BlockSpec((1, 1, D_EXPERT, D), lambda i, e: (layer, e, 0, 0)),
                  pl.BlockSpec((1, D), lambda i, e: (0, 0))],
        out_specs=pl.BlockSpec((tm, D), lambda i, e: (i, 0)),
        out_shape=jax.ShapeDtypeStruct((N_TOK, D), F32),
        scratch_shapes=[pltpu.VMEM((tm, D), F32)],
        compiler_params=_params("arbitrary", "arbitrary"),
    )(h2, gates, x1, mod_l.reshape(SUBLANES, 1, 6 * D), w_gate, w_up, w_down, final_gain.reshape(1, D))


def _to_col_major(x):
    b, t, d = x.shape
    return x.reshape(b, t // GRID_W, GRID_W, d).transpose(0, 2, 1, 3).reshape(b, t, d)


def _to_row_major(x):
    b, t, d = x.shape
    return x.reshape(b, GRID_W, t // GRID_W, d).transpose(0, 2, 1, 3).reshape(b, t, d)


def _layer_ab(x, mod_l, state_lru, state_gdn, norm1_g, w_in, lru_conv_w, lru_conv_b, lru_wa, lru_ba, lru_wx,
              lru_bx, lru_lam, gdn_conv_w, gdn_conv_b, gdn_a_log, gdn_dt_bias, gdn_norm_g):
    w = w_in.astype(BF16)
    n_qkv = 2 * GDN_H * GDN_DK + GDN_H * GDN_DV
    o = 0
    w_y, o = w[:, o:o + D], o + D
    w_x, o = w[:, o:o + D], o + D
    w_qkv, o = w[:, o:o + n_qkv], o + n_qkv
    w_z, o = w[:, o:o + GDN_H * GDN_DV], o + GDN_H * GDN_DV
    w_ab = w[:, o:]
    w_ab_pad = jnp.pad(w_ab, ((0, 0), (0, LANES - w_ab.shape[1])))
    ybr, xbr, qkv, z, ab, abt = _inproj(x, mod_l, norm1_g, [w_y, w_x, w_qkv, w_z, w_ab_pad],
                                        [BF16, BF16, BF16, BF16, F32], w_t=w_ab.T)

    nb = D // LRU_BLOCK
    wg = jnp.stack([lru_wa[0], lru_wx[0], lru_wa[1], lru_wx[1]], axis=1)
    wg = wg.transpose(0, 2, 1, 3).reshape(nb, LRU_BLOCK, 4 * LRU_BLOCK).astype(BF16)
    bg = jnp.stack([lru_ba[0], lru_bx[0], lru_ba[1], lru_bx[1]], axis=0)
    bg = bg.reshape(4, nb, LRU_BLOCK).transpose(1, 0, 2).reshape(nb, 1, 4 * LRU_BLOCK)
    lam = lru_lam.reshape(2, nb, LRU_BLOCK).transpose(1, 0, 2)
    lru_p = dict(conv_w=lru_conv_w, conv_b=lru_conv_b.reshape(1, D), wg=wg, bg=bg, lam=lam)
    mix, lru_state = _lru_call(xbr, ybr, None, lru_p, T_PROMPT, N_PROMPT_SEQ, 0)
    mix = _lru_call(xbr, ybr, mix, lru_p, T_SAMPLE, N_SAMPLE_SEQ, N_PROMPT, h0=state_lru)

    pad16 = lambda v: jnp.pad(v.reshape(1, 2 * GDN_H), ((0, 0), (0, LANES - 2 * GDN_H)))
    col32 = lambda v: jnp.pad(v.reshape(2 * GDN_H, 1), ((0, 2 * GDN_H), (0, 0)))
    gdn_p = dict(conv_w=gdn_conv_w, conv_b=gdn_conv_b.reshape(1, n_qkv), alog_row=pad16(gdn_a_log),
                 dtb_row=pad16(gdn_dt_bias), alog_col=col32(gdn_a_log), dtb_col=col32(gdn_dt_bias),
                 norm_g=gdn_norm_g.reshape(1, GDN_DV))
    abt_p = abt[:, :N_PROMPT].reshape(4 * GDN_H, N_PROMPT_SEQ, T_PROMPT // GDN_SLAB, GDN_SLAB)
    abt_s = abt[:, N_PROMPT:].reshape(4 * GDN_H, N_SAMPLE_SEQ, T_SAMPLE // GDN_SLAB, GDN_SLAB)
    mix, gdn_state = _gdn_call(qkv, z, ab, abt_p, mix, gdn_p, T_PROMPT, N_PROMPT_SEQ, 0)
    mix = _gdn_call(qkv, z, ab, abt_s, mix, gdn_p, T_SAMPLE, N_SAMPLE_SEQ, N_PROMPT, s0=state_gdn)
    return mix, lru_state, gdn_state


def _layer_c(x, mod_l, state_gla, norm1_g, w_in, w2, b2, norm_g):
    w = w_in.astype(BF16)
    nk = GLA_H * GLA_DK
    nv = GLA_H * GLA_DV
    w_q, w_k, w_v, w_z = w[:, 0:nk], w[:, nk:2 * nk], w[:, 2 * nk:2 * nk + nv], w[:, 2 * nk + nv:2 * nk + 2 * nv]
    w_lr = jnp.pad(w[:, 2 * nk + 2 * nv:], ((0, 0), (0, LANES - 2 * GLA_RANK)))
    xs_cm = _to_col_major(x[N_PROMPT:].reshape(N_SAMPLE_SEQ, T_SAMPLE, D)).reshape(N_SAMPLE, D)
    x_cm = jnp.concatenate([x[:N_PROMPT], xs_cm], axis=0)
    q, k, v, z, lr = _inproj(x_cm, mod_l, norm1_g, [w_q, w_k, w_v, w_z, w_lr], [BF16, BF16, BF16, BF16, F32])
    w2h = w2.reshape(2, GLA_RANK, GLA_H, GLA_DK).transpose(0, 2, 1, 3)
    w2big = jnp.zeros((2, GLA_H, LANES, GLA_DK), F32)
    w2big = w2big.at[0, :, 0:GLA_RANK].set(w2h[0]).at[1, :, GLA_RANK:2 * GLA_RANK].set(w2h[1])
    gla_p = dict(w2=w2big, b2=b2.reshape(2, 1, nk), norm_g=norm_g.reshape(1, GLA_DV))
    mix, gla_state = _gla_call(q, k, v, z, lr, None, gla_p, T_PROMPT, N_PROMPT_SEQ, 0)
    mix = _gla_call(q, k, v, z, lr, mix, gla_p, T_SAMPLE, N_SAMPLE_SEQ, N_PROMPT, s0=state_gla)
    ms_rm = _to_row_major(mix[N_PROMPT:].reshape(N_SAMPLE_SEQ, T_SAMPLE, D)).reshape(N_SAMPLE, D)
    return jnp.concatenate([mix[:N_PROMPT], ms_rm], axis=0), gla_state


def kernel(x_prompt, x_sample, state_lru, state_gdn, state_gla, c, c_ctx, ada_w, ada_b, norm1_g, norm2_g,
           final_norm_g, ab_w_in, lru_conv_w, lru_conv_b, lru_wa, lru_ba, lru_wx, lru_bx, lru_lam, gdn_conv_w,
           gdn_conv_b, gdn_a_log, gdn_dt_bias, gdn_norm_g, ab_w_out, gla_w_in, gla_w2, gla_b2, gla_norm_g,
           gla_w_out, router_w, router_b, moe_w_gate, moe_w_up, moe_w_down):
    x = jnp.concatenate([x_prompt.reshape(N_PROMPT, D), x_sample.reshape(N_SAMPLE, D)], axis=0)
    cvec = jnp.concatenate([c_ctx.reshape(1, D), c, jnp.zeros((SUBLANES - 1 - N_SAMPLE_SEQ, D), F32)], axis=0)
    mod = _modulation(cvec, ada_w, ada_b)
    router_wt = router_w.T

    mix, lru_state, gdn_state = _layer_ab(
        x, mod[0], state_lru[:, 0], state_gdn[:, 0], norm1_g[0], ab_w_in[0], lru_conv_w[0], lru_conv_b[0],
        lru_wa[0], lru_ba[0], lru_wx[0], lru_bx[0], lru_lam[0], gdn_conv_w[0], gdn_conv_b[0], gdn_a_log[0],
        gdn_dt_bias[0], gdn_norm_g[0])
    x1, h2, gates = _outproj(mix, x, mod[0], ab_w_out[0].astype(BF16), norm2_g[0], router_wt, router_b)
    x = _moe(h2, gates, x1, mod[0], moe_w_gate, moe_w_up, moe_w_down, 0, final_norm_g, False)

    mix, gla_state = _layer_c(x, mod[1], state_gla[:, 0], norm1_g[1], gla_w_in[0], gla_w2[0], gla_b2[0],
                              gla_norm_g[0])
    x1, h2, gates = _outproj(mix, x, mod[1], gla_w_out[0].astype(BF16), norm2_g[1], router_wt, router_b)
    y = _moe(h2, gates, x1, mod[1], moe_w_gate, moe_w_up, moe_w_down, 1, final_norm_g, True)

    y_prompt = y[:N_PROMPT].reshape(N_PROMPT_SEQ, T_PROMPT, D)
    y_sample = y[N_PROMPT:].reshape(N_SAMPLE_SEQ, T_SAMPLE, D)
    return (y_prompt, y_sample, lru_state[:, None], gdn_state[:, None], gla_state[:, None])
```

```python
import functools
import math
from typing import NamedTuple

import jax
import jax.numpy as jnp
from jax import lax
from jax.experimental import pallas as pl
from jax.experimental.pallas import tpu as pltpu

F32 = jnp.float32
BF16 = jnp.bfloat16

D = 1024
N_PROMPT_SEQ, T_PROMPT = 16, 256
N_SAMPLE_SEQ, T_SAMPLE = 4, 2048
N_PROMPT = N_PROMPT_SEQ * T_PROMPT
N_SAMPLE = N_SAMPLE_SEQ * T_SAMPLE
GRID_W = 64
GRID_H = T_SAMPLE // GRID_W
EPS = 1e-6
LANES = 128
SUBLANES = 8

LRU_C = 8.0
LRU_BLOCK = 128
CONV_LEFT = 2
CONV_W = 4
GDN_H, GDN_DK, GDN_DV, GDN_CHUNK = 8, 128, 128, 64
GLA_H, GLA_DK, GLA_DV, GLA_CHUNK, GLA_RANK, GLA_TAU = 4, 128, 256, 32, 16, 16.0
N_EXPERTS, N_GROUPS, D_EXPERT = 16, 4, 512
GROUP_SZ = N_EXPERTS // N_GROUPS

TM = 256
TM_MOE = 1024
VMEM_LIMIT = 56 * 1024 * 1024

_NT = (((1,), (1,)), ((), ()))
_TN = (((0,), (0,)), ((), ()))


def _dot(a, b):
    return jnp.dot(a, b, preferred_element_type=F32)


def _dot_nt(a, b):
    return lax.dot_general(a, b, _NT, preferred_element_type=F32)


def _dot_tn(a, b):
    return lax.dot_general(a, b, _TN, preferred_element_type=F32)


def _split2(x):
    hi = x.astype(BF16)
    lo = (x - hi.astype(F32)).astype(BF16)
    return hi, lo


def _split3(x):
    hi = x.astype(BF16)
    r = x - hi.astype(F32)
    mid = r.astype(BF16)
    lo = (r - mid.astype(F32)).astype(BF16)
    return hi, mid, lo


def _dot_sel(sel_bf16, x):
    hi, mid, lo = _split3(x)
    return _dot(sel_bf16, hi) + _dot(sel_bf16, mid) + _dot(sel_bf16, lo)


def _dot_hp(a, b, dot=_dot):
    ah, al = _split2(a)
    bh, bl = _split2(b)
    return dot(ah, bh) + dot(ah, bl) + dot(al, bh)


def _sigmoid(x):
    return 1.0 / (1.0 + jnp.exp(-x))


def _silu(x):
    return x * _sigmoid(x)


def _softplus(x):
    return jnp.maximum(x, 0.0) + jnp.log1p(jnp.exp(-jnp.abs(x)))


def _gelu_tanh(x):
    return 0.5 * x * (1.0 + jnp.tanh(math.sqrt(2.0 / math.pi) * (x + 0.044715 * (x * x * x))))


def _params(*sem):
    return pltpu.CompilerParams(dimension_semantics=sem, vmem_limit_bytes=VMEM_LIMIT)


class _Stream(NamedTuple):
    n_tok: int
    t: int
    nseq: int
    per_request_mod: bool

    def mod_row(self, i, tm):
        return 1 + i // (self.t // tm) if self.per_request_mod else 0


PROMPT = _Stream(N_PROMPT, T_PROMPT, N_PROMPT_SEQ, False)
SAMPLE = _Stream(N_SAMPLE, T_SAMPLE, N_SAMPLE_SEQ, True)
COLS_PER_TILE = TM // GRID_H


def _grid_perm(to_col_major):
    i = lax.broadcasted_iota(jnp.int32, (TM, TM), 0)
    j = lax.broadcasted_iota(jnp.int32, (TM, TM), 1)
    if to_col_major:
        src = (i % GRID_H) * COLS_PER_TILE + i // GRID_H
    else:
        src = (i % COLS_PER_TILE) * GRID_H + i // COLS_PER_TILE
    return jnp.where(j == src, 1.0, 0.0).astype(BF16)


def _mod_kernel(c_ref, w_ref, b_ref, o_ref):
    c = c_ref[...]
    o_ref[0] = _dot_hp(_silu(c), w_ref[0]) + b_ref[0]


def _modulation(cvec, ada_w, ada_b):
    depth = ada_w.shape[0]
    n6 = ada_w.shape[2]
    return pl.pallas_call(
        _mod_kernel,
        grid=(depth, n6 // D),
        in_specs=[pl.BlockSpec((SUBLANES, D), lambda l, j: (0, 0)),
                  pl.BlockSpec((1, D, D), lambda l, j: (l, 0, j)),
                  pl.BlockSpec((1, 1, D), lambda l, j: (l, 0, j))],
        out_specs=pl.BlockSpec((1, SUBLANES, D), lambda l, j: (l, 0, j)),
        out_shape=jax.ShapeDtypeStruct((depth, SUBLANES, n6), F32),
        compiler_params=_params("arbitrary", "arbitrary"),
    )(cvec, ada_w, ada_b.reshape(depth, 1, n6))


def _grid_tile_spec():
    tiles_per_seq = GRID_W // COLS_PER_TILE
    return pl.BlockSpec((1, GRID_H, COLS_PER_TILE, D), lambda i: (i // tiles_per_seq, 0, i % tiles_per_seq, 0))


def _inproj_kernel(x_ref, mod_ref, g_ref, *refs, n_w, has_t, to_col_major):
    w_refs = refs[:n_w]
    o_refs = refs[n_w + has_t:2 * n_w + has_t]
    x = x_ref[0].reshape(TM, D) if to_col_major else x_ref[...]
    y = x * lax.rsqrt(jnp.mean(x * x, axis=-1, keepdims=True) + EPS) * g_ref[...]
    shift = mod_ref[0, :, 0:D]
    scale = mod_ref[0, :, D:2 * D]
    h = (y * (1.0 + scale) + shift).astype(BF16)
    if to_col_major:
        h = _dot(_grid_perm(True), h).astype(BF16)
    for w_ref, o_ref in zip(w_refs, o_refs):
        o_ref[...] = _dot(h, w_ref[...]).astype(o_ref.dtype)
    if has_t:
        wt_ref = refs[n_w]
        ot_ref = refs[2 * n_w + 1]
        ot_ref[...] = _dot_nt(wt_ref[...], h)


def _inproj(st, x, mod_l, gain, weights, out_dtypes, w_t=None, to_col_major=False):
    n_w = len(weights)
    in_specs = [_grid_tile_spec() if to_col_major else pl.BlockSpec((TM, D), lambda i: (i, 0)),
                pl.BlockSpec((1, 1, 6 * D), lambda i: (st.mod_row(i, TM), 0, 0)),
                pl.BlockSpec((1, D), lambda i: (0, 0))]
    in_specs += [pl.BlockSpec(w.shape, lambda i: (0, 0)) for w in weights]
    out_specs = [pl.BlockSpec((TM, w.shape[1]), lambda i: (i, 0)) for w in weights]
    out_shape = [jax.ShapeDtypeStruct((st.n_tok, w.shape[1]), dt) for w, dt in zip(weights, out_dtypes)]
    args = [x, mod_l.reshape(SUBLANES, 1, 6 * D), gain.reshape(1, D)] + list(weights)
    if w_t is not None:
        in_specs.append(pl.BlockSpec(w_t.shape, lambda i: (0, 0)))
        out_specs.append(pl.BlockSpec((w_t.shape[0], TM), lambda i: (0, i)))
        out_shape.append(jax.ShapeDtypeStruct((w_t.shape[0], st.n_tok), F32))
        args.append(w_t)
    return pl.pallas_call(
        functools.partial(_inproj_kernel, n_w=n_w, has_t=int(w_t is not None), to_col_major=to_col_major),
        grid=(st.n_tok // TM,),
        in_specs=in_specs, out_specs=out_specs, out_shape=out_shape,
        compiler_params=_params("arbitrary"),
    )(*args)


def _conv_from_padded(xp_ref, t, cw, cb):
    acc = cb
    for j in range(CONV_W):
        acc = acc + cw[j:j + 1, :] * xp_ref[pl.ds(SUBLANES - CONV_LEFT + j, t), :]
    return acc


def _fill_padded(xp_ref, x, t):
    zeros = jnp.zeros((SUBLANES, xp_ref.shape[1]), F32)
    xp_ref[pl.ds(0, SUBLANES), :] = zeros
    xp_ref[pl.ds(SUBLANES + t, SUBLANES), :] = zeros
    xp_ref[pl.ds(SUBLANES, t), :] = x


def _chunk_cumsum(x, c, rev):
    t = x.shape[0]
    row = lax.broadcasted_iota(jnp.int32, x.shape, 0) % c
    s = 1
    while s < c:
        if rev:
            x = x + jnp.where(row < c - s, pltpu.roll(x, t - s, 0), 0.0)
        else:
            x = x + jnp.where(row >= s, pltpu.roll(x, s, 0), 0.0)
        s *= 2
    return x


_SCAN_ROWS = 32


def _scan_tile(a, b, carry, rev):
    row = lax.broadcasted_iota(jnp.int32, a.shape, 0)
    for s in (1, 2, 4):
        if rev:
            a_s = pltpu.roll(a, SUBLANES - s, 0)
            b_s = pltpu.roll(b, SUBLANES - s, 0)
            m = row < SUBLANES - s
        else:
            a_s = pltpu.roll(a, s, 0)
            b_s = pltpu.roll(b, s, 0)
            m = row >= s
        b = jnp.where(m, a * b_s + b, b)
        a = jnp.where(m, a * a_s, a)
    h = a * carry + b
    return h, (h[0:1] if rev else h[SUBLANES - 1:SUBLANES])


def _lru_kernel(*refs, t, with_h0):
    if with_h0:
        (x_ref, y_ref, cw_ref, cb_ref, wg_ref, bg_ref, lam_ref, h0_ref,
         o_ref, xp, a_f, b_f, a_b, b_b) = refs
    else:
        (x_ref, y_ref, cw_ref, cb_ref, wg_ref, bg_ref, lam_ref,
         o_ref, st_ref, xp, a_f, b_f, a_b, b_b) = refs
    _fill_padded(xp, x_ref[...].astype(F32), t)
    xc = _conv_from_padded(xp, t, cw_ref[...], cb_ref[...])
    gates = _sigmoid(_dot(xc.astype(BF16), wg_ref[0]) + bg_ref[0])
    sp = _softplus(-lam_ref[0])
    for d, (a_ref, b_ref) in enumerate(((a_f, b_f), (a_b, b_b))):
        r = gates[:, (2 * d) * LRU_BLOCK:(2 * d + 1) * LRU_BLOCK]
        ig = gates[:, (2 * d + 1) * LRU_BLOCK:(2 * d + 2) * LRU_BLOCK]
        log_a = -LRU_C * r * sp[d:d + 1, :]
        a = jnp.exp(log_a)
        a_ref[...] = a
        b_ref[...] = jnp.sqrt(1.0 - a * a) * ig * xc

    n_it = t // _SCAN_ROWS
    if with_h0:
        c0 = (h0_ref[0, 0:1, :], h0_ref[0, 1:2, :])
    else:
        c0 = (jnp.zeros((1, LRU_BLOCK), F32), jnp.zeros((1, LRU_BLOCK), F32))

    def body(i, carry):
        cf, cb_ = carry
        base_f = pl.multiple_of(i * _SCAN_ROWS, _SCAN_ROWS)
        base_b = pl.multiple_of((n_it - 1 - i) * _SCAN_ROWS, _SCAN_ROWS)
        for k in range(_SCAN_ROWS // SUBLANES):
            rf = pl.ds(base_f + k * SUBLANES, SUBLANES)
            h, cf = _scan_tile(a_f[rf, :], b_f[rf, :], cf, False)
            b_f[rf, :] = h
            rb = pl.ds(base_b + _SCAN_ROWS - (k + 1) * SUBLANES, SUBLANES)
            h, cb_ = _scan_tile(a_b[rb, :], b_b[rb, :], cb_, True)
            b_b[rb, :] = h
        return cf, cb_

    cf, cb_ = lax.fori_loop(0, n_it, body, c0)
    o_ref[...] = (_gelu_tanh(y_ref[...].astype(F32)) * (b_f[...] + b_b[...])).astype(o_ref.dtype)
    if not with_h0:
        st_ref[0, 0:1, :] = cf
        st_ref[0, 1:2, :] = cb_


def _lru_call(st, xbr, ybr, p, h0=None):
    nb = D // LRU_BLOCK
    t, nseq = st.t, st.nseq
    tok = lambda s, j: (s, j)
    in_specs = [pl.BlockSpec((t, LRU_BLOCK), tok),
                pl.BlockSpec((t, LRU_BLOCK), tok),
                pl.BlockSpec((CONV_W, LRU_BLOCK), lambda s, j: (0, j)),
                pl.BlockSpec((1, LRU_BLOCK), lambda s, j: (0, j)),
                pl.BlockSpec((1, LRU_BLOCK, 4 * LRU_BLOCK), lambda s, j: (j, 0, 0)),
                pl.BlockSpec((1, 1, 4 * LRU_BLOCK), lambda s, j: (j, 0, 0)),
                pl.BlockSpec((1, 2, LRU_BLOCK), lambda s, j: (j, 0, 0))]
    args = [xbr, ybr, p["conv_w"], p["conv_b"], p["wg"], p["bg"], p["lam"]]
    scratch = [pltpu.VMEM((t + 2 * SUBLANES, LRU_BLOCK), F32)] + [pltpu.VMEM((t, LRU_BLOCK), F32)] * 4
    mix_shape = jax.ShapeDtypeStruct((st.n_tok, D), BF16)
    out_mix_spec = pl.BlockSpec((t, LRU_BLOCK), tok)
    if h0 is not None:
        in_specs.append(pl.BlockSpec((1, 2, LRU_BLOCK), lambda s, j: (s, 0, j)))
        args.append(h0)
        return pl.pallas_call(
            functools.partial(_lru_kernel, t=t, with_h0=True),
            grid=(nseq, nb), in_specs=in_specs, out_specs=out_mix_spec, out_shape=mix_shape,
            scratch_shapes=scratch,
            compiler_params=_params("arbitrary", "arbitrary"),
        )(*args)
    return pl.pallas_call(
        functools.partial(_lru_kernel, t=t, with_h0=False),
        grid=(nseq, nb), in_specs=in_specs,
        out_specs=[out_mix_spec, pl.BlockSpec((1, 2, LRU_BLOCK), lambda s, j: (s, 0, j))],
        out_shape=[mix_shape, jax.ShapeDtypeStruct((nseq, 2, D), F32)],
        scratch_shapes=scratch,
        compiler_params=_params("arbitrary", "arbitrary"),
    )(*args)


GDN_GROUP = 4
GDN_SLAB = GDN_GROUP * GDN_CHUNK


def _compact_select(x, c):
    g = x.shape[0] // c
    lane_blk = lax.broadcasted_iota(jnp.int32, (c, g * c), 1) // c
    out = x[0:c]
    for p in range(1, g):
        out = jnp.where(lane_blk == p, x[p * c:(p + 1) * c], out)
    return out


def _block_diag(xc, same_block):
    g = xc.shape[1] // xc.shape[0]
    return jnp.where(same_block, jnp.concatenate([xc] * g, axis=0), jnp.zeros((), xc.dtype))


def _dot_hp_bd(a, bc, same_block):
    ah, al = _split2(a)
    bh, bl = _split2(bc)
    bdh = _block_diag(bh, same_block)
    return _dot(ah, bdh) + _dot(al, bdh) + _dot(ah, _block_diag(bl, same_block))


def _tri_inverse_compact(a_cs, eye_c, same_block):
    c = eye_c.shape[0]
    p_cs = [-a_c for a_c in a_cs]
    t_cs = [eye_c + p_c for p_c in p_cs]
    p_cs = [_dot_hp_bd(p_c, p_c, same_block) for p_c in p_cs]
    levels = int(math.log2(c)) - 1
    for lvl in range(1, levels):
        xs = [_dot_hp_bd(jnp.concatenate([t_c, p_c], axis=0), p_c, same_block) for t_c, p_c in zip(t_cs, p_cs)]
        t_cs = [t_c + x[0:c] for t_c, x in zip(t_cs, xs)]
        p_cs = [x[c:2 * c] for x in xs]
    return [t_c + _dot_hp_bd(t_c, p_c, same_block) for t_c, p_c in zip(t_cs, p_cs)]


def _gdn_prepare_group(q, k, v, gcb, gcrow, beta, gl, masks):
    c = GDN_CHUNK
    same_block, eye_c, incl_c, strict_c = masks
    kb = [k * beta[d] for d in range(2)]
    prod = _dot_nt(jnp.concatenate([kb[0], kb[1], q], axis=0).astype(BF16), k.astype(BF16))
    n = q.shape[0]
    qk_c = _compact_select(prod[2 * n:3 * n], c)
    low_half = lax.broadcasted_iota(jnp.int32, (c, LANES), 1) < c
    decay, a_cs = [], []
    for d in range(2):
        gi_c = jnp.concatenate(
            [jnp.where(low_half, gcb[d][(2 * h) * c:(2 * h + 1) * c], gcb[d][(2 * h + 1) * c:(2 * h + 2) * c])
             for h in range(GDN_GROUP // 2)], axis=1)
        diff = gi_c - gcrow[d]
        decay.append(jnp.where(incl_c[d], jnp.exp(jnp.where(incl_c[d], diff, 0.0)), 0.0))
        a_cs.append(jnp.where(strict_c[d], _compact_select(prod[d * n:(d + 1) * n], c) * decay[d], 0.0))
    t_cs = _tri_inverse_compact(a_cs, eye_c, same_block)
    out = []
    for d in range(2):
        eg = jnp.exp(gcb[d])
        rhs = jnp.concatenate([v * beta[d], kb[d] * eg], axis=1).astype(BF16)
        uw = _dot(_block_diag(t_cs[d].astype(BF16), same_block), rhs)
        k_dec_t = (k * jnp.exp(gl[d] - gcb[d])).T
        out.append((uw[:, 0:GDN_DV], uw[:, GDN_DV:].astype(BF16), (q * eg).astype(BF16), qk_c * decay[d], k_dec_t))
    return out


def _gdn_kernel(*refs, t, with_s0):
    if with_s0:
        (q_ref, k_ref, v_ref, z_ref, ab_ref, abt_ref, cwq_ref, cwk_ref, cwv_ref, cbq_ref, cbk_ref, cbv_ref,
         alog_ref, dtb_ref, alogc_ref, dtbc_ref, ng_ref, s0_ref,
         o_ref, xp, qs, ks, vs, osum, gcb_s, beta_s, gcrow_s, u_s, wq_s, qkkd_s) = refs
    else:
        (q_ref, k_ref, v_ref, z_ref, ab_ref, abt_ref, cwq_ref, cwk_ref, cwv_ref, cbq_ref, cbk_ref, cbv_ref,
         alog_ref, dtb_ref, alogc_ref, dtbc_ref, ng_ref,
         o_ref, st_ref, xp, qs, ks, vs, osum, gcb_s, beta_s, gcrow_s, u_s, wq_s, qkkd_s) = refs
    c = GDN_CHUNK
    n_chunks = t // c
    head = pl.program_id(1)

    for x_ref, cw_ref, cb_ref, dst, kind in ((q_ref, cwq_ref, cbq_ref, qs, "q"), (k_ref, cwk_ref, cbk_ref, ks, "k"),
                                             (v_ref, cwv_ref, cbv_ref, vs, "v")):
        _fill_padded(xp, x_ref[...].astype(F32), t)
        y = _silu(_conv_from_padded(xp, t, cw_ref[...], cb_ref[...]))
        if kind != "v":
            y = y * lax.rsqrt(jnp.sum(y * y, axis=-1, keepdims=True) + EPS)
        if kind == "q":
            y = y * (GDN_DK ** -0.5)
        dst[...] = y

    ab = ab_ref[...]
    lane = lax.broadcasted_iota(jnp.int32, ab.shape, 1)
    gall = jnp.where(lane < 2 * GDN_H, -jnp.exp(alog_ref[...]) * _softplus(ab + dtb_ref[...]), _sigmoid(ab))
    pick = lambda idx: jnp.sum(jnp.where(lane == idx, gall, 0.0), axis=-1, keepdims=True)
    abt_rows = lambda idx: abt_ref[idx, 0]
    slab = GDN_SLAB
    bi = lax.broadcasted_iota(jnp.int32, (slab, slab), 0)
    bj = lax.broadcasted_iota(jnp.int32, (slab, slab), 1)
    same_chunk = (bi // c) == (bj // c)
    tri = [jnp.where(same_chunk & ((bi <= bj) if d == 1 else (bi >= bj)), 1.0, 0.0).astype(BF16) for d in range(2)]
    for d in range(2):
        g_col = pick(d * GDN_H + head)
        beta_s[d] = jnp.broadcast_to(pick(2 * GDN_H + d * GDN_H + head), (t, LANES))
        gcb_s[d] = _chunk_cumsum(jnp.broadcast_to(g_col, (t, LANES)), c, d == 1)
        a_row = abt_rows(d * GDN_H + head)
        al = alogc_ref[pl.ds(d * GDN_H + head, 1), :]
        db = dtbc_ref[pl.ds(d * GDN_H + head, 1), :]
        g_rows = -jnp.exp(al) * _softplus(a_row + db)
        gcrow_s[d] = _dot_sel_right(g_rows, tri[1 - d])

    osum[...] = jnp.zeros_like(osum)
    ci = lax.broadcasted_iota(jnp.int32, (c, slab), 0)
    cj = lax.broadcasted_iota(jnp.int32, (c, slab), 1) % c
    masks = (same_chunk, jnp.where(ci == cj, 1.0, 0.0), (ci >= cj, ci <= cj), (ci > cj, ci < cj))
    if with_s0:
        s_init = (s0_ref[0, 0, 0], s0_ref[0, 1, 0])
    else:
        s_init = (jnp.zeros((GDN_DK, GDN_DV), F32), jnp.zeros((GDN_DK, GDN_DV), F32))

    def chunk_total(d, base):
        return gcb_s[d, pl.ds(base if d == 1 else base + c - 1, 1), :]

    def prepare(i, _):
        base = pl.multiple_of(i * slab, slab)
        rows = pl.ds(base, slab)
        gl = [jnp.concatenate([jnp.broadcast_to(chunk_total(d, base + p * c), (c, LANES)) for p in range(GDN_GROUP)],
                              axis=0) for d in range(2)]
        res = _gdn_prepare_group(
            qs[rows, :], ks[rows, :], vs[rows, :], [gcb_s[d, rows, :] for d in range(2)],
            [gcrow_s[d, pl.ds(i, 1), :] for d in range(2)], [beta_s[d, rows, :] for d in range(2)], gl, masks)
        for d, (u, w, qg, qk_c, k_dec_t) in enumerate(res):
            u_s[d, rows, :] = u
            for p in range(GDN_GROUP):
                n = i * GDN_GROUP + p
                dst = pl.multiple_of(base * 2 + p * 2 * c, 2 * c)
                wq_s[d, pl.ds(dst, c), :] = w[p * c:(p + 1) * c]
                wq_s[d, pl.ds(dst + c, c), :] = qg[p * c:(p + 1) * c]
                qkkd_s[d, n, pl.ds(0, c), :] = qk_c[:, p * c:(p + 1) * c].astype(BF16)
                qkkd_s[d, n, pl.ds(c, GDN_DK), :] = k_dec_t[:, p * c:(p + 1) * c].astype(BF16)
        return 0

    lax.fori_loop(0, t // slab, prepare, 0)

    def body(i, carry):
        out = []
        for d in range(2):
            n = (n_chunks - 1 - i) if d == 1 else i
            base = pl.multiple_of(n * c, c)
            rows = pl.ds(base, c)
            s = carry[d]
            ws_qs = _dot(wq_s[d, pl.ds(pl.multiple_of(n * 2 * c, 2 * c), 2 * c), :], s.astype(BF16))
            v_new = (u_s[d, rows, :] - ws_qs[0:c]).astype(BF16)
            upd = _dot(qkkd_s[d, n], v_new)
            out.append(s * jnp.exp(chunk_total(d, base)) + upd[c:c + GDN_DK])
            osum[rows, :] = osum[rows, :] + ws_qs[c:2 * c] + upd[0:c]
        return tuple(out)

    s_f, s_b = lax.fori_loop(0, n_chunks, body, s_init)
    o = osum[...]
    y = o * lax.rsqrt(jnp.mean(o * o, axis=-1, keepdims=True) + EPS) * ng_ref[...]
    o_ref[...] = (y * _silu(z_ref[...].astype(F32))).astype(o_ref.dtype)
    if not with_s0:
        st_ref[0, 0, 0] = s_f
        st_ref[0, 1, 0] = s_b


def _dot_sel_right(x, sel_bf16):
    hi, mid, lo = _split3(x)
    return _dot(hi, sel_bf16) + _dot(mid, sel_bf16) + _dot(lo, sel_bf16)


def _gdn_call(st, qkv, z, ab, abt, p, s0=None):
    c = GDN_SLAB
    t, nseq = st.t, st.nseq
    n_chunks = t // c
    n_pad = max(n_chunks, SUBLANES)
    abt = abt.reshape(4 * GDN_H, nseq, n_chunks, c)
    if n_pad != n_chunks:
        abt = jnp.pad(abt, ((0, 0), (0, 0), (0, n_pad - n_chunks), (0, 0)))
    nqk = GDN_H * GDN_DK // LANES
    in_specs = [pl.BlockSpec((t, LANES), lambda s, h: (s, h)),
                pl.BlockSpec((t, LANES), lambda s, h: (s, nqk + h)),
                pl.BlockSpec((t, LANES), lambda s, h: (s, 2 * nqk + h)),
                pl.BlockSpec((t, LANES), lambda s, h: (s, h)),
                pl.BlockSpec((t, LANES), lambda s, h: (s, 0)),
                pl.BlockSpec((4 * GDN_H, 1, n_pad, c), lambda s, h: (0, s, 0, 0)),
                pl.BlockSpec((CONV_W, LANES), lambda s, h: (0, h)),
                pl.BlockSpec((CONV_W, LANES), lambda s, h: (0, nqk + h)),
                pl.BlockSpec((CONV_W, LANES), lambda s, h: (0, 2 * nqk + h)),
                pl.BlockSpec((1, LANES), lambda s, h: (0, h)),
                pl.BlockSpec((1, LANES), lambda s, h: (0, nqk + h)),
                pl.BlockSpec((1, LANES), lambda s, h: (0, 2 * nqk + h)),
                pl.BlockSpec((1, LANES), lambda s, h: (0, 0)),
                pl.BlockSpec((1, LANES), lambda s, h: (0, 0)),
                pl.BlockSpec((4 * GDN_H, 1), lambda s, h: (0, 0)),
                pl.BlockSpec((4 * GDN_H, 1), lambda s, h: (0, 0)),
                pl.BlockSpec((1, LANES), lambda s, h: (0, 0))]
    args = [qkv, qkv, qkv, z, ab, abt, p["conv_w"], p["conv_w"], p["conv_w"], p["conv_b"], p["conv_b"], p["conv_b"],
            p["alog_row"], p["dtb_row"], p["alog_col"], p["dtb_col"], p["norm_g"]]
    scratch = [pltpu.VMEM((t + 2 * SUBLANES, LANES), F32)] + [pltpu.VMEM((t, LANES), F32)] * 4 + [
        pltpu.VMEM((2, t, LANES), F32), pltpu.VMEM((2, t, LANES), F32), pltpu.VMEM((2, n_pad, c), F32),
        pltpu.VMEM((2, t, GDN_DV), F32), pltpu.VMEM((2, 2 * t, GDN_DK), BF16),
        pltpu.VMEM((2, t // GDN_CHUNK, GDN_CHUNK + GDN_DK, GDN_CHUNK), BF16)]
    mix_shape = jax.ShapeDtypeStruct((st.n_tok, GDN_H * GDN_DV), BF16)
    out_mix_spec = pl.BlockSpec((t, LANES), lambda s, h: (s, h))
    st_spec = pl.BlockSpec((1, 2, 1, GDN_DK, GDN_DV), lambda s, h: (s, 0, h, 0, 0))
    if s0 is not None:
        in_specs.append(st_spec)
        args.append(s0)
        return pl.pallas_call(
            functools.partial(_gdn_kernel, t=t, with_s0=True),
            grid=(nseq, GDN_H), in_specs=in_specs, out_specs=out_mix_spec, out_shape=mix_shape,
            scratch_shapes=scratch,
            compiler_params=_params("arbitrary", "arbitrary"),
        )(*args)
    return pl.pallas_call(
        functools.partial(_gdn_kernel, t=t, with_s0=False),
        grid=(nseq, GDN_H), in_specs=in_specs,
        out_specs=[out_mix_spec, st_spec],
        out_shape=[mix_shape, jax.ShapeDtypeStruct((nseq, 2, GDN_H, GDN_DK, GDN_DV), F32)],
        scratch_shapes=scratch,
        compiler_params=_params("arbitrary", "arbitrary"),
    )(*args)


def _gla_kernel(*refs, t, with_s0):
    if with_s0:
        (q_ref, k_ref, v_ref, z_ref, lr_ref, w2_ref, b2_ref, ng_ref, s0_ref,
         o_ref, osum, gc_s) = refs
    else:
        (q_ref, k_ref, v_ref, z_ref, lr_ref, w2_ref, b2_ref, ng_ref,
         o_ref, st_ref, osum, gc_s) = refs
    c = GLA_CHUNK
    n_chunks = t // c
    slab = 256
    bi = lax.broadcasted_iota(jnp.int32, (slab, slab), 0)
    bj = lax.broadcasted_iota(jnp.int32, (slab, slab), 1)
    same_chunk = (bi // c) == (bj // c)
    lr = lr_ref[...]
    for d in range(2):
        pre = _dot_hp(lr, w2_ref[d, 0]) + b2_ref[d]
        glog = (jnp.minimum(pre, 0.0) - jnp.log1p(jnp.exp(-jnp.abs(pre)))) * (1.0 / GLA_TAU)
        gc_s[d] = _chunk_cumsum(glog, c, d == 1)

    osum[...] = jnp.zeros_like(osum)
    n_slabs = t // slab
    per_slab = slab // c
    causal = [same_chunk & ((bi <= bj) if d == 1 else (bi >= bj)) for d in range(2)]
    own_block = (lax.broadcasted_iota(jnp.int32, (slab, per_slab * GLA_DK), 0) // c
                 == lax.broadcasted_iota(jnp.int32, (slab, per_slab * GLA_DK), 1) // GLA_DK)
    if with_s0:
        s_init = (s0_ref[0, 0, 0].T, s0_ref[0, 1, 0].T)
    else:
        s_init = (jnp.zeros((GLA_DV, GLA_DK), F32), jnp.zeros((GLA_DV, GLA_DK), F32))

    def spread(x):
        return jnp.where(own_block, jnp.concatenate([x] * per_slab, axis=1), jnp.zeros((), x.dtype))

    def chunk_row(d, row):
        return gc_s[d, pl.ds(row, 1), :]

    def per_chunk_rows(d, base, offset):
        return jnp.concatenate([jnp.broadcast_to(chunk_row(d, base + p * c + offset), (c, GLA_DK))
                                for p in range(per_slab)], axis=0)

    def body(i, carry):
        stage = []
        for d in range(2):
            rev = d == 1
            base = pl.multiple_of(((n_slabs - 1 - i) if rev else i) * slab, slab)
            rows = pl.ds(base, slab)
            q = q_ref[rows, :].astype(F32) * (GLA_DK ** -0.5)
            k = k_ref[rows, :].astype(F32)
            v = v_ref[rows, :]
            gc = gc_s[d, rows, :]
            gm = per_chunk_rows(d, base, c // 2)
            gl = per_chunk_rows(d, base, 0 if rev else c - 1)
            scores = _dot_nt((q * jnp.exp(gc - gm)).astype(BF16), (k * jnp.exp(gm - gc)).astype(BF16))
            a_mat = jnp.where(causal[d], scores, 0.0).astype(BF16)
            ds_cat = _dot_tn(v, spread((k * jnp.exp(gl - gc)).astype(BF16)))
            stage.append((base, rows, v, a_mat, ds_cat, spread((q * jnp.exp(gc)).astype(BF16))))
        out = []
        for d in range(2):
            rev = d == 1
            base, rows, v, a_mat, ds_cat, qg_spread = stage[d]
            st = carry[d]
            prev = [None] * per_slab
            for p in (range(per_slab - 1, -1, -1) if rev else range(per_slab)):
                prev[p] = st.astype(BF16)
                gl_p = chunk_row(d, base + p * c + (0 if rev else c - 1))
                st = st * jnp.exp(gl_p) + ds_cat[:, p * GLA_DK:(p + 1) * GLA_DK]
            o = _dot(a_mat, v) + _dot_nt(qg_spread, jnp.concatenate(prev, axis=1))
            osum[rows, :] = osum[rows, :] + o
            out.append(st)
        return tuple(out)

    s_f, s_b = lax.fori_loop(0, n_slabs, body, s_init)
    o = osum[...]
    y = o * lax.rsqrt(jnp.mean(o * o, axis=-1, keepdims=True) + EPS) * ng_ref[...]
    o_ref[...] = (y * _silu(z_ref[...].astype(F32))).astype(o_ref.dtype)
    if not with_s0:
        st_ref[0, 0, 0] = s_f.T
        st_ref[0, 1, 0] = s_b.T


def _gla_call(st, q, k, v, z, lr, p, s0=None):
    t, nseq = st.t, st.nseq
    in_specs = [pl.BlockSpec((t, GLA_DK), lambda s, h: (s, h)),
                pl.BlockSpec((t, GLA_DK), lambda s, h: (s, h)),
                pl.BlockSpec((t, GLA_DV), lambda s, h: (s, h)),
                pl.BlockSpec((t, GLA_DV), lambda s, h: (s, h)),
                pl.BlockSpec((t, LANES), lambda s, h: (s, 0)),
                pl.BlockSpec((2, 1, LANES, GLA_DK), lambda s, h: (0, h, 0, 0)),
                pl.BlockSpec((2, 1, GLA_DK), lambda s, h: (0, 0, h)),
                pl.BlockSpec((1, GLA_DV), lambda s, h: (0, 0))]
    args = [q, k, v, z, lr, p["w2"], p["b2"], p["norm_g"]]
    scratch = [pltpu.VMEM((t, GLA_DV), F32), pltpu.VMEM((2, t, GLA_DK), F32)]
    mix_shape = jax.ShapeDtypeStruct((st.n_tok, GLA_H * GLA_DV), BF16)
    out_mix_spec = pl.BlockSpec((t, GLA_DV), lambda s, h: (s, h))
    st_spec = pl.BlockSpec((1, 2, 1, GLA_DK, GLA_DV), lambda s, h: (s, 0, h, 0, 0))
    if s0 is not None:
        in_specs.append(st_spec)
        args.append(s0)
        return pl.pallas_call(
            functools.partial(_gla_kernel, t=t, with_s0=True),
            grid=(nseq, GLA_H), in_specs=in_specs, out_specs=out_mix_spec, out_shape=mix_shape,
            scratch_shapes=scratch,
            compiler_params=_params("arbitrary", "arbitrary"),
        )(*args)
    return pl.pallas_call(
        functools.partial(_gla_kernel, t=t, with_s0=False),
        grid=(nseq, GLA_H), in_specs=in_specs,
        out_specs=[out_mix_spec, st_spec],
        out_shape=[mix_shape, jax.ShapeDtypeStruct((nseq, 2, GLA_H, GLA_DK, GLA_DV), F32)],
        scratch_shapes=scratch,
        compiler_params=_params("arbitrary", "arbitrary"),
    )(*args)


def _route(sel, s):
    scores = []
    for g in range(N_GROUPS):
        m = sel[g * GROUP_SZ:(g + 1) * GROUP_SZ]
        best = None
        for a in range(GROUP_SZ):
            for b in range(a + 1, GROUP_SZ):
                pair = m[a] + m[b]
                best = pair if best is None else jnp.maximum(best, pair)
        scores.append(best)
    gbest = jnp.zeros_like(scores[0], dtype=jnp.int32)
    top = scores[0]
    for g in range(1, N_GROUPS):
        better = scores[g] > top
        gbest = jnp.where(better, g, gbest)
        top = jnp.where(better, scores[g], top)
    picked = []
    for e in range(N_EXPERTS):
        g = e // GROUP_SZ
        rank = jnp.zeros_like(gbest)
        for m in range(g * GROUP_SZ, (g + 1) * GROUP_SZ):
            if m == e:
                continue
            ahead = (sel[m] > sel[e]) | (sel[m] == sel[e]) if m < e else (sel[m] > sel[e])
            rank = rank + ahead.astype(jnp.int32)
        picked.append(jnp.where((gbest == g) & (rank < 2), s[e], 0.0))
    total = picked[0]
    for e in range(1, N_EXPERTS):
        total = total + picked[e]
    return [p / total for p in picked]


def _outproj_kernel(*refs, n_mix, from_col_major):
    mix_refs = refs[:n_mix]
    w_refs = refs[n_mix:2 * n_mix]
    x_ref, mod_ref, g_ref, rwt_ref, rb_ref, x1_ref, h2_ref, gates_ref = refs[2 * n_mix:]
    gate1 = mod_ref[0, :, 2 * D:3 * D]
    delta = None
    for mix_ref, w_ref in zip(mix_refs, w_refs):
        mix = mix_ref[...]
        if from_col_major:
            mix = _dot(_grid_perm(False), mix).astype(BF16)
        part = _dot(mix, w_ref[...])
        delta = part if delta is None else delta + part
    x = x_ref[0].reshape(TM, D) if from_col_major else x_ref[...]
    x1 = x + gate1 * delta
    x1_ref[...] = x1
    y = x1 * lax.rsqrt(jnp.mean(x1 * x1, axis=-1, keepdims=True) + EPS) * g_ref[...]
    h2 = y * (1.0 + mod_ref[0, :, 4 * D:5 * D]) + mod_ref[0, :, 3 * D:4 * D]
    h2_ref[...] = h2.astype(BF16)
    logits = _dot_hp(rwt_ref[...], h2, dot=_dot_nt)
    s_all = _sigmoid(logits)
    sel_all = s_all + rb_ref[...]
    s = [s_all[e:e + 1, :] for e in range(N_EXPERTS)]
    sel = [sel_all[e:e + 1, :] for e in range(N_EXPERTS)]
    gates_t = jnp.concatenate(_route(sel, s), axis=0)
    ri = lax.broadcasted_iota(jnp.int32, (N_EXPERTS, LANES), 0)
    ci = lax.broadcasted_iota(jnp.int32, (N_EXPERTS, LANES), 1)
    eye = jnp.where(ri == ci, 1.0, 0.0).astype(BF16)
    hi, mid, lo = _split3(gates_t)
    gates_ref[...] = _dot_tn(hi, eye) + _dot_tn(mid, eye) + _dot_tn(lo, eye)


def _outproj(st, mixes, w_outs, x, mod_l, gain2, router_wt, router_b, from_col_major=False):
    n_mix = len(mixes)
    in_specs = [pl.BlockSpec((TM, m.shape[1]), lambda i: (i, 0)) for m in mixes]
    in_specs += [pl.BlockSpec(w.shape, lambda i: (0, 0)) for w in w_outs]
    in_specs += [_grid_tile_spec() if from_col_major else pl.BlockSpec((TM, D), lambda i: (i, 0)),
                 pl.BlockSpec((1, 1, 6 * D), lambda i: (st.mod_row(i, TM), 0, 0)),
                 pl.BlockSpec((1, D), lambda i: (0, 0)),
                 pl.BlockSpec((N_EXPERTS, D), lambda i: (0, 0)),
                 pl.BlockSpec((N_EXPERTS, 1), lambda i: (0, 0))]
    return pl.pallas_call(
        functools.partial(_outproj_kernel, n_mix=n_mix, from_col_major=from_col_major),
        grid=(st.n_tok // TM,),
        in_specs=in_specs,
        out_specs=[pl.BlockSpec((TM, D), lambda i: (i, 0)),
                   pl.BlockSpec((TM, D), lambda i: (i, 0)),
                   pl.BlockSpec((TM, LANES), lambda i: (i, 0))],
        out_shape=[jax.ShapeDtypeStruct((st.n_tok, D), F32),
                   jax.ShapeDtypeStruct((st.n_tok, D), BF16),
                   jax.ShapeDtypeStruct((st.n_tok, LANES), F32)],
        compiler_params=_params("arbitrary"),
    )(*mixes, *w_outs, x, mod_l.reshape(SUBLANES, 1, 6 * D), gain2.reshape(1, D), router_wt,
      router_b.reshape(N_EXPERTS, 1))


def _moe_kernel(h_ref, gates_ref, x1_ref, mod_ref, wg_ref, wu_ref, wd_ref, fg_ref, o_ref, acc, *, final_norm,
                to_grid):
    e = pl.program_id(1)

    @pl.when(e == 0)
    def _():
        acc[...] = jnp.zeros_like(acc)

    h = h_ref[...]
    gates = gates_ref[...]
    lane = lax.broadcasted_iota(jnp.int32, gates.shape, 1)
    gate = jnp.sum(jnp.where(lane == e, gates, 0.0), axis=-1, keepdims=True)
    hid = _silu(_dot(h, wg_ref[0, 0].astype(BF16))) * _dot(h, wu_ref[0, 0].astype(BF16)) * gate
    acc[...] += _dot(hid.astype(BF16), wd_ref[0, 0].astype(BF16))

    @pl.when(e == N_EXPERTS - 1)
    def _():
        out = x1_ref[...] + mod_ref[0, :, 5 * D:6 * D] * acc[...]
        if final_norm:
            out = out * lax.rsqrt(jnp.mean(out * out, axis=-1, keepdims=True) + EPS) * fg_ref[...]
        if to_grid:
            for j in range(TM_MOE // TM):
                o_ref[0, :, j * COLS_PER_TILE:(j + 1) * COLS_PER_TILE, :] = (
                    out[j * TM:(j + 1) * TM].reshape(GRID_H, COLS_PER_TILE, D))
        else:
            o_ref[...] = out


def _moe(st, h2, gates, x1, mod_l, w_gate, w_up, w_down, layer, final_gain, final_norm, to_grid=False):
    tm = TM_MOE
    if to_grid:
        tiles_per_seq = st.t // tm
        cols = tm // GRID_H
        out_spec = pl.BlockSpec((1, GRID_H, cols, D), lambda i, e: (i // tiles_per_seq, 0, i % tiles_per_seq, 0))
        out_shape = jax.ShapeDtypeStruct((st.nseq, GRID_H, GRID_W, D), F32)
    else:
        out_spec = pl.BlockSpec((tm, D), lambda i, e: (i, 0))
        out_shape = jax.ShapeDtypeStruct((st.n_tok, D), F32)
    return pl.pallas_call(
        functools.partial(_moe_kernel, final_norm=final_norm, to_grid=to_grid),
        grid=(st.n_tok // tm, N_EXPERTS),
        in_specs=[pl.BlockSpec((tm, D), lambda i, e: (i, 0)),
                  pl.BlockSpec((tm, LANES), lambda i, e: (i, 0)),
                  pl.BlockSpec((tm, D), lambda i, e: (i, 0)),
                  pl.BlockSpec((1, 1, 6 * D), lambda i, e: (st.mod_row(i, tm), 0, 0)),
                  pl.BlockSpec((1, 1, D, D_EXPERT), lambda i, e: (layer, e, 0, 0)),
                  pl.BlockSpec((1, 1, D, D_EXPERT), lambda i, e: (layer, e, 0, 0)),
                  pl.BlockSpec((1, 1, D_EXPERT, D), lambda i, e: (layer, e, 0, 0)),
                  pl.BlockSpec((1, D), lambda i, e: (0, 0))],
        out_specs=out_spec, out_shape=out_shape,
        scratch_shapes=[pltpu.VMEM((tm, D), F32)],
        compiler_params=_params("arbitrary", "arbitrary"),
    )(h2, gates, x1, mod_l.reshape(SUBLANES, 1, 6 * D), w_gate, w_up, w_down, final_gain.reshape(1, D))


def _layer_ab(xs, mod_l, state_lru, state_gdn, norm1_g, w_in, lru_conv_w, lru_conv_b, lru_wa, lru_ba, lru_wx,
              lru_bx, lru_lam, gdn_conv_w, gdn_conv_b, gdn_a_log, gdn_dt_bias, gdn_norm_g):
    w = w_in.astype(BF16)
    n_qkv = 2 * GDN_H * GDN_DK + GDN_H * GDN_DV
    o = 0
    w_y, o = w[:, o:o + D], o + D
    w_x, o = w[:, o:o + D], o + D
    w_qkv, o = w[:, o:o + n_qkv], o + n_qkv
    w_z, o = w[:, o:o + GDN_H * GDN_DV], o + GDN_H * GDN_DV
    w_ab = w[:, o:]
    w_ab_pad = jnp.pad(w_ab, ((0, 0), (0, LANES - w_ab.shape[1])))
    proj = [_inproj(st, x, mod_l, norm1_g, [w_y, w_x, w_qkv, w_z, w_ab_pad], [BF16, BF16, BF16, BF16, F32],
                    w_t=w_ab.T) for st, x in zip((PROMPT, SAMPLE), xs)]

    nb = D // LRU_BLOCK
    wg = jnp.stack([lru_wa[0], lru_wx[0], lru_wa[1], lru_wx[1]], axis=1)
    wg = wg.transpose(0, 2, 1, 3).reshape(nb, LRU_BLOCK, 4 * LRU_BLOCK).astype(BF16)
    bg = jnp.stack([lru_ba[0], lru_bx[0], lru_ba[1], lru_bx[1]], axis=0)
    bg = bg.reshape(4, nb, LRU_BLOCK).transpose(1, 0, 2).reshape(nb, 1, 4 * LRU_BLOCK)
    lam = lru_lam.reshape(2, nb, LRU_BLOCK).transpose(1, 0, 2)
    lru_p = dict(conv_w=lru_conv_w, conv_b=lru_conv_b.reshape(1, D), wg=wg, bg=bg, lam=lam)
    (ybr_p, xbr_p, qkv_p, z_p, ab_p, abt_p), (ybr_s, xbr_s, qkv_s, z_s, ab_s, abt_s) = proj
    lru_p_out, lru_state = _lru_call(PROMPT, xbr_p, ybr_p, lru_p)
    lru_s_out = _lru_call(SAMPLE, xbr_s, ybr_s, lru_p, h0=state_lru)

    pad16 = lambda v: jnp.pad(v.reshape(1, 2 * GDN_H), ((0, 0), (0, LANES - 2 * GDN_H)))
    col32 = lambda v: jnp.pad(v.reshape(2 * GDN_H, 1), ((0, 2 * GDN_H), (0, 0)))
    gdn_p = dict(conv_w=gdn_conv_w, conv_b=gdn_conv_b.reshape(1, n_qkv), alog_row=pad16(gdn_a_log),
                 dtb_row=pad16(gdn_dt_bias), alog_col=col32(gdn_a_log), dtb_col=col32(gdn_dt_bias),
                 norm_g=gdn_norm_g.reshape(1, GDN_DV))
    gdn_p_out, gdn_state = _gdn_call(PROMPT, qkv_p, z_p, ab_p, abt_p, gdn_p)
    gdn_s_out = _gdn_call(SAMPLE, qkv_s, z_s, ab_s, abt_s, gdn_p, s0=state_gdn)
    return [lru_p_out, gdn_p_out], [lru_s_out, gdn_s_out], lru_state, gdn_state


def _layer_c(xs, mod_l, state_gla, norm1_g, w_in, w2, b2, norm_g):
    w = w_in.astype(BF16)
    nk = GLA_H * GLA_DK
    nv = GLA_H * GLA_DV
    w_q, w_k, w_v, w_z = w[:, 0:nk], w[:, nk:2 * nk], w[:, 2 * nk:2 * nk + nv], w[:, 2 * nk + nv:2 * nk + 2 * nv]
    w_lr = jnp.pad(w[:, 2 * nk + 2 * nv:], ((0, 0), (0, LANES - 2 * GLA_RANK)))
    weights, dtypes = [w_q, w_k, w_v, w_z, w_lr], [BF16, BF16, BF16, BF16, F32]
    proj_p = _inproj(PROMPT, xs[0], mod_l, norm1_g, weights, dtypes)
    proj_s = _inproj(SAMPLE, xs[1], mod_l, norm1_g, weights, dtypes, to_col_major=True)
    w2h = w2.reshape(2, GLA_RANK, GLA_H, GLA_DK).transpose(0, 2, 1, 3)
    w2big = jnp.zeros((2, GLA_H, LANES, GLA_DK), F32)
    w2big = w2big.at[0, :, 0:GLA_RANK].set(w2h[0]).at[1, :, GLA_RANK:2 * GLA_RANK].set(w2h[1])
    gla_p = dict(w2=w2big, b2=b2.reshape(2, 1, nk), norm_g=norm_g.reshape(1, GLA_DV))
    gla_p_out, gla_state = _gla_call(PROMPT, *proj_p, gla_p)
    gla_s_out = _gla_call(SAMPLE, *proj_s, gla_p, s0=state_gla)
    return gla_p_out, gla_s_out, gla_state


def kernel(x_prompt, x_sample, state_lru, state_gdn, state_gla, c, c_ctx, ada_w, ada_b, norm1_g, norm2_g,
           final_norm_g, ab_w_in, lru_conv_w, lru_conv_b, lru_wa, lru_ba, lru_wx, lru_bx, lru_lam, gdn_conv_w,
           gdn_conv_b, gdn_a_log, gdn_dt_bias, gdn_norm_g, ab_w_out, gla_w_in, gla_w2, gla_b2, gla_norm_g,
           gla_w_out, router_w, router_b, moe_w_gate, moe_w_up, moe_w_down):
    xp = x_prompt.reshape(N_PROMPT, D)
    xs = x_sample.reshape(N_SAMPLE, D)
    cvec = jnp.concatenate([c_ctx.reshape(1, D), c, jnp.zeros((SUBLANES - 1 - N_SAMPLE_SEQ, D), F32)], axis=0)
    mod = _modulation(cvec, ada_w, ada_b)
    router_wt = router_w.T
    moe_w = (moe_w_gate, moe_w_up, moe_w_down)

    mix_p, mix_s, lru_state, gdn_state = _layer_ab(
        (xp, xs), mod[0], state_lru[:, 0], state_gdn[:, 0], norm1_g[0], ab_w_in[0], lru_conv_w[0], lru_conv_b[0],
        lru_wa[0], lru_ba[0], lru_wx[0], lru_bx[0], lru_lam[0], gdn_conv_w[0], gdn_conv_b[0], gdn_a_log[0],
        gdn_dt_bias[0], gdn_norm_g[0])
    w_out = ab_w_out[0].astype(BF16)
    w_outs = [w_out[:D], w_out[D:]]
    streams = []
    for st, mixes, x in ((PROMPT, mix_p, xp), (SAMPLE, mix_s, xs)):
        x1, h2, gates = _outproj(st, mixes, w_outs, x, mod[0], norm2_g[0], router_wt, router_b)
        streams.append(_moe(st, h2, gates, x1, mod[0], *moe_w, 0, final_norm_g, False))
    xp, xs = streams
    xs_grid = xs.reshape(N_SAMPLE_SEQ, GRID_H, GRID_W, D)

    gla_p_out, gla_s_out, gla_state = _layer_c((xp, xs_grid), mod[1], state_gla[:, 0], norm1_g[1], gla_w_in[0],
                                               gla_w2[0], gla_b2[0], gla_norm_g[0])
    w_outs = [gla_w_out[0].astype(BF16)]
    x1, h2, gates = _outproj(PROMPT, [gla_p_out], w_outs, xp, mod[1], norm2_g[1], router_wt, router_b)
    y_prompt = _moe(PROMPT, h2, gates, x1, mod[1], *moe_w, 1, final_norm_g, True)
    x1, h2, gates = _outproj(SAMPLE, [gla_s_out], w_outs, xs_grid, mod[1], norm2_g[1], router_wt, router_b,
                             from_col_major=True)
    y_sample = _moe(SAMPLE, h2, gates, x1, mod[1], *moe_w, 1, final_norm_g, True, to_grid=True)

    return (y_prompt.reshape(N_PROMPT_SEQ, T_PROMPT, D), y_sample.reshape(N_SAMPLE_SEQ, T_SAMPLE, D),
            lru_state[:, None], gdn_state[:, None], gla_state[:, None])
```

```python
import functools
import math
from typing import NamedTuple

import jax
import jax.numpy as jnp
from jax import lax
from jax.experimental import pallas as pl
from jax.experimental.pallas import tpu as pltpu

F32 = jnp.float32
BF16 = jnp.bfloat16

D = 1024
N_PROMPT_SEQ, T_PROMPT = 16, 256
N_SAMPLE_SEQ, T_SAMPLE = 4, 2048
N_PROMPT = N_PROMPT_SEQ * T_PROMPT
N_SAMPLE = N_SAMPLE_SEQ * T_SAMPLE
GRID_W = 64
GRID_H = T_SAMPLE // GRID_W
EPS = 1e-6
LANES = 128
SUBLANES = 8

LRU_C = 8.0
LRU_BLOCK = 128
CONV_LEFT = 2
CONV_W = 4
GDN_H, GDN_DK, GDN_DV, GDN_CHUNK = 8, 128, 128, 64
GLA_H, GLA_DK, GLA_DV, GLA_CHUNK, GLA_RANK, GLA_TAU = 4, 128, 256, 32, 16, 16.0
N_EXPERTS, N_GROUPS, D_EXPERT = 16, 4, 512
GROUP_SZ = N_EXPERTS // N_GROUPS

TM = 256
TM_MOE = 1024
MOE_EXPERTS_PER_STEP = 2
VMEM_LIMIT = 60 * 1024 * 1024

_NT = (((1,), (1,)), ((), ()))
_TN = (((0,), (0,)), ((), ()))


def _dot(a, b):
    return jnp.dot(a, b, preferred_element_type=F32)


def _dot_nt(a, b):
    return lax.dot_general(a, b, _NT, preferred_element_type=F32)


def _dot_tn(a, b):
    return lax.dot_general(a, b, _TN, preferred_element_type=F32)


def _split2(x):
    hi = x.astype(BF16)
    lo = (x - hi.astype(F32)).astype(BF16)
    return hi, lo


def _split3(x):
    hi = x.astype(BF16)
    r = x - hi.astype(F32)
    mid = r.astype(BF16)
    lo = (r - mid.astype(F32)).astype(BF16)
    return hi, mid, lo


def _dot_sel(sel_bf16, x):
    hi, mid, lo = _split3(x)
    return _dot(sel_bf16, hi) + _dot(sel_bf16, mid) + _dot(sel_bf16, lo)


def _dot_hp(a, b, dot=_dot):
    ah, al = _split2(a)
    bh, bl = _split2(b)
    return dot(ah, bh) + dot(ah, bl) + dot(al, bh)


def _sigmoid(x):
    return 1.0 / (1.0 + jnp.exp(-x))


def _silu(x):
    return x * _sigmoid(x)


def _softplus(x):
    return jnp.maximum(x, 0.0) + jnp.log1p(jnp.exp(-jnp.abs(x)))


def _gelu_tanh(x):
    return 0.5 * x * (1.0 + jnp.tanh(math.sqrt(2.0 / math.pi) * (x + 0.044715 * (x * x * x))))


def _params(*sem):
    return pltpu.CompilerParams(dimension_semantics=sem, vmem_limit_bytes=VMEM_LIMIT)


class _Stream(NamedTuple):
    n_tok: int
    t: int
    nseq: int
    per_request_mod: bool

    def mod_row(self, i, tm):
        return 1 + i // (self.t // tm) if self.per_request_mod else 0


PROMPT = _Stream(N_PROMPT, T_PROMPT, N_PROMPT_SEQ, False)
SAMPLE = _Stream(N_SAMPLE, T_SAMPLE, N_SAMPLE_SEQ, True)
COLS_PER_TILE = TM // GRID_H


def _grid_perm(to_col_major):
    i = lax.broadcasted_iota(jnp.int32, (TM, TM), 0)
    j = lax.broadcasted_iota(jnp.int32, (TM, TM), 1)
    if to_col_major:
        src = (i % GRID_H) * COLS_PER_TILE + i // GRID_H
    else:
        src = (i % COLS_PER_TILE) * GRID_H + i // COLS_PER_TILE
    return jnp.where(j == src, 1.0, 0.0).astype(BF16)


def _mod_kernel(c_ref, w_ref, b_ref, o_ref):
    c = c_ref[...]
    o_ref[0] = _dot_hp(_silu(c), w_ref[0]) + b_ref[0]


def _modulation(cvec, ada_w, ada_b):
    depth = ada_w.shape[0]
    n6 = ada_w.shape[2]
    return pl.pallas_call(
        _mod_kernel,
        grid=(depth, n6 // D),
        in_specs=[pl.BlockSpec((SUBLANES, D), lambda l, j: (0, 0)),
                  pl.BlockSpec((1, D, D), lambda l, j: (l, 0, j)),
                  pl.BlockSpec((1, 1, D), lambda l, j: (l, 0, j))],
        out_specs=pl.BlockSpec((1, SUBLANES, D), lambda l, j: (l, 0, j)),
        out_shape=jax.ShapeDtypeStruct((depth, SUBLANES, n6), F32),
        compiler_params=_params("arbitrary", "arbitrary"),
    )(cvec, ada_w, ada_b.reshape(depth, 1, n6))


def _grid_tile_spec():
    tiles_per_seq = GRID_W // COLS_PER_TILE
    return pl.BlockSpec((1, GRID_H, COLS_PER_TILE, D), lambda i: (i // tiles_per_seq, 0, i % tiles_per_seq, 0))


def _inproj_kernel(x_ref, mod_ref, g_ref, *refs, n_w, has_t, to_col_major):
    w_refs = refs[:n_w]
    o_refs = refs[n_w + has_t:2 * n_w + has_t]
    x = x_ref[0].reshape(TM, D) if to_col_major else x_ref[...]
    y = x * lax.rsqrt(jnp.mean(x * x, axis=-1, keepdims=True) + EPS) * g_ref[...]
    shift = mod_ref[0, :, 0:D]
    scale = mod_ref[0, :, D:2 * D]
    h = (y * (1.0 + scale) + shift).astype(BF16)
    if to_col_major:
        h = _dot(_grid_perm(True), h).astype(BF16)
    for w_ref, o_ref in zip(w_refs, o_refs):
        o_ref[...] = _dot(h, w_ref[...]).astype(o_ref.dtype)
    if has_t:
        wt_ref = refs[n_w]
        ot_ref = refs[2 * n_w + 1]
        ot_ref[...] = _dot_nt(wt_ref[...], h)


def _inproj(st, x, mod_l, gain, weights, out_dtypes, w_t=None, to_col_major=False):
    n_w = len(weights)
    in_specs = [_grid_tile_spec() if to_col_major else pl.BlockSpec((TM, D), lambda i: (i, 0)),
                pl.BlockSpec((1, 1, 6 * D), lambda i: (st.mod_row(i, TM), 0, 0)),
                pl.BlockSpec((1, D), lambda i: (0, 0))]
    in_specs += [pl.BlockSpec(w.shape, lambda i: (0, 0)) for w in weights]
    out_specs = [pl.BlockSpec((TM, w.shape[1]), lambda i: (i, 0)) for w in weights]
    out_shape = [jax.ShapeDtypeStruct((st.n_tok, w.shape[1]), dt) for w, dt in zip(weights, out_dtypes)]
    args = [x, mod_l.reshape(SUBLANES, 1, 6 * D), gain.reshape(1, D)] + list(weights)
    if w_t is not None:
        in_specs.append(pl.BlockSpec(w_t.shape, lambda i: (0, 0)))
        out_specs.append(pl.BlockSpec((w_t.shape[0], TM), lambda i: (0, i)))
        out_shape.append(jax.ShapeDtypeStruct((w_t.shape[0], st.n_tok), F32))
        args.append(w_t)
    return pl.pallas_call(
        functools.partial(_inproj_kernel, n_w=n_w, has_t=int(w_t is not None), to_col_major=to_col_major),
        grid=(st.n_tok // TM,),
        in_specs=in_specs, out_specs=out_specs, out_shape=out_shape,
        compiler_params=_params("arbitrary"),
    )(*args)


def _conv_from_padded(xp_ref, t, cw, cb):
    acc = cb
    for j in range(CONV_W):
        acc = acc + cw[j:j + 1, :] * xp_ref[pl.ds(SUBLANES - CONV_LEFT + j, t), :]
    return acc


def _fill_padded(xp_ref, x, t):
    zeros = jnp.zeros((SUBLANES, xp_ref.shape[1]), F32)
    xp_ref[pl.ds(0, SUBLANES), :] = zeros
    xp_ref[pl.ds(SUBLANES + t, SUBLANES), :] = zeros
    xp_ref[pl.ds(SUBLANES, t), :] = x


def _chunk_cumsum(x, c, rev):
    t = x.shape[0]
    row = lax.broadcasted_iota(jnp.int32, x.shape, 0) % c
    s = 1
    while s < c:
        if rev:
            x = x + jnp.where(row < c - s, pltpu.roll(x, t - s, 0), 0.0)
        else:
            x = x + jnp.where(row >= s, pltpu.roll(x, s, 0), 0.0)
        s *= 2
    return x


_SCAN_ROWS = 32


def _scan_tile(a, b, carry, rev):
    row = lax.broadcasted_iota(jnp.int32, a.shape, 0)
    for s in (1, 2, 4):
        if rev:
            a_s = pltpu.roll(a, SUBLANES - s, 0)
            b_s = pltpu.roll(b, SUBLANES - s, 0)
            m = row < SUBLANES - s
        else:
            a_s = pltpu.roll(a, s, 0)
            b_s = pltpu.roll(b, s, 0)
            m = row >= s
        b = jnp.where(m, a * b_s + b, b)
        a = jnp.where(m, a * a_s, a)
    h = a * carry + b
    return h, (h[0:1] if rev else h[SUBLANES - 1:SUBLANES])


def _lru_kernel(*refs, t, with_h0):
    if with_h0:
        (x_ref, y_ref, cw_ref, cb_ref, wg_ref, bg_ref, lam_ref, h0_ref,
         o_ref, xp, a_f, b_f, a_b, b_b) = refs
    else:
        (x_ref, y_ref, cw_ref, cb_ref, wg_ref, bg_ref, lam_ref,
         o_ref, st_ref, xp, a_f, b_f, a_b, b_b) = refs
    _fill_padded(xp, x_ref[...].astype(F32), t)
    xc = _conv_from_padded(xp, t, cw_ref[...], cb_ref[...])
    gates = _sigmoid(_dot(xc.astype(BF16), wg_ref[0]) + bg_ref[0])
    sp = _softplus(-lam_ref[0])
    for d, (a_ref, b_ref) in enumerate(((a_f, b_f), (a_b, b_b))):
        r = gates[:, (2 * d) * LRU_BLOCK:(2 * d + 1) * LRU_BLOCK]
        ig = gates[:, (2 * d + 1) * LRU_BLOCK:(2 * d + 2) * LRU_BLOCK]
        log_a = -LRU_C * r * sp[d:d + 1, :]
        a = jnp.exp(log_a)
        a_ref[...] = a
        b_ref[...] = jnp.sqrt(1.0 - a * a) * ig * xc

    n_it = t // _SCAN_ROWS
    if with_h0:
        c0 = (h0_ref[0, 0:1, :], h0_ref[0, 1:2, :])
    else:
        c0 = (jnp.zeros((1, LRU_BLOCK), F32), jnp.zeros((1, LRU_BLOCK), F32))

    def body(i, carry):
        cf, cb_ = carry
        base_f = pl.multiple_of(i * _SCAN_ROWS, _SCAN_ROWS)
        base_b = pl.multiple_of((n_it - 1 - i) * _SCAN_ROWS, _SCAN_ROWS)
        for k in range(_SCAN_ROWS // SUBLANES):
            rf = pl.ds(base_f + k * SUBLANES, SUBLANES)
            h, cf = _scan_tile(a_f[rf, :], b_f[rf, :], cf, False)
            b_f[rf, :] = h
            rb = pl.ds(base_b + _SCAN_ROWS - (k + 1) * SUBLANES, SUBLANES)
            h, cb_ = _scan_tile(a_b[rb, :], b_b[rb, :], cb_, True)
            b_b[rb, :] = h
        return cf, cb_

    cf, cb_ = lax.fori_loop(0, n_it, body, c0)
    o_ref[...] = (_gelu_tanh(y_ref[...].astype(F32)) * (b_f[...] + b_b[...])).astype(o_ref.dtype)
    if not with_h0:
        st_ref[0, 0:1, :] = cf
        st_ref[0, 1:2, :] = cb_


def _lru_call(st, xbr, ybr, p, h0=None):
    nb = D // LRU_BLOCK
    t, nseq = st.t, st.nseq
    tok = lambda s, j: (s, j)
    in_specs = [pl.BlockSpec((t, LRU_BLOCK), tok),
                pl.BlockSpec((t, LRU_BLOCK), tok),
                pl.BlockSpec((CONV_W, LRU_BLOCK), lambda s, j: (0, j)),
                pl.BlockSpec((1, LRU_BLOCK), lambda s, j: (0, j)),
                pl.BlockSpec((1, LRU_BLOCK, 4 * LRU_BLOCK), lambda s, j: (j, 0, 0)),
                pl.BlockSpec((1, 1, 4 * LRU_BLOCK), lambda s, j: (j, 0, 0)),
                pl.BlockSpec((1, 2, LRU_BLOCK), lambda s, j: (j, 0, 0))]
    args = [xbr, ybr, p["conv_w"], p["conv_b"], p["wg"], p["bg"], p["lam"]]
    scratch = [pltpu.VMEM((t + 2 * SUBLANES, LRU_BLOCK), F32)] + [pltpu.VMEM((t, LRU_BLOCK), F32)] * 4
    mix_shape = jax.ShapeDtypeStruct((st.n_tok, D), BF16)
    out_mix_spec = pl.BlockSpec((t, LRU_BLOCK), tok)
    if h0 is not None:
        in_specs.append(pl.BlockSpec((1, 2, LRU_BLOCK), lambda s, j: (s, 0, j)))
        args.append(h0)
        return pl.pallas_call(
            functools.partial(_lru_kernel, t=t, with_h0=True),
            grid=(nseq, nb), in_specs=in_specs, out_specs=out_mix_spec, out_shape=mix_shape,
            scratch_shapes=scratch,
            compiler_params=_params("arbitrary", "arbitrary"),
        )(*args)
    return pl.pallas_call(
        functools.partial(_lru_kernel, t=t, with_h0=False),
        grid=(nseq, nb), in_specs=in_specs,
        out_specs=[out_mix_spec, pl.BlockSpec((1, 2, LRU_BLOCK), lambda s, j: (s, 0, j))],
        out_shape=[mix_shape, jax.ShapeDtypeStruct((nseq, 2, D), F32)],
        scratch_shapes=scratch,
        compiler_params=_params("arbitrary", "arbitrary"),
    )(*args)


GDN_GROUP = 4
GDN_SLAB = GDN_GROUP * GDN_CHUNK


def _compact_select(x, c):
    g = x.shape[0] // c
    lane_blk = lax.broadcasted_iota(jnp.int32, (c, g * c), 1) // c
    out = x[0:c]
    for p in range(1, g):
        out = jnp.where(lane_blk == p, x[p * c:(p + 1) * c], out)
    return out


def _block_diag(xc, same_block):
    g = xc.shape[1] // xc.shape[0]
    return jnp.where(same_block, jnp.concatenate([xc] * g, axis=0), jnp.zeros((), xc.dtype))


def _dot_hp_bd(a, bc, same_block):
    ah, al = _split2(a)
    bh, bl = _split2(bc)
    bdh = _block_diag(bh, same_block)
    return _dot(ah, bdh) + _dot(al, bdh) + _dot(ah, _block_diag(bl, same_block))


GDN_INV_BASE = 16


def _tri_inverse_compact(a_cs, eye_c, same_block):
    c = eye_c.shape[0]
    row = lax.broadcasted_iota(jnp.int32, eye_c.shape, 0)
    col = lax.broadcasted_iota(jnp.int32, eye_c.shape, 1) % c
    diag = (row // GDN_INV_BASE) == (col // GDN_INV_BASE)
    p_cs = [jnp.where(diag, -a_c, 0.0) for a_c in a_cs]
    t_cs = [eye_c + p_c for p_c in p_cs]
    p_cs = [_dot_hp_bd(p_c, p_c, same_block) for p_c in p_cs]
    for _ in range(int(math.log2(GDN_INV_BASE)) - 2):
        xs = [_dot_hp_bd(jnp.concatenate([t_c, p_c], axis=0), p_c, same_block) for t_c, p_c in zip(t_cs, p_cs)]
        t_cs = [t_c + x[0:c] for t_c, x in zip(t_cs, xs)]
        p_cs = [x[c:2 * c] for x in xs]
    t_cs = [t_c + _dot_hp_bd(t_c, p_c, same_block) for t_c, p_c in zip(t_cs, p_cs)]
    b = GDN_INV_BASE
    while b < c:
        off_diag = ((row // (2 * b)) == (col // (2 * b))) & ((row // b) != (col // b))
        xs = [_dot_hp_bd(jnp.where(off_diag, a_c, 0.0), t_c, same_block) for a_c, t_c in zip(a_cs, t_cs)]
        t_cs = [t_c - _dot_hp_bd(t_c, x, same_block) for t_c, x in zip(t_cs, xs)]
        b *= 2
    return t_cs


def _gdn_prepare_group(q, k, v, gcb, gcrow, beta, gl, masks):
    c = GDN_CHUNK
    same_block, eye_c, incl_c, strict_c = masks
    kb = [k * beta[d] for d in range(2)]
    prod = _dot_nt(jnp.concatenate([kb[0], kb[1], q], axis=0).astype(BF16), k.astype(BF16))
    n = q.shape[0]
    qk_c = _compact_select(prod[2 * n:3 * n], c)
    low_half = lax.broadcasted_iota(jnp.int32, (c, LANES), 1) < c
    decay, a_cs = [], []
    for d in range(2):
        gi_c = jnp.concatenate(
            [jnp.where(low_half, gcb[d][(2 * h) * c:(2 * h + 1) * c], gcb[d][(2 * h + 1) * c:(2 * h + 2) * c])
             for h in range(GDN_GROUP // 2)], axis=1)
        diff = gi_c - gcrow[d]
        decay.append(jnp.where(incl_c[d], jnp.exp(jnp.where(incl_c[d], diff, 0.0)), 0.0))
        a_cs.append(jnp.where(strict_c[d], _compact_select(prod[d * n:(d + 1) * n], c) * decay[d], 0.0))
    t_cs = _tri_inverse_compact(a_cs, eye_c, same_block)
    out = []
    for d in range(2):
        eg = jnp.exp(gcb[d])
        rhs = jnp.concatenate([v * beta[d], kb[d] * eg], axis=1).astype(BF16)
        uw = _dot(_block_diag(t_cs[d].astype(BF16), same_block), rhs)
        k_dec_t = (k * jnp.exp(gl[d] - gcb[d])).T
        out.append((uw[:, 0:GDN_DV], uw[:, GDN_DV:].astype(BF16), (q * eg).astype(BF16), qk_c * decay[d], k_dec_t))
    return out


def _gdn_kernel(*refs, t, with_s0):
    if with_s0:
        (q_ref, k_ref, v_ref, z_ref, ab_ref, abt_ref, cwq_ref, cwk_ref, cwv_ref, cbq_ref, cbk_ref, cbv_ref,
         alog_ref, dtb_ref, alogc_ref, dtbc_ref, ng_ref, s0_ref,
         o_ref, xp, qs, ks, vs, osum, gcb_s, beta_s, gcrow_s, u_s, wq_s, qkkd_s) = refs
    else:
        (q_ref, k_ref, v_ref, z_ref, ab_ref, abt_ref, cwq_ref, cwk_ref, cwv_ref, cbq_ref, cbk_ref, cbv_ref,
         alog_ref, dtb_ref, alogc_ref, dtbc_ref, ng_ref,
         o_ref, st_ref, xp, qs, ks, vs, osum, gcb_s, beta_s, gcrow_s, u_s, wq_s, qkkd_s) = refs
    c = GDN_CHUNK
    n_chunks = t // c
    head = pl.program_id(1)

    for x_ref, cw_ref, cb_ref, dst, kind in ((q_ref, cwq_ref, cbq_ref, qs, "q"), (k_ref, cwk_ref, cbk_ref, ks, "k"),
                                             (v_ref, cwv_ref, cbv_ref, vs, "v")):
        _fill_padded(xp, x_ref[...].astype(F32), t)
        y = _silu(_conv_from_padded(xp, t, cw_ref[...], cb_ref[...]))
        if kind != "v":
            y = y * lax.rsqrt(jnp.sum(y * y, axis=-1, keepdims=True) + EPS)
        if kind == "q":
            y = y * (GDN_DK ** -0.5)
        dst[...] = y

    ab = ab_ref[...]
    lane = lax.broadcasted_iota(jnp.int32, ab.shape, 1)
    gall = jnp.where(lane < 2 * GDN_H, -jnp.exp(alog_ref[...]) * _softplus(ab + dtb_ref[...]), _sigmoid(ab))
    pick = lambda idx: jnp.sum(jnp.where(lane == idx, gall, 0.0), axis=-1, keepdims=True)
    abt_rows = lambda idx: abt_ref[idx, 0]
    slab = GDN_SLAB
    bi = lax.broadcasted_iota(jnp.int32, (slab, slab), 0)
    bj = lax.broadcasted_iota(jnp.int32, (slab, slab), 1)
    same_chunk = (bi // c) == (bj // c)
    tri = [jnp.where(same_chunk & ((bi <= bj) if d == 1 else (bi >= bj)), 1.0, 0.0).astype(BF16) for d in range(2)]
    for d in range(2):
        g_col = pick(d * GDN_H + head)
        beta_s[d] = jnp.broadcast_to(pick(2 * GDN_H + d * GDN_H + head), (t, LANES))
        gcb_s[d] = _chunk_cumsum(jnp.broadcast_to(g_col, (t, LANES)), c, d == 1)
        a_row = abt_rows(d * GDN_H + head)
        al = alogc_ref[pl.ds(d * GDN_H + head, 1), :]
        db = dtbc_ref[pl.ds(d * GDN_H + head, 1), :]
        g_rows = -jnp.exp(al) * _softplus(a_row + db)
        gcrow_s[d] = _dot_sel_right(g_rows, tri[1 - d])

    osum[...] = jnp.zeros_like(osum)
    ci = lax.broadcasted_iota(jnp.int32, (c, slab), 0)
    cj = lax.broadcasted_iota(jnp.int32, (c, slab), 1) % c
    masks = (same_chunk, jnp.where(ci == cj, 1.0, 0.0), (ci >= cj, ci <= cj), (ci > cj, ci < cj))
    if with_s0:
        s_init = (s0_ref[0, 0, 0], s0_ref[0, 1, 0])
    else:
        s_init = (jnp.zeros((GDN_DK, GDN_DV), F32), jnp.zeros((GDN_DK, GDN_DV), F32))

    def chunk_total(d, base):
        return gcb_s[d, pl.ds(base if d == 1 else base + c - 1, 1), :]

    def prepare(i, _):
        base = pl.multiple_of(i * slab, slab)
        rows = pl.ds(base, slab)
        gl = [jnp.concatenate([jnp.broadcast_to(chunk_total(d, base + p * c), (c, LANES)) for p in range(GDN_GROUP)],
                              axis=0) for d in range(2)]
        res = _gdn_prepare_group(
            qs[rows, :], ks[rows, :], vs[rows, :], [gcb_s[d, rows, :] for d in range(2)],
            [gcrow_s[d, pl.ds(i, 1), :] for d in range(2)], [beta_s[d, rows, :] for d in range(2)], gl, masks)
        for d, (u, w, qg, qk_c, k_dec_t) in enumerate(res):
            u_s[d, rows, :] = u
            for p in range(GDN_GROUP):
                n = i * GDN_GROUP + p
                dst = pl.multiple_of(base * 2 + p * 2 * c, 2 * c)
                wq_s[d, pl.ds(dst, c), :] = w[p * c:(p + 1) * c]
                wq_s[d, pl.ds(dst + c, c), :] = qg[p * c:(p + 1) * c]
                qkkd_s[d, n, pl.ds(0, c), :] = qk_c[:, p * c:(p + 1) * c].astype(BF16)
                qkkd_s[d, n, pl.ds(c, GDN_DK), :] = k_dec_t[:, p * c:(p + 1) * c].astype(BF16)
        return 0

    lax.fori_loop(0, t // slab, prepare, 0)

    def body(i, carry):
        out = []
        for d in range(2):
            n = (n_chunks - 1 - i) if d == 1 else i
            base = pl.multiple_of(n * c, c)
            rows = pl.ds(base, c)
            s = carry[d]
            ws_qs = _dot(wq_s[d, pl.ds(pl.multiple_of(n * 2 * c, 2 * c), 2 * c), :], s.astype(BF16))
            v_new = (u_s[d, rows, :] - ws_qs[0:c]).astype(BF16)
            upd = _dot(qkkd_s[d, n], v_new)
            out.append(s * jnp.exp(chunk_total(d, base)) + upd[c:c + GDN_DK])
            osum[rows, :] = osum[rows, :] + ws_qs[c:2 * c] + upd[0:c]
        return tuple(out)

    s_f, s_b = lax.fori_loop(0, n_chunks, body, s_init)
    o = osum[...]
    y = o * lax.rsqrt(jnp.mean(o * o, axis=-1, keepdims=True) + EPS) * ng_ref[...]
    o_ref[...] = (y * _silu(z_ref[...].astype(F32))).astype(o_ref.dtype)
    if not with_s0:
        st_ref[0, 0, 0] = s_f
        st_ref[0, 1, 0] = s_b


def _dot_sel_right(x, sel_bf16):
    hi, mid, lo = _split3(x)
    return _dot(hi, sel_bf16) + _dot(mid, sel_bf16) + _dot(lo, sel_bf16)


def _gdn_call(st, qkv, z, ab, abt, p, s0=None):
    c = GDN_SLAB
    t, nseq = st.t, st.nseq
    n_chunks = t // c
    n_pad = max(n_chunks, SUBLANES)
    abt = abt.reshape(4 * GDN_H, nseq, n_chunks, c)
    if n_pad != n_chunks:
        abt = jnp.pad(abt, ((0, 0), (0, 0), (0, n_pad - n_chunks), (0, 0)))
    nqk = GDN_H * GDN_DK // LANES
    in_specs = [pl.BlockSpec((t, LANES), lambda s, h: (s, h)),
                pl.BlockSpec((t, LANES), lambda s, h: (s, nqk + h)),
                pl.BlockSpec((t, LANES), lambda s, h: (s, 2 * nqk + h)),
                pl.BlockSpec((t, LANES), lambda s, h: (s, h)),
                pl.BlockSpec((t, LANES), lambda s, h: (s, 0)),
                pl.BlockSpec((4 * GDN_H, 1, n_pad, c), lambda s, h: (0, s, 0, 0)),
                pl.BlockSpec((CONV_W, LANES), lambda s, h: (0, h)),
                pl.BlockSpec((CONV_W, LANES), lambda s, h: (0, nqk + h)),
                pl.BlockSpec((CONV_W, LANES), lambda s, h: (0, 2 * nqk + h)),
                pl.BlockSpec((1, LANES), lambda s, h: (0, h)),
                pl.BlockSpec((1, LANES), lambda s, h: (0, nqk + h)),
                pl.BlockSpec((1, LANES), lambda s, h: (0, 2 * nqk + h)),
                pl.BlockSpec((1, LANES), lambda s, h: (0, 0)),
                pl.BlockSpec((1, LANES), lambda s, h: (0, 0)),
                pl.BlockSpec((4 * GDN_H, 1), lambda s, h: (0, 0)),
                pl.BlockSpec((4 * GDN_H, 1), lambda s, h: (0, 0)),
                pl.BlockSpec((1, LANES), lambda s, h: (0, 0))]
    args = [qkv, qkv, qkv, z, ab, abt, p["conv_w"], p["conv_w"], p["conv_w"], p["conv_b"], p["conv_b"], p["conv_b"],
            p["alog_row"], p["dtb_row"], p["alog_col"], p["dtb_col"], p["norm_g"]]
    scratch = [pltpu.VMEM((t + 2 * SUBLANES, LANES), F32)] + [pltpu.VMEM((t, LANES), F32)] * 4 + [
        pltpu.VMEM((2, t, LANES), F32), pltpu.VMEM((2, t, LANES), F32), pltpu.VMEM((2, n_pad, c), F32),
        pltpu.VMEM((2, t, GDN_DV), F32), pltpu.VMEM((2, 2 * t, GDN_DK), BF16),
        pltpu.VMEM((2, t // GDN_CHUNK, GDN_CHUNK + GDN_DK, GDN_CHUNK), BF16)]
    mix_shape = jax.ShapeDtypeStruct((st.n_tok, GDN_H * GDN_DV), BF16)
    out_mix_spec = pl.BlockSpec((t, LANES), lambda s, h: (s, h))
    st_spec = pl.BlockSpec((1, 2, 1, GDN_DK, GDN_DV), lambda s, h: (s, 0, h, 0, 0))
    if s0 is not None:
        in_specs.append(st_spec)
        args.append(s0)
        return pl.pallas_call(
            functools.partial(_gdn_kernel, t=t, with_s0=True),
            grid=(nseq, GDN_H), in_specs=in_specs, out_specs=out_mix_spec, out_shape=mix_shape,
            scratch_shapes=scratch,
            compiler_params=_params("arbitrary", "arbitrary"),
        )(*args)
    return pl.pallas_call(
        functools.partial(_gdn_kernel, t=t, with_s0=False),
        grid=(nseq, GDN_H), in_specs=in_specs,
        out_specs=[out_mix_spec, st_spec],
        out_shape=[mix_shape, jax.ShapeDtypeStruct((nseq, 2, GDN_H, GDN_DK, GDN_DV), F32)],
        scratch_shapes=scratch,
        compiler_params=_params("arbitrary", "arbitrary"),
    )(*args)


def _gla_kernel(*refs, t, with_s0):
    if with_s0:
        (q_ref, k_ref, v_ref, z_ref, lr_ref, w2_ref, b2_ref, ng_ref, s0_ref,
         o_ref, osum, gc_s) = refs
    else:
        (q_ref, k_ref, v_ref, z_ref, lr_ref, w2_ref, b2_ref, ng_ref,
         o_ref, st_ref, osum, gc_s) = refs
    c = GLA_CHUNK
    n_chunks = t // c
    slab = 256
    bi = lax.broadcasted_iota(jnp.int32, (slab, slab), 0)
    bj = lax.broadcasted_iota(jnp.int32, (slab, slab), 1)
    same_chunk = (bi // c) == (bj // c)
    lr = lr_ref[...]
    for d in range(2):
        pre = _dot_hp(lr, w2_ref[d, 0]) + b2_ref[d]
        glog = (jnp.minimum(pre, 0.0) - jnp.log1p(jnp.exp(-jnp.abs(pre)))) * (1.0 / GLA_TAU)
        gc_s[d] = _chunk_cumsum(glog, c, d == 1)

    osum[...] = jnp.zeros_like(osum)
    n_slabs = t // slab
    per_slab = slab // c
    causal = [same_chunk & ((bi <= bj) if d == 1 else (bi >= bj)) for d in range(2)]
    own_block = (lax.broadcasted_iota(jnp.int32, (slab, per_slab * GLA_DK), 0) // c
                 == lax.broadcasted_iota(jnp.int32, (slab, per_slab * GLA_DK), 1) // GLA_DK)
    if with_s0:
        s_init = (s0_ref[0, 0, 0].T, s0_ref[0, 1, 0].T)
    else:
        s_init = (jnp.zeros((GLA_DV, GLA_DK), F32), jnp.zeros((GLA_DV, GLA_DK), F32))

    def spread(x):
        return jnp.where(own_block, jnp.concatenate([x] * per_slab, axis=1), jnp.zeros((), x.dtype))

    def chunk_row(d, row):
        return gc_s[d, pl.ds(row, 1), :]

    def per_chunk_rows(d, base, offset):
        return jnp.concatenate([jnp.broadcast_to(chunk_row(d, base + p * c + offset), (c, GLA_DK))
                                for p in range(per_slab)], axis=0)

    def body(i, carry):
        stage = []
        for d in range(2):
            rev = d == 1
            base = pl.multiple_of(((n_slabs - 1 - i) if rev else i) * slab, slab)
            rows = pl.ds(base, slab)
            q = q_ref[rows, :].astype(F32) * (GLA_DK ** -0.5)
            k = k_ref[rows, :].astype(F32)
            v = v_ref[rows, :]
            gc = gc_s[d, rows, :]
            gm = per_chunk_rows(d, base, c // 2)
            gl = per_chunk_rows(d, base, 0 if rev else c - 1)
            scores = _dot_nt((q * jnp.exp(gc - gm)).astype(BF16), (k * jnp.exp(gm - gc)).astype(BF16))
            a_mat = jnp.where(causal[d], scores, 0.0).astype(BF16)
            ds_cat = _dot_tn(v, spread((k * jnp.exp(gl - gc)).astype(BF16)))
            stage.append((base, rows, v, a_mat, ds_cat, spread((q * jnp.exp(gc)).astype(BF16))))
        out = []
        for d in range(2):
            rev = d == 1
            base, rows, v, a_mat, ds_cat, qg_spread = stage[d]
            st = carry[d]
            prev = [None] * per_slab
            for p in (range(per_slab - 1, -1, -1) if rev else range(per_slab)):
                prev[p] = st.astype(BF16)
                gl_p = chunk_row(d, base + p * c + (0 if rev else c - 1))
                st = st * jnp.exp(gl_p) + ds_cat[:, p * GLA_DK:(p + 1) * GLA_DK]
            o = _dot(a_mat, v) + _dot_nt(qg_spread, jnp.concatenate(prev, axis=1))
            osum[rows, :] = osum[rows, :] + o
            out.append(st)
        return tuple(out)

    s_f, s_b = lax.fori_loop(0, n_slabs, body, s_init)
    o = osum[...]
    y = o * lax.rsqrt(jnp.mean(o * o, axis=-1, keepdims=True) + EPS) * ng_ref[...]
    o_ref[...] = (y * _silu(z_ref[...].astype(F32))).astype(o_ref.dtype)
    if not with_s0:
        st_ref[0, 0, 0] = s_f.T
        st_ref[0, 1, 0] = s_b.T


def _gla_call(st, q, k, v, z, lr, p, s0=None):
    t, nseq = st.t, st.nseq
    in_specs = [pl.BlockSpec((t, GLA_DK), lambda s, h: (s, h)),
                pl.BlockSpec((t, GLA_DK), lambda s, h: (s, h)),
                pl.BlockSpec((t, GLA_DV), lambda s, h: (s, h)),
                pl.BlockSpec((t, GLA_DV), lambda s, h: (s, h)),
                pl.BlockSpec((t, LANES), lambda s, h: (s, 0)),
                pl.BlockSpec((2, 1, LANES, GLA_DK), lambda s, h: (0, h, 0, 0)),
                pl.BlockSpec((2, 1, GLA_DK), lambda s, h: (0, 0, h)),
                pl.BlockSpec((1, GLA_DV), lambda s, h: (0, 0))]
    args = [q, k, v, z, lr, p["w2"], p["b2"], p["norm_g"]]
    scratch = [pltpu.VMEM((t, GLA_DV), F32), pltpu.VMEM((2, t, GLA_DK), F32)]
    mix_shape = jax.ShapeDtypeStruct((st.n_tok, GLA_H * GLA_DV), BF16)
    out_mix_spec = pl.BlockSpec((t, GLA_DV), lambda s, h: (s, h))
    st_spec = pl.BlockSpec((1, 2, 1, GLA_DK, GLA_DV), lambda s, h: (s, 0, h, 0, 0))
    if s0 is not None:
        in_specs.append(st_spec)
        args.append(s0)
        return pl.pallas_call(
            functools.partial(_gla_kernel, t=t, with_s0=True),
            grid=(nseq, GLA_H), in_specs=in_specs, out_specs=out_mix_spec, out_shape=mix_shape,
            scratch_shapes=scratch,
            compiler_params=_params("arbitrary", "arbitrary"),
        )(*args)
    return pl.pallas_call(
        functools.partial(_gla_kernel, t=t, with_s0=False),
        grid=(nseq, GLA_H), in_specs=in_specs,
        out_specs=[out_mix_spec, st_spec],
        out_shape=[mix_shape, jax.ShapeDtypeStruct((nseq, 2, GLA_H, GLA_DK, GLA_DV), F32)],
        scratch_shapes=scratch,
        compiler_params=_params("arbitrary", "arbitrary"),
    )(*args)


def _route(sel, s):
    scores = []
    for g in range(N_GROUPS):
        m = sel[g * GROUP_SZ:(g + 1) * GROUP_SZ]
        best = None
        for a in range(GROUP_SZ):
            for b in range(a + 1, GROUP_SZ):
                pair = m[a] + m[b]
                best = pair if best is None else jnp.maximum(best, pair)
        scores.append(best)
    gbest = jnp.zeros_like(scores[0], dtype=jnp.int32)
    top = scores[0]
    for g in range(1, N_GROUPS):
        better = scores[g] > top
        gbest = jnp.where(better, g, gbest)
        top = jnp.where(better, scores[g], top)
    picked = []
    for e in range(N_EXPERTS):
        g = e // GROUP_SZ
        rank = jnp.zeros_like(gbest)
        for m in range(g * GROUP_SZ, (g + 1) * GROUP_SZ):
            if m == e:
                continue
            ahead = (sel[m] > sel[e]) | (sel[m] == sel[e]) if m < e else (sel[m] > sel[e])
            rank = rank + ahead.astype(jnp.int32)
        picked.append(jnp.where((gbest == g) & (rank < 2), s[e], 0.0))
    total = picked[0]
    for e in range(1, N_EXPERTS):
        total = total + picked[e]
    return [p / total for p in picked]


def _outproj_kernel(*refs, n_mix, from_col_major):
    mix_refs = refs[:n_mix]
    w_refs = refs[n_mix:2 * n_mix]
    x_ref, mod_ref, g_ref, rwt_ref, rb_ref, x1_ref, h2_ref, gates_ref = refs[2 * n_mix:]
    gate1 = mod_ref[0, :, 2 * D:3 * D]
    delta = None
    for mix_ref, w_ref in zip(mix_refs, w_refs):
        mix = mix_ref[...]
        if from_col_major:
            mix = _dot(_grid_perm(False), mix).astype(BF16)
        part = _dot(mix, w_ref[...])
        delta = part if delta is None else delta + part
    x = x_ref[0].reshape(TM, D) if from_col_major else x_ref[...]
    x1 = x + gate1 * delta
    x1_ref[...] = x1
    y = x1 * lax.rsqrt(jnp.mean(x1 * x1, axis=-1, keepdims=True) + EPS) * g_ref[...]
    h2 = y * (1.0 + mod_ref[0, :, 4 * D:5 * D]) + mod_ref[0, :, 3 * D:4 * D]
    h2_ref[...] = h2.astype(BF16)
    rw_hi, rw_lo = _split2(rwt_ref[...])
    h2_hi, h2_lo = _split2(h2)
    both = _dot_nt(jnp.concatenate([rw_hi, rw_lo], axis=0), h2_hi)
    logits = both[0:N_EXPERTS] + both[N_EXPERTS:2 * N_EXPERTS] + _dot_nt(rw_hi, h2_lo)
    s_all = _sigmoid(logits)
    sel_all = s_all + rb_ref[...]
    s = [s_all[e:e + 1, :] for e in range(N_EXPERTS)]
    sel = [sel_all[e:e + 1, :] for e in range(N_EXPERTS)]
    gate_rows = _route(sel, s) + [jnp.zeros((LANES - N_EXPERTS, TM), F32)]
    gates_ref[...] = jnp.concatenate(gate_rows, axis=0).T


def _outproj(st, mixes, w_outs, x, mod_l, gain2, router_wt, router_b, from_col_major=False):
    n_mix = len(mixes)
    in_specs = [pl.BlockSpec((TM, m.shape[1]), lambda i: (i, 0)) for m in mixes]
    in_specs += [pl.BlockSpec(w.shape, lambda i: (0, 0)) for w in w_outs]
    in_specs += [_grid_tile_spec() if from_col_major else pl.BlockSpec((TM, D), lambda i: (i, 0)),
                 pl.BlockSpec((1, 1, 6 * D), lambda i: (st.mod_row(i, TM), 0, 0)),
                 pl.BlockSpec((1, D), lambda i: (0, 0)),
                 pl.BlockSpec((N_EXPERTS, D), lambda i: (0, 0)),
                 pl.BlockSpec((N_EXPERTS, 1), lambda i: (0, 0))]
    return pl.pallas_call(
        functools.partial(_outproj_kernel, n_mix=n_mix, from_col_major=from_col_major),
        grid=(st.n_tok // TM,),
        in_specs=in_specs,
        out_specs=[pl.BlockSpec((TM, D), lambda i: (i, 0)),
                   pl.BlockSpec((TM, D), lambda i: (i, 0)),
                   pl.BlockSpec((TM, LANES), lambda i: (i, 0))],
        out_shape=[jax.ShapeDtypeStruct((st.n_tok, D), F32),
                   jax.ShapeDtypeStruct((st.n_tok, D), BF16),
                   jax.ShapeDtypeStruct((st.n_tok, LANES), F32)],
        compiler_params=_params("arbitrary"),
    )(*mixes, *w_outs, x, mod_l.reshape(SUBLANES, 1, 6 * D), gain2.reshape(1, D), router_wt,
      router_b.reshape(N_EXPERTS, 1))


def _moe_kernel(h_ref, gates_ref, x1_ref, mod_ref, wg_ref, wu_ref, wd_ref, fg_ref, o_ref, acc, *, final_norm,
                to_grid):
    e = pl.program_id(1)

    @pl.when(e == 0)
    def _():
        acc[...] = jnp.zeros_like(acc)

    h = h_ref[...]
    gates = gates_ref[...]
    lane = lax.broadcasted_iota(jnp.int32, gates.shape, 1)
    total = None
    for k in range(MOE_EXPERTS_PER_STEP):
        gate = jnp.sum(jnp.where(lane == e * MOE_EXPERTS_PER_STEP + k, gates, 0.0), axis=-1, keepdims=True)
        hid = _silu(_dot(h, wg_ref[0, k].astype(BF16))) * _dot(h, wu_ref[0, k].astype(BF16)) * gate
        part = _dot(hid.astype(BF16), wd_ref[0, k].astype(BF16))
        total = part if total is None else total + part
    acc[...] += total

    @pl.when(e == N_EXPERTS // MOE_EXPERTS_PER_STEP - 1)
    def _():
        out = x1_ref[...] + mod_ref[0, :, 5 * D:6 * D] * acc[...]
        if final_norm:
            out = out * lax.rsqrt(jnp.mean(out * out, axis=-1, keepdims=True) + EPS) * fg_ref[...]
        if to_grid:
            for j in range(TM_MOE // TM):
                o_ref[0, :, j * COLS_PER_TILE:(j + 1) * COLS_PER_TILE, :] = (
                    out[j * TM:(j + 1) * TM].reshape(GRID_H, COLS_PER_TILE, D))
        else:
            o_ref[...] = out


def _moe(st, h2, gates, x1, mod_l, w_gate, w_up, w_down, layer, final_gain, final_norm, to_grid=False):
    tm = TM_MOE
    if to_grid:
        tiles_per_seq = st.t // tm
        cols = tm // GRID_H
        out_spec = pl.BlockSpec((1, GRID_H, cols, D), lambda i, e: (i // tiles_per_seq, 0, i % tiles_per_seq, 0))
        out_shape = jax.ShapeDtypeStruct((st.nseq, GRID_H, GRID_W, D), F32)
    else:
        out_spec = pl.BlockSpec((tm, D), lambda i, e: (i, 0))
        out_shape = jax.ShapeDtypeStruct((st.n_tok, D), F32)
    return pl.pallas_call(
        functools.partial(_moe_kernel, final_norm=final_norm, to_grid=to_grid),
        grid=(st.n_tok // tm, N_EXPERTS // MOE_EXPERTS_PER_STEP),
        in_specs=[pl.BlockSpec((tm, D), lambda i, e: (i, 0)),
                  pl.BlockSpec((tm, LANES), lambda i, e: (i, 0)),
                  pl.BlockSpec((tm, D), lambda i, e: (i, 0)),
                  pl.BlockSpec((1, 1, 6 * D), lambda i, e: (st.mod_row(i, tm), 0, 0)),
                  pl.BlockSpec((1, MOE_EXPERTS_PER_STEP, D, D_EXPERT), lambda i, e: (layer, e, 0, 0)),
                  pl.BlockSpec((1, MOE_EXPERTS_PER_STEP, D, D_EXPERT), lambda i, e: (layer, e, 0, 0)),
                  pl.BlockSpec((1, MOE_EXPERTS_PER_STEP, D_EXPERT, D), lambda i, e: (layer, e, 0, 0)),
                  pl.BlockSpec((1, D), lambda i, e: (0, 0))],
        out_specs=out_spec, out_shape=out_shape,
        scratch_shapes=[pltpu.VMEM((tm, D), F32)],
        compiler_params=_params("arbitrary", "arbitrary"),
    )(h2, gates, x1, mod_l.reshape(SUBLANES, 1, 6 * D), w_gate, w_up, w_down, final_gain.reshape(1, D))


def _layer_ab(xs, mod_l, state_lru, state_gdn, norm1_g, w_in, lru_conv_w, lru_conv_b, lru_wa, lru_ba, lru_wx,
              lru_bx, lru_lam, gdn_conv_w, gdn_conv_b, gdn_a_log, gdn_dt_bias, gdn_norm_g):
    w = w_in.astype(BF16)
    n_qkv = 2 * GDN_H * GDN_DK + GDN_H * GDN_DV
    o = 0
    w_y, o = w[:, o:o + D], o + D
    w_x, o = w[:, o:o + D], o + D
    w_qkv, o = w[:, o:o + n_qkv], o + n_qkv
    w_z, o = w[:, o:o + GDN_H * GDN_DV], o + GDN_H * GDN_DV
    w_ab = w[:, o:]
    w_ab_pad = jnp.pad(w_ab, ((0, 0), (0, LANES - w_ab.shape[1])))
    proj = [_inproj(st, x, mod_l, norm1_g, [w_y, w_x, w_qkv, w_z, w_ab_pad], [BF16, BF16, BF16, BF16, F32],
                    w_t=w_ab.T) for st, x in zip((PROMPT, SAMPLE), xs)]

    nb = D // LRU_BLOCK
    wg = jnp.stack([lru_wa[0], lru_wx[0], lru_wa[1], lru_wx[1]], axis=1)
    wg = wg.transpose(0, 2, 1, 3).reshape(nb, LRU_BLOCK, 4 * LRU_BLOCK).astype(BF16)
    bg = jnp.stack([lru_ba[0], lru_bx[0], lru_ba[1], lru_bx[1]], axis=0)
    bg = bg.reshape(4, nb, LRU_BLOCK).transpose(1, 0, 2).reshape(nb, 1, 4 * LRU_BLOCK)
    lam = lru_lam.reshape(2, nb, LRU_BLOCK).transpose(1, 0, 2)
    lru_p = dict(conv_w=lru_conv_w, conv_b=lru_conv_b.reshape(1, D), wg=wg, bg=bg, lam=lam)
    (ybr_p, xbr_p, qkv_p, z_p, ab_p, abt_p), (ybr_s, xbr_s, qkv_s, z_s, ab_s, abt_s) = proj
    lru_p_out, lru_state = _lru_call(PROMPT, xbr_p, ybr_p, lru_p)
    lru_s_out = _lru_call(SAMPLE, xbr_s, ybr_s, lru_p, h0=state_lru)

    pad16 = lambda v: jnp.pad(v.reshape(1, 2 * GDN_H), ((0, 0), (0, LANES - 2 * GDN_H)))
    col32 = lambda v: jnp.pad(v.reshape(2 * GDN_H, 1), ((0, 2 * GDN_H), (0, 0)))
    gdn_p = dict(conv_w=gdn_conv_w, conv_b=gdn_conv_b.reshape(1, n_qkv), alog_row=pad16(gdn_a_log),
                 dtb_row=pad16(gdn_dt_bias), alog_col=col32(gdn_a_log), dtb_col=col32(gdn_dt_bias),
                 norm_g=gdn_norm_g.reshape(1, GDN_DV))
    gdn_p_out, gdn_state = _gdn_call(PROMPT, qkv_p, z_p, ab_p, abt_p, gdn_p)
    gdn_s_out = _gdn_call(SAMPLE, qkv_s, z_s, ab_s, abt_s, gdn_p, s0=state_gdn)
    return [lru_p_out, gdn_p_out], [lru_s_out, gdn_s_out], lru_state, gdn_state


def _layer_c(xs, mod_l, state_gla, norm1_g, w_in, w2, b2, norm_g):
    w = w_in.astype(BF16)
    nk = GLA_H * GLA_DK
    nv = GLA_H * GLA_DV
    w_q, w_k, w_v, w_z = w[:, 0:nk], w[:, nk:2 * nk], w[:, 2 * nk:2 * nk + nv], w[:, 2 * nk + nv:2 * nk + 2 * nv]
    w_lr = jnp.pad(w[:, 2 * nk + 2 * nv:], ((0, 0), (0, LANES - 2 * GLA_RANK)))
    weights, dtypes = [w_q, w_k, w_v, w_z, w_lr], [BF16, BF16, BF16, BF16, F32]
    proj_p = _inproj(PROMPT, xs[0], mod_l, norm1_g, weights, dtypes)
    proj_s = _inproj(SAMPLE, xs[1], mod_l, norm1_g, weights, dtypes, to_col_major=True)
    w2h = w2.reshape(2, GLA_RANK, GLA_H, GLA_DK).transpose(0, 2, 1, 3)
    w2big = jnp.zeros((2, GLA_H, LANES, GLA_DK), F32)
    w2big = w2big.at[0, :, 0:GLA_RANK].set(w2h[0]).at[1, :, GLA_RANK:2 * GLA_RANK].set(w2h[1])
    gla_p = dict(w2=w2big, b2=b2.reshape(2, 1, nk), norm_g=norm_g.reshape(1, GLA_DV))
    gla_p_out, gla_state = _gla_call(PROMPT, *proj_p, gla_p)
    gla_s_out = _gla_call(SAMPLE, *proj_s, gla_p, s0=state_gla)
    return gla_p_out, gla_s_out, gla_state


def kernel(x_prompt, x_sample, state_lru, state_gdn, state_gla, c, c_ctx, ada_w, ada_b, norm1_g, norm2_g,
           final_norm_g, ab_w_in, lru_conv_w, lru_conv_b, lru_wa, lru_ba, lru_wx, lru_bx, lru_lam, gdn_conv_w,
           gdn_conv_b, gdn_a_log, gdn_dt_bias, gdn_norm_g, ab_w_out, gla_w_in, gla_w2, gla_b2, gla_norm_g,
           gla_w_out, router_w, router_b, moe_w_gate, moe_w_up, moe_w_down):
    xp = x_prompt.reshape(N_PROMPT, D)
    xs = x_sample.reshape(N_SAMPLE, D)
    cvec = jnp.concatenate([c_ctx.reshape(1, D), c, jnp.zeros((SUBLANES - 1 - N_SAMPLE_SEQ, D), F32)], axis=0)
    mod = _modulation(cvec, ada_w, ada_b)
    router_wt = router_w.T
    moe_w = (moe_w_gate, moe_w_up, moe_w_down)

    mix_p, mix_s, lru_state, gdn_state = _layer_ab(
        (xp, xs), mod[0], state_lru[:, 0], state_gdn[:, 0], norm1_g[0], ab_w_in[0], lru_conv_w[0], lru_conv_b[0],
        lru_wa[0], lru_ba[0], lru_wx[0], lru_bx[0], lru_lam[0], gdn_conv_w[0], gdn_conv_b[0], gdn_a_log[0],
        gdn_dt_bias[0], gdn_norm_g[0])
    w_out = ab_w_out[0].astype(BF16)
    w_outs = [w_out[:D], w_out[D:]]
    streams = []
    for st, mixes, x in ((PROMPT, mix_p, xp), (SAMPLE, mix_s, xs)):
        x1, h2, gates = _outproj(st, mixes, w_outs, x, mod[0], norm2_g[0], router_wt, router_b)
        streams.append(_moe(st, h2, gates, x1, mod[0], *moe_w, 0, final_norm_g, False))
    xp, xs = streams
    xs_grid = xs.reshape(N_SAMPLE_SEQ, GRID_H, GRID_W, D)

    gla_p_out, gla_s_out, gla_state = _layer_c((xp, xs_grid), mod[1], state_gla[:, 0], norm1_g[1], gla_w_in[0],
                                               gla_w2[0], gla_b2[0], gla_norm_g[0])
    w_outs = [gla_w_out[0].astype(BF16)]
    x1, h2, gates = _outproj(PROMPT, [gla_p_out], w_outs, xp, mod[1], norm2_g[1], router_wt, router_b)
    y_prompt = _moe(PROMPT, h2, gates, x1, mod[1], *moe_w, 1, final_norm_g, True)
    x1, h2, gates = _outproj(SAMPLE, [gla_s_out], w_outs, xs_grid, mod[1], norm2_g[1], router_wt, router_b,
                             from_col_major=True)
    y_sample = _moe(SAMPLE, h2, gates, x1, mod[1], *moe_w, 1, final_norm_g, True, to_grid=True)

    return (y_prompt.reshape(N_PROMPT_SEQ, T_PROMPT, D), y_sample.reshape(N_SAMPLE_SEQ, T_SAMPLE, D),
            lru_state[:, None], gdn_state[:, None], gla_state[:, None])
```

```python
import functools
import math
from typing import NamedTuple

import jax
import jax.numpy as jnp
from jax import lax
from jax.experimental import pallas as pl
from jax.experimental.pallas import tpu as pltpu

F32 = jnp.float32
BF16 = jnp.bfloat16

D = 1024
N_PROMPT_SEQ, T_PROMPT = 16, 256
N_SAMPLE_SEQ, T_SAMPLE = 4, 2048
N_PROMPT = N_PROMPT_SEQ * T_PROMPT
N_SAMPLE = N_SAMPLE_SEQ * T_SAMPLE
GRID_W = 64
GRID_H = T_SAMPLE // GRID_W
EPS = 1e-6
LANES = 128
SUBLANES = 8

LRU_C = 8.0
LRU_BLOCK = 128
CONV_LEFT = 2
CONV_W = 4
GDN_H, GDN_DK, GDN_DV, GDN_CHUNK = 8, 128, 128, 64
GLA_H, GLA_DK, GLA_DV, GLA_CHUNK, GLA_RANK, GLA_TAU = 4, 128, 256, 32, 16, 16.0
N_EXPERTS, N_GROUPS, D_EXPERT = 16, 4, 512
GROUP_SZ = N_EXPERTS // N_GROUPS

TM = 256
TM_MOE = 1024
MOE_EXPERTS_PER_STEP = 2
VMEM_LIMIT = 60 * 1024 * 1024

_NT = (((1,), (1,)), ((), ()))
_TN = (((0,), (0,)), ((), ()))


def _dot(a, b):
    return jnp.dot(a, b, preferred_element_type=F32)


def _dot_nt(a, b):
    return lax.dot_general(a, b, _NT, preferred_element_type=F32)


def _dot_tn(a, b):
    return lax.dot_general(a, b, _TN, preferred_element_type=F32)


def _split2(x):
    hi = x.astype(BF16)
    lo = (x - hi.astype(F32)).astype(BF16)
    return hi, lo


def _split3(x):
    hi = x.astype(BF16)
    r = x - hi.astype(F32)
    mid = r.astype(BF16)
    lo = (r - mid.astype(F32)).astype(BF16)
    return hi, mid, lo


def _dot_sel(sel_bf16, x):
    hi, mid, lo = _split3(x)
    return _dot(sel_bf16, hi) + _dot(sel_bf16, mid) + _dot(sel_bf16, lo)


def _dot_hp(a, b, dot=_dot):
    ah, al = _split2(a)
    bh, bl = _split2(b)
    return dot(ah, bh) + dot(ah, bl) + dot(al, bh)


def _sigmoid(x):
    return 1.0 / (1.0 + jnp.exp(-x))


def _silu(x):
    return x * _sigmoid(x)


def _softplus(x):
    return jnp.maximum(x, 0.0) + jnp.log1p(jnp.exp(-jnp.abs(x)))


def _gelu_tanh(x):
    return 0.5 * x * (1.0 + jnp.tanh(math.sqrt(2.0 / math.pi) * (x + 0.044715 * (x * x * x))))


def _params(*sem):
    return pltpu.CompilerParams(dimension_semantics=sem, vmem_limit_bytes=VMEM_LIMIT)


class _Stream(NamedTuple):
    n_tok: int
    t: int
    nseq: int
    per_request_mod: bool

    def mod_row(self, i, tm):
        return 1 + i // (self.t // tm) if self.per_request_mod else 0


PROMPT = _Stream(N_PROMPT, T_PROMPT, N_PROMPT_SEQ, False)
SAMPLE = _Stream(N_SAMPLE, T_SAMPLE, N_SAMPLE_SEQ, True)
COLS_PER_TILE = TM // GRID_H


def _grid_perm(to_col_major):
    i = lax.broadcasted_iota(jnp.int32, (TM, TM), 0)
    j = lax.broadcasted_iota(jnp.int32, (TM, TM), 1)
    if to_col_major:
        src = (i % GRID_H) * COLS_PER_TILE + i // GRID_H
    else:
        src = (i % COLS_PER_TILE) * GRID_H + i // COLS_PER_TILE
    return jnp.where(j == src, 1.0, 0.0).astype(BF16)


def _mod_kernel(c_ref, w_ref, b_ref, o_ref):
    c = c_ref[...]
    o_ref[0] = _dot_hp(_silu(c), w_ref[0]) + b_ref[0]


def _modulation(cvec, ada_w, ada_b):
    depth = ada_w.shape[0]
    n6 = ada_w.shape[2]
    return pl.pallas_call(
        _mod_kernel,
        grid=(depth, n6 // D),
        in_specs=[pl.BlockSpec((SUBLANES, D), lambda l, j: (0, 0)),
                  pl.BlockSpec((1, D, D), lambda l, j: (l, 0, j)),
                  pl.BlockSpec((1, 1, D), lambda l, j: (l, 0, j))],
        out_specs=pl.BlockSpec((1, SUBLANES, D), lambda l, j: (l, 0, j)),
        out_shape=jax.ShapeDtypeStruct((depth, SUBLANES, n6), F32),
        compiler_params=_params("arbitrary", "arbitrary"),
    )(cvec, ada_w, ada_b.reshape(depth, 1, n6))


def _grid_tile_spec():
    tiles_per_seq = GRID_W // COLS_PER_TILE
    return pl.BlockSpec((1, GRID_H, COLS_PER_TILE, D), lambda i: (i // tiles_per_seq, 0, i % tiles_per_seq, 0))


def _inproj_kernel(x_ref, mod_ref, g_ref, *refs, n_w, has_t, to_col_major):
    w_refs = refs[:n_w]
    o_refs = refs[n_w + has_t:2 * n_w + has_t]
    x = x_ref[0].reshape(TM, D) if to_col_major else x_ref[...]
    y = x * lax.rsqrt(jnp.mean(x * x, axis=-1, keepdims=True) + EPS) * g_ref[...]
    shift = mod_ref[0, :, 0:D]
    scale = mod_ref[0, :, D:2 * D]
    h = (y * (1.0 + scale) + shift).astype(BF16)
    if to_col_major:
        h = _dot(_grid_perm(True), h).astype(BF16)
    for w_ref, o_ref in zip(w_refs, o_refs):
        o_ref[...] = _dot(h, w_ref[...]).astype(o_ref.dtype)
    if has_t:
        wt_ref = refs[n_w]
        ot_ref = refs[2 * n_w + 1]
        ot_ref[...] = _dot_nt(wt_ref[...], h)


def _inproj(st, x, mod_l, gain, weights, out_dtypes, w_t=None, to_col_major=False):
    n_w = len(weights)
    in_specs = [_grid_tile_spec() if to_col_major else pl.BlockSpec((TM, D), lambda i: (i, 0)),
                pl.BlockSpec((1, 1, 6 * D), lambda i: (st.mod_row(i, TM), 0, 0)),
                pl.BlockSpec((1, D), lambda i: (0, 0))]
    in_specs += [pl.BlockSpec(w.shape, lambda i: (0, 0)) for w in weights]
    out_specs = [pl.BlockSpec((TM, w.shape[1]), lambda i: (i, 0)) for w in weights]
    out_shape = [jax.ShapeDtypeStruct((st.n_tok, w.shape[1]), dt) for w, dt in zip(weights, out_dtypes)]
    args = [x, mod_l.reshape(SUBLANES, 1, 6 * D), gain.reshape(1, D)] + list(weights)
    if w_t is not None:
        in_specs.append(pl.BlockSpec(w_t.shape, lambda i: (0, 0)))
        out_specs.append(pl.BlockSpec((w_t.shape[0], TM), lambda i: (0, i)))
        out_shape.append(jax.ShapeDtypeStruct((w_t.shape[0], st.n_tok), F32))
        args.append(w_t)
    return pl.pallas_call(
        functools.partial(_inproj_kernel, n_w=n_w, has_t=int(w_t is not None), to_col_major=to_col_major),
        grid=(st.n_tok // TM,),
        in_specs=in_specs, out_specs=out_specs, out_shape=out_shape,
        compiler_params=_params("arbitrary"),
    )(*args)


def _conv_from_padded(xp_ref, t, cw, cb):
    acc = cb
    for j in range(CONV_W):
        acc = acc + cw[j:j + 1, :] * xp_ref[pl.ds(SUBLANES - CONV_LEFT + j, t), :]
    return acc


def _fill_padded(xp_ref, x, t):
    zeros = jnp.zeros((SUBLANES, xp_ref.shape[1]), F32)
    xp_ref[pl.ds(0, SUBLANES), :] = zeros
    xp_ref[pl.ds(SUBLANES + t, SUBLANES), :] = zeros
    xp_ref[pl.ds(SUBLANES, t), :] = x


def _chunk_cumsum(x, c, rev):
    t = x.shape[0]
    row = lax.broadcasted_iota(jnp.int32, x.shape, 0) % c
    s = 1
    while s < c:
        if rev:
            x = x + jnp.where(row < c - s, pltpu.roll(x, t - s, 0), 0.0)
        else:
            x = x + jnp.where(row >= s, pltpu.roll(x, s, 0), 0.0)
        s *= 2
    return x


_SCAN_ROWS = 32


def _scan_tile(a, b, carry, rev):
    row = lax.broadcasted_iota(jnp.int32, a.shape, 0)
    for s in (1, 2, 4):
        if rev:
            a_s = pltpu.roll(a, SUBLANES - s, 0)
            b_s = pltpu.roll(b, SUBLANES - s, 0)
            m = row < SUBLANES - s
        else:
            a_s = pltpu.roll(a, s, 0)
            b_s = pltpu.roll(b, s, 0)
            m = row >= s
        b = jnp.where(m, a * b_s + b, b)
        a = jnp.where(m, a * a_s, a)
    h = a * carry + b
    return h, (h[0:1] if rev else h[SUBLANES - 1:SUBLANES])


def _lru_kernel(*refs, t, with_h0):
    if with_h0:
        (x_ref, y_ref, cw_ref, cb_ref, wg_ref, bg_ref, lam_ref, h0_ref,
         o_ref, xp, a_f, b_f, a_b, b_b) = refs
    else:
        (x_ref, y_ref, cw_ref, cb_ref, wg_ref, bg_ref, lam_ref,
         o_ref, st_ref, xp, a_f, b_f, a_b, b_b) = refs
    _fill_padded(xp, x_ref[...].astype(F32), t)
    xc = _conv_from_padded(xp, t, cw_ref[...], cb_ref[...])
    gates = _sigmoid(_dot(xc.astype(BF16), wg_ref[0]) + bg_ref[0])
    sp = _softplus(-lam_ref[0])
    for d, (a_ref, b_ref) in enumerate(((a_f, b_f), (a_b, b_b))):
        r = gates[:, (2 * d) * LRU_BLOCK:(2 * d + 1) * LRU_BLOCK]
        ig = gates[:, (2 * d + 1) * LRU_BLOCK:(2 * d + 2) * LRU_BLOCK]
        log_a = -LRU_C * r * sp[d:d + 1, :]
        a = jnp.exp(log_a)
        a_ref[...] = a
        b_ref[...] = jnp.sqrt(1.0 - a * a) * ig * xc

    n_it = t // _SCAN_ROWS
    if with_h0:
        c0 = (h0_ref[0, 0:1, :], h0_ref[0, 1:2, :])
    else:
        c0 = (jnp.zeros((1, LRU_BLOCK), F32), jnp.zeros((1, LRU_BLOCK), F32))

    def body(i, carry):
        cf, cb_ = carry
        base_f = pl.multiple_of(i * _SCAN_ROWS, _SCAN_ROWS)
        base_b = pl.multiple_of((n_it - 1 - i) * _SCAN_ROWS, _SCAN_ROWS)
        for k in range(_SCAN_ROWS // SUBLANES):
            rf = pl.ds(base_f + k * SUBLANES, SUBLANES)
            h, cf = _scan_tile(a_f[rf, :], b_f[rf, :], cf, False)
            b_f[rf, :] = h
            rb = pl.ds(base_b + _SCAN_ROWS - (k + 1) * SUBLANES, SUBLANES)
            h, cb_ = _scan_tile(a_b[rb, :], b_b[rb, :], cb_, True)
            b_b[rb, :] = h
        return cf, cb_

    cf, cb_ = lax.fori_loop(0, n_it, body, c0)
    o_ref[...] = (_gelu_tanh(y_ref[...].astype(F32)) * (b_f[...] + b_b[...])).astype(o_ref.dtype)
    if not with_h0:
        st_ref[0, 0:1, :] = cf
        st_ref[0, 1:2, :] = cb_


def _lru_call(st, xbr, ybr, p, h0=None):
    nb = D // LRU_BLOCK
    t, nseq = st.t, st.nseq
    tok = lambda s, j: (s, j)
    in_specs = [pl.BlockSpec((t, LRU_BLOCK), tok),
                pl.BlockSpec((t, LRU_BLOCK), tok),
                pl.BlockSpec((CONV_W, LRU_BLOCK), lambda s, j: (0, j)),
                pl.BlockSpec((1, LRU_BLOCK), lambda s, j: (0, j)),
                pl.BlockSpec((1, LRU_BLOCK, 4 * LRU_BLOCK), lambda s, j: (j, 0, 0)),
                pl.BlockSpec((1, 1, 4 * LRU_BLOCK), lambda s, j: (j, 0, 0)),
                pl.BlockSpec((1, 2, LRU_BLOCK), lambda s, j: (j, 0, 0))]
    args = [xbr, ybr, p["conv_w"], p["conv_b"], p["wg"], p["bg"], p["lam"]]
    scratch = [pltpu.VMEM((t + 2 * SUBLANES, LRU_BLOCK), F32)] + [pltpu.VMEM((t, LRU_BLOCK), F32)] * 4
    mix_shape = jax.ShapeDtypeStruct((st.n_tok, D), BF16)
    out_mix_spec = pl.BlockSpec((t, LRU_BLOCK), tok)
    if h0 is not None:
        in_specs.append(pl.BlockSpec((1, 2, LRU_BLOCK), lambda s, j: (s, 0, j)))
        args.append(h0)
        return pl.pallas_call(
            functools.partial(_lru_kernel, t=t, with_h0=True),
            grid=(nseq, nb), in_specs=in_specs, out_specs=out_mix_spec, out_shape=mix_shape,
            scratch_shapes=scratch,
            compiler_params=_params("arbitrary", "arbitrary"),
        )(*args)
    return pl.pallas_call(
        functools.partial(_lru_kernel, t=t, with_h0=False),
        grid=(nseq, nb), in_specs=in_specs,
        out_specs=[out_mix_spec, pl.BlockSpec((1, 2, LRU_BLOCK), lambda s, j: (s, 0, j))],
        out_shape=[mix_shape, jax.ShapeDtypeStruct((nseq, 2, D), F32)],
        scratch_shapes=scratch,
        compiler_params=_params("arbitrary", "arbitrary"),
    )(*args)


GDN_GROUP = 4
GDN_SLAB = GDN_GROUP * GDN_CHUNK


def _compact_select(x, c):
    g = x.shape[0] // c
    lane_blk = lax.broadcasted_iota(jnp.int32, (c, g * c), 1) // c
    out = x[0:c]
    for p in range(1, g):
        out = jnp.where(lane_blk == p, x[p * c:(p + 1) * c], out)
    return out


def _block_diag(xc, same_block):
    g = xc.shape[1] // xc.shape[0]
    return jnp.where(same_block, jnp.concatenate([xc] * g, axis=0), jnp.zeros((), xc.dtype))


def _dot_hp_bd(a, bc, same_block):
    ah, al = _split2(a)
    bh, bl = _split2(bc)
    bdh = _block_diag(bh, same_block)
    return _dot(ah, bdh) + _dot(al, bdh) + _dot(ah, _block_diag(bl, same_block))


GDN_INV_BASE = 16
GDN_HEADS_PER_STEP = LANES // GDN_CHUNK


def _tri_inverse_compact(a_cs, eye_c, same_block):
    c = eye_c.shape[0]
    row = lax.broadcasted_iota(jnp.int32, eye_c.shape, 0)
    col = lax.broadcasted_iota(jnp.int32, eye_c.shape, 1) % c
    diag = (row // GDN_INV_BASE) == (col // GDN_INV_BASE)
    p_cs = [jnp.where(diag, -a_c, 0.0) for a_c in a_cs]
    t_cs = [eye_c + p_c for p_c in p_cs]
    p_cs = [_dot_hp_bd(p_c, p_c, same_block) for p_c in p_cs]
    for _ in range(int(math.log2(GDN_INV_BASE)) - 2):
        xs = [_dot_hp_bd(jnp.concatenate([t_c, p_c], axis=0), p_c, same_block) for t_c, p_c in zip(t_cs, p_cs)]
        t_cs = [t_c + x[0:c] for t_c, x in zip(t_cs, xs)]
        p_cs = [x[c:2 * c] for x in xs]
    t_cs = [t_c + _dot_hp_bd(t_c, p_c, same_block) for t_c, p_c in zip(t_cs, p_cs)]
    b = GDN_INV_BASE
    while b < c:
        off_diag = ((row // (2 * b)) == (col // (2 * b))) & ((row // b) != (col // b))
        xs = [_dot_hp_bd(jnp.where(off_diag, a_c, 0.0), t_c, same_block) for a_c, t_c in zip(a_cs, t_cs)]
        t_cs = [t_c - _dot_hp_bd(t_c, x, same_block) for t_c, x in zip(t_cs, xs)]
        b *= 2
    return t_cs


def _gdn_scores(q, k, gcb, gcrow, beta, masks):
    c = GDN_CHUNK
    same_block, eye_c, incl_c, strict_c = masks
    kb = [k * beta[d] for d in range(2)]
    prod = _dot_nt(jnp.concatenate([kb[0], kb[1], q], axis=0).astype(BF16), k.astype(BF16))
    n = q.shape[0]
    qk_c = _compact_select(prod[2 * n:3 * n], c)
    low_half = lax.broadcasted_iota(jnp.int32, (c, LANES), 1) < c
    a_cs, qk_cs = [], []
    for d in range(2):
        gi_c = jnp.concatenate(
            [jnp.where(low_half, gcb[d][(2 * h) * c:(2 * h + 1) * c], gcb[d][(2 * h + 1) * c:(2 * h + 2) * c])
             for h in range(GDN_GROUP // 2)], axis=1)
        diff = gi_c - gcrow[d]
        decay = jnp.where(incl_c[d], jnp.exp(jnp.where(incl_c[d], diff, 0.0)), 0.0)
        a_cs.append(jnp.where(strict_c[d], _compact_select(prod[d * n:(d + 1) * n], c) * decay, 0.0))
        qk_cs.append(qk_c * decay)
    return a_cs, qk_cs


def _gdn_wy(q, k, v, gcb, beta, gl, t_c, same_block):
    eg = jnp.exp(gcb)
    rhs = jnp.concatenate([v * beta, k * beta * eg], axis=1).astype(BF16)
    uw = _dot(_block_diag(t_c.astype(BF16), same_block), rhs)
    k_dec_t = (k * jnp.exp(gl - gcb)).T
    return uw[:, 0:GDN_DV], uw[:, GDN_DV:].astype(BF16), (q * eg).astype(BF16), k_dec_t


def _pair_lanes(blocks, p, c):
    lane = None
    out = None
    for h, x in enumerate(blocks):
        half = x[:, (p // 2) * LANES:(p // 2 + 1) * LANES]
        if p % 2 != h:
            half = pltpu.roll(half, c, 1)
        if out is None:
            lane = lax.broadcasted_iota(jnp.int32, half.shape, 1)
            out = half
        else:
            out = jnp.where(lane >= h * c, half, out)
    return out


def _gdn_kernel(*refs, t, with_s0):
    if with_s0:
        (q_ref, k_ref, v_ref, z_ref, ab_ref, abt_ref, cwq_ref, cwk_ref, cwv_ref, cbq_ref, cbk_ref, cbv_ref,
         alog_ref, dtb_ref, alogc_ref, dtbc_ref, ng_ref, s0_ref,
         o_ref, xp, qs, ks, vs, osum, gl_s, gcrow_s, u_s, wq_s, qkkd_s) = refs
    else:
        (q_ref, k_ref, v_ref, z_ref, ab_ref, abt_ref, cwq_ref, cwk_ref, cwv_ref, cbq_ref, cbk_ref, cbv_ref,
         alog_ref, dtb_ref, alogc_ref, dtbc_ref, ng_ref,
         o_ref, st_ref, xp, qs, ks, vs, osum, gl_s, gcrow_s, u_s, wq_s, qkkd_s) = refs
    c = GDN_CHUNK
    n_chunks = t // c
    hp = GDN_HEADS_PER_STEP
    heads = [pl.program_id(1) * hp + hh for hh in range(hp)]
    head_lanes = [pl.ds(hh * LANES, LANES) for hh in range(hp)]

    for hh in range(hp):
        for x_ref, cw_ref, cb_ref, dst, kind in ((q_ref, cwq_ref, cbq_ref, qs, "q"),
                                                 (k_ref, cwk_ref, cbk_ref, ks, "k"),
                                                 (v_ref, cwv_ref, cbv_ref, vs, "v")):
            _fill_padded(xp, x_ref[:, head_lanes[hh]].astype(F32), t)
            y = _silu(_conv_from_padded(xp, t, cw_ref[:, head_lanes[hh]], cb_ref[:, head_lanes[hh]]))
            if kind != "v":
                y = y * lax.rsqrt(jnp.sum(y * y, axis=-1, keepdims=True) + EPS)
            if kind == "q":
                y = y * (GDN_DK ** -0.5)
            dst[hh] = y

    abt_rows = lambda idx: abt_ref[idx, 0]
    slab = GDN_SLAB
    bi = lax.broadcasted_iota(jnp.int32, (slab, slab), 0)
    bj = lax.broadcasted_iota(jnp.int32, (slab, slab), 1)
    same_chunk = (bi // c) == (bj // c)
    tri = [jnp.where(same_chunk & ((bi <= bj) if d == 1 else (bi >= bj)), 1.0, 0.0).astype(BF16) for d in range(2)]
    for hh, head in enumerate(heads):
        for d in range(2):
            a_row = abt_rows(d * GDN_H + head)
            al = alogc_ref[pl.ds(d * GDN_H + head, 1), :]
            db = dtbc_ref[pl.ds(d * GDN_H + head, 1), :]
            g_rows = -jnp.exp(al) * _softplus(a_row + db)
            gcrow_s[hh, d] = _dot_sel_right(g_rows, tri[1 - d])

    osum[...] = jnp.zeros_like(osum)
    ci = lax.broadcasted_iota(jnp.int32, (c, slab), 0)
    cj = lax.broadcasted_iota(jnp.int32, (c, slab), 1) % c
    masks = (same_chunk, jnp.where(ci == cj, 1.0, 0.0), (ci >= cj, ci <= cj), (ci > cj, ci < cj))
    if with_s0:
        s_init = tuple(jnp.concatenate([s0_ref[0, d, hh] for hh in range(hp)], axis=1) for d in range(2))
    else:
        s_init = (jnp.zeros((GDN_DK, hp * GDN_DV), F32), jnp.zeros((GDN_DK, hp * GDN_DV), F32))

    lane = lax.broadcasted_iota(jnp.int32, (slab, LANES), 1)

    def prepare(i, _):
        base = pl.multiple_of(i * slab, slab)
        rows = pl.ds(base, slab)
        ab = ab_ref[rows, :]
        gall = jnp.where(lane < 2 * GDN_H, -jnp.exp(alog_ref[...]) * _softplus(ab + dtb_ref[...]), _sigmoid(ab))
        pick = lambda idx: jnp.broadcast_to(jnp.sum(jnp.where(lane == idx, gall, 0.0), axis=-1, keepdims=True),
                                            (slab, LANES))
        beta = [[pick(2 * GDN_H + d * GDN_H + head) for d in range(2)] for head in heads]
        gcb = [[_chunk_cumsum(pick(d * GDN_H + head), c, d == 1) for d in range(2)] for head in heads]
        a_cs, qk_cs = [], []
        for hh in range(hp):
            a_hh, qk_hh = _gdn_scores(qs[hh, rows, :], ks[hh, rows, :], gcb[hh],
                                      [gcrow_s[hh, d, pl.ds(i, 1), :] for d in range(2)], beta[hh], masks)
            a_cs += a_hh
            qk_cs.append(qk_hh)
        t_cs = _tri_inverse_compact(a_cs, masks[1], same_chunk)
        for d in range(2):
            k_dec_ts = []
            for hh in range(hp):
                totals = [gcb[hh][d][p * c + (0 if d == 1 else c - 1):p * c + (0 if d == 1 else c - 1) + 1]
                          for p in range(GDN_GROUP)]
                for p in range(GDN_GROUP):
                    gl_s[hh, d, pl.ds(i * GDN_GROUP + p, 1), :] = totals[p]
                gl = jnp.concatenate([jnp.broadcast_to(row, (c, LANES)) for row in totals], axis=0)
                u, w, qg, k_dec_t = _gdn_wy(qs[hh, rows, :], ks[hh, rows, :], vs[hh, rows, :], gcb[hh][d],
                                            beta[hh][d], gl, t_cs[hh * 2 + d], same_chunk)
                k_dec_ts.append(k_dec_t)
                u_s[d, rows, head_lanes[hh]] = u
                for p in range(GDN_GROUP):
                    dst = pl.multiple_of(base * 2 + p * 2 * c, 2 * c)
                    wq_s[d, pl.ds(dst, c), head_lanes[hh]] = w[p * c:(p + 1) * c]
                    wq_s[d, pl.ds(dst + c, c), head_lanes[hh]] = qg[p * c:(p + 1) * c]
            for p in range(GDN_GROUP):
                n = i * GDN_GROUP + p
                qkkd_s[d, n, pl.ds(0, c), :] = _pair_lanes([qk_cs[hh][d] for hh in range(hp)], p, c).astype(BF16)
                qkkd_s[d, n, pl.ds(c, GDN_DK), :] = _pair_lanes(k_dec_ts, p, c).astype(BF16)
        return 0

    lax.fori_loop(0, t // slab, prepare, 0)

    sr = lax.broadcasted_iota(jnp.int32, (hp * GDN_DK, hp * GDN_DV), 0) // GDN_DK
    sc = lax.broadcasted_iota(jnp.int32, (hp * GDN_DK, hp * GDN_DV), 1) // GDN_DV
    vr = lax.broadcasted_iota(jnp.int32, (hp * c, hp * GDN_DV), 0) // c
    vc = lax.broadcasted_iota(jnp.int32, (hp * c, hp * GDN_DV), 1) // GDN_DV

    def body(i, carry):
        out = []
        for d in range(2):
            n = (n_chunks - 1 - i) if d == 1 else i
            base = pl.multiple_of(n * c, c)
            rows = pl.ds(base, c)
            s = carry[d]
            s_bd = jnp.where(sr == sc, jnp.concatenate([s] * hp, axis=0), 0.0).astype(BF16)
            ws_qs = _dot(wq_s[d, pl.ds(pl.multiple_of(n * 2 * c, 2 * c), 2 * c), :], s_bd)
            v_new = u_s[d, rows, :] - ws_qs[0:c]
            v_bd = jnp.where(vr == vc, jnp.concatenate([v_new] * hp, axis=0), 0.0).astype(BF16)
            upd = _dot(qkkd_s[d, n], v_bd)
            gl = jnp.concatenate([gl_s[hh, d, pl.ds(n, 1), :] for hh in range(hp)], axis=1)
            out.append(s * jnp.exp(gl) + upd[c:c + GDN_DK])
            osum[rows, :] = osum[rows, :] + ws_qs[c:2 * c] + upd[0:c]
        return tuple(out)

    s_fin = lax.fori_loop(0, n_chunks, body, s_init)
    for hh in range(hp):
        o = osum[:, head_lanes[hh]]
        y = o * lax.rsqrt(jnp.mean(o * o, axis=-1, keepdims=True) + EPS) * ng_ref[...]
        o_ref[:, head_lanes[hh]] = (y * _silu(z_ref[:, head_lanes[hh]].astype(F32))).astype(o_ref.dtype)
        if not with_s0:
            for d in range(2):
                st_ref[0, d, hh] = s_fin[d][:, hh * GDN_DV:(hh + 1) * GDN_DV]


def _dot_sel_right(x, sel_bf16):
    hi, mid, lo = _split3(x)
    return _dot(hi, sel_bf16) + _dot(mid, sel_bf16) + _dot(lo, sel_bf16)


def _gdn_call(st, qkv, z, ab, abt, p, s0=None):
    c = GDN_SLAB
    t, nseq = st.t, st.nseq
    n_chunks = t // c
    n_pad = max(n_chunks, SUBLANES)
    abt = abt.reshape(4 * GDN_H, nseq, n_chunks, c)
    if n_pad != n_chunks:
        abt = jnp.pad(abt, ((0, 0), (0, 0), (0, n_pad - n_chunks), (0, 0)))
    hp = GDN_HEADS_PER_STEP
    wide = hp * LANES
    nqk = GDN_H // hp
    in_specs = [pl.BlockSpec((t, wide), lambda s, h: (s, h)),
                pl.BlockSpec((t, wide), lambda s, h: (s, nqk + h)),
                pl.BlockSpec((t, wide), lambda s, h: (s, 2 * nqk + h)),
                pl.BlockSpec((t, wide), lambda s, h: (s, h)),
                pl.BlockSpec((t, LANES), lambda s, h: (s, 0)),
                pl.BlockSpec((4 * GDN_H, 1, n_pad, c), lambda s, h: (0, s, 0, 0)),
                pl.BlockSpec((CONV_W, wide), lambda s, h: (0, h)),
                pl.BlockSpec((CONV_W, wide), lambda s, h: (0, nqk + h)),
                pl.BlockSpec((CONV_W, wide), lambda s, h: (0, 2 * nqk + h)),
                pl.BlockSpec((1, wide), lambda s, h: (0, h)),
                pl.BlockSpec((1, wide), lambda s, h: (0, nqk + h)),
                pl.BlockSpec((1, wide), lambda s, h: (0, 2 * nqk + h)),
                pl.BlockSpec((1, LANES), lambda s, h: (0, 0)),
                pl.BlockSpec((1, LANES), lambda s, h: (0, 0)),
                pl.BlockSpec((4 * GDN_H, 1), lambda s, h: (0, 0)),
                pl.BlockSpec((4 * GDN_H, 1), lambda s, h: (0, 0)),
                pl.BlockSpec((1, LANES), lambda s, h: (0, 0))]
    args = [qkv, qkv, qkv, z, ab, abt, p["conv_w"], p["conv_w"], p["conv_w"], p["conv_b"], p["conv_b"], p["conv_b"],
            p["alog_row"], p["dtb_row"], p["alog_col"], p["dtb_col"], p["norm_g"]]
    scratch = [pltpu.VMEM((t + 2 * SUBLANES, LANES), F32)] + [pltpu.VMEM((hp, t, LANES), F32)] * 3 + [
        pltpu.VMEM((t, wide), F32),
        pltpu.VMEM((hp, 2, max(t // GDN_CHUNK, SUBLANES), LANES), F32), pltpu.VMEM((hp, 2, n_pad, c), F32),
        pltpu.VMEM((2, t, wide), F32), pltpu.VMEM((2, 2 * t, wide), BF16),
        pltpu.VMEM((2, t // GDN_CHUNK, GDN_CHUNK + GDN_DK, hp * GDN_CHUNK), BF16)]
    mix_shape = jax.ShapeDtypeStruct((st.n_tok, GDN_H * GDN_DV), BF16)
    out_mix_spec = pl.BlockSpec((t, wide), lambda s, h: (s, h))
    st_spec = pl.BlockSpec((1, 2, hp, GDN_DK, GDN_DV), lambda s, h: (s, 0, h, 0, 0))
    if s0 is not None:
        in_specs.append(st_spec)
        args.append(s0)
        return pl.pallas_call(
            functools.partial(_gdn_kernel, t=t, with_s0=True),
            grid=(nseq, nqk), in_specs=in_specs, out_specs=out_mix_spec, out_shape=mix_shape,
            scratch_shapes=scratch,
            compiler_params=_params("arbitrary", "arbitrary"),
        )(*args)
    return pl.pallas_call(
        functools.partial(_gdn_kernel, t=t, with_s0=False),
        grid=(nseq, nqk), in_specs=in_specs,
        out_specs=[out_mix_spec, st_spec],
        out_shape=[mix_shape, jax.ShapeDtypeStruct((nseq, 2, GDN_H, GDN_DK, GDN_DV), F32)],
        scratch_shapes=scratch,
        compiler_params=_params("arbitrary", "arbitrary"),
    )(*args)


def _gla_kernel(*refs, t, with_s0):
    if with_s0:
        (q_ref, k_ref, v_ref, z_ref, lr_ref, w2_ref, b2_ref, ng_ref, s0_ref,
         o_ref, osum, gc_s) = refs
    else:
        (q_ref, k_ref, v_ref, z_ref, lr_ref, w2_ref, b2_ref, ng_ref,
         o_ref, st_ref, osum, gc_s) = refs
    c = GLA_CHUNK
    n_chunks = t // c
    slab = 256
    bi = lax.broadcasted_iota(jnp.int32, (slab, slab), 0)
    bj = lax.broadcasted_iota(jnp.int32, (slab, slab), 1)
    same_chunk = (bi // c) == (bj // c)
    lr = lr_ref[...]
    for d in range(2):
        pre = _dot_hp(lr, w2_ref[d, 0]) + b2_ref[d]
        glog = (jnp.minimum(pre, 0.0) - jnp.log1p(jnp.exp(-jnp.abs(pre)))) * (1.0 / GLA_TAU)
        gc_s[d] = _chunk_cumsum(glog, c, d == 1)

    osum[...] = jnp.zeros_like(osum)
    n_slabs = t // slab
    per_slab = slab // c
    causal = [same_chunk & ((bi <= bj) if d == 1 else (bi >= bj)) for d in range(2)]
    own_block = (lax.broadcasted_iota(jnp.int32, (slab, per_slab * GLA_DK), 0) // c
                 == lax.broadcasted_iota(jnp.int32, (slab, per_slab * GLA_DK), 1) // GLA_DK)
    if with_s0:
        s_init = (s0_ref[0, 0, 0].T, s0_ref[0, 1, 0].T)
    else:
        s_init = (jnp.zeros((GLA_DV, GLA_DK), F32), jnp.zeros((GLA_DV, GLA_DK), F32))

    def spread(x):
        return jnp.where(own_block, jnp.concatenate([x] * per_slab, axis=1), jnp.zeros((), x.dtype))

    def chunk_row(d, row):
        return gc_s[d, pl.ds(row, 1), :]

    def per_chunk_rows(d, base, offset):
        return jnp.concatenate([jnp.broadcast_to(chunk_row(d, base + p * c + offset), (c, GLA_DK))
                                for p in range(per_slab)], axis=0)

    def body(i, carry):
        stage = []
        for d in range(2):
            rev = d == 1
            base = pl.multiple_of(((n_slabs - 1 - i) if rev else i) * slab, slab)
            rows = pl.ds(base, slab)
            q = q_ref[rows, :].astype(F32) * (GLA_DK ** -0.5)
            k = k_ref[rows, :].astype(F32)
            v = v_ref[rows, :]
            gc = gc_s[d, rows, :]
            gm = per_chunk_rows(d, base, c // 2)
            gl = per_chunk_rows(d, base, 0 if rev else c - 1)
            scores = _dot_nt((q * jnp.exp(gc - gm)).astype(BF16), (k * jnp.exp(gm - gc)).astype(BF16))
            a_mat = jnp.where(causal[d], scores, 0.0).astype(BF16)
            ds_cat = _dot_tn(v, spread((k * jnp.exp(gl - gc)).astype(BF16)))
            stage.append((base, rows, v, a_mat, ds_cat, spread((q * jnp.exp(gc)).astype(BF16))))
        out = []
        for d in range(2):
            rev = d == 1
            base, rows, v, a_mat, ds_cat, qg_spread = stage[d]
            st = carry[d]
            prev = [None] * per_slab
            for p in (range(per_slab - 1, -1, -1) if rev else range(per_slab)):
                prev[p] = st.astype(BF16)
                gl_p = chunk_row(d, base + p * c + (0 if rev else c - 1))
                st = st * jnp.exp(gl_p) + ds_cat[:, p * GLA_DK:(p + 1) * GLA_DK]
            o = _dot(a_mat, v) + _dot_nt(qg_spread, jnp.concatenate(prev, axis=1))
            osum[rows, :] = osum[rows, :] + o
            out.append(st)
        return tuple(out)

    s_f, s_b = lax.fori_loop(0, n_slabs, body, s_init)
    o = osum[...]
    y = o * lax.rsqrt(jnp.mean(o * o, axis=-1, keepdims=True) + EPS) * ng_ref[...]
    o_ref[...] = (y * _silu(z_ref[...].astype(F32))).astype(o_ref.dtype)
    if not with_s0:
        st_ref[0, 0, 0] = s_f.T
        st_ref[0, 1, 0] = s_b.T


def _gla_call(st, q, k, v, z, lr, p, s0=None):
    t, nseq = st.t, st.nseq
    in_specs = [pl.BlockSpec((t, GLA_DK), lambda s, h: (s, h)),
                pl.BlockSpec((t, GLA_DK), lambda s, h: (s, h)),
                pl.BlockSpec((t, GLA_DV), lambda s, h: (s, h)),
                pl.BlockSpec((t, GLA_DV), lambda s, h: (s, h)),
                pl.BlockSpec((t, LANES), lambda s, h: (s, 0)),
                pl.BlockSpec((2, 1, LANES, GLA_DK), lambda s, h: (0, h, 0, 0)),
                pl.BlockSpec((2, 1, GLA_DK), lambda s, h: (0, 0, h)),
                pl.BlockSpec((1, GLA_DV), lambda s, h: (0, 0))]
    args = [q, k, v, z, lr, p["w2"], p["b2"], p["norm_g"]]
    scratch = [pltpu.VMEM((t, GLA_DV), F32), pltpu.VMEM((2, t, GLA_DK), F32)]
    mix_shape = jax.ShapeDtypeStruct((st.n_tok, GLA_H * GLA_DV), BF16)
    out_mix_spec = pl.BlockSpec((t, GLA_DV), lambda s, h: (s, h))
    st_spec = pl.BlockSpec((1, 2, 1, GLA_DK, GLA_DV), lambda s, h: (s, 0, h, 0, 0))
    if s0 is not None:
        in_specs.append(st_spec)
        args.append(s0)
        return pl.pallas_call(
            functools.partial(_gla_kernel, t=t, with_s0=True),
            grid=(nseq, GLA_H), in_specs=in_specs, out_specs=out_mix_spec, out_shape=mix_shape,
            scratch_shapes=scratch,
            compiler_params=_params("arbitrary", "arbitrary"),
        )(*args)
    return pl.pallas_call(
        functools.partial(_gla_kernel, t=t, with_s0=False),
        grid=(nseq, GLA_H), in_specs=in_specs,
        out_specs=[out_mix_spec, st_spec],
        out_shape=[mix_shape, jax.ShapeDtypeStruct((nseq, 2, GLA_H, GLA_DK, GLA_DV), F32)],
        scratch_shapes=scratch,
        compiler_params=_params("arbitrary", "arbitrary"),
    )(*args)


def _route(sel, s):
    scores = []
    for g in range(N_GROUPS):
        m = sel[g * GROUP_SZ:(g + 1) * GROUP_SZ]
        best = None
        for a in range(GROUP_SZ):
            for b in range(a + 1, GROUP_SZ):
                pair = m[a] + m[b]
                best = pair if best is None else jnp.maximum(best, pair)
        scores.append(best)
    gbest = jnp.zeros_like(scores[0], dtype=jnp.int32)
    top = scores[0]
    for g in range(1, N_GROUPS):
        better = scores[g] > top
        gbest = jnp.where(better, g, gbest)
        top = jnp.where(better, scores[g], top)
    picked = []
    for e in range(N_EXPERTS):
        g = e // GROUP_SZ
        rank = jnp.zeros_like(gbest)
        for m in range(g * GROUP_SZ, (g + 1) * GROUP_SZ):
            if m == e:
                continue
            ahead = (sel[m] > sel[e]) | (sel[m] == sel[e]) if m < e else (sel[m] > sel[e])
            rank = rank + ahead.astype(jnp.int32)
        picked.append(jnp.where((gbest == g) & (rank < 2), s[e], 0.0))
    total = picked[0]
    for e in range(1, N_EXPERTS):
        total = total + picked[e]
    return [p / total for p in picked]


def _outproj_kernel(*refs, n_mix, from_col_major):
    mix_refs = refs[:n_mix]
    w_refs = refs[n_mix:2 * n_mix]
    x_ref, mod_ref, g_ref, rwt_ref, rb_ref, x1_ref, h2_ref, gates_ref = refs[2 * n_mix:]
    gate1 = mod_ref[0, :, 2 * D:3 * D]
    delta = None
    for mix_ref, w_ref in zip(mix_refs, w_refs):
        mix = mix_ref[...]
        if from_col_major:
            mix = _dot(_grid_perm(False), mix).astype(BF16)
        part = _dot(mix, w_ref[...])
        delta = part if delta is None else delta + part
    x = x_ref[0].reshape(TM, D) if from_col_major else x_ref[...]
    x1 = x + gate1 * delta
    x1_ref[...] = x1
    y = x1 * lax.rsqrt(jnp.mean(x1 * x1, axis=-1, keepdims=True) + EPS) * g_ref[...]
    h2 = y * (1.0 + mod_ref[0, :, 4 * D:5 * D]) + mod_ref[0, :, 3 * D:4 * D]
    h2_ref[...] = h2.astype(BF16)
    rw_hi, rw_lo = _split2(rwt_ref[...])
    h2_hi, h2_lo = _split2(h2)
    both = _dot_nt(jnp.concatenate([rw_hi, rw_lo], axis=0), h2_hi)
    logits = both[0:N_EXPERTS] + both[N_EXPERTS:2 * N_EXPERTS] + _dot_nt(rw_hi, h2_lo)
    s_all = _sigmoid(logits)
    sel_all = s_all + rb_ref[...]
    s = [s_all[e:e + 1, :] for e in range(N_EXPERTS)]
    sel = [sel_all[e:e + 1, :] for e in range(N_EXPERTS)]
    gate_rows = _route(sel, s) + [jnp.zeros((LANES - N_EXPERTS, TM), F32)]
    gates_ref[...] = jnp.concatenate(gate_rows, axis=0).T


def _outproj(st, mixes, w_outs, x, mod_l, gain2, router_wt, router_b, from_col_major=False):
    n_mix = len(mixes)
    in_specs = [pl.BlockSpec((TM, m.shape[1]), lambda i: (i, 0)) for m in mixes]
    in_specs += [pl.BlockSpec(w.shape, lambda i: (0, 0)) for w in w_outs]
    in_specs += [_grid_tile_spec() if from_col_major else pl.BlockSpec((TM, D), lambda i: (i, 0)),
                 pl.BlockSpec((1, 1, 6 * D), lambda i: (st.mod_row(i, TM), 0, 0)),
                 pl.BlockSpec((1, D), lambda i: (0, 0)),
                 pl.BlockSpec((N_EXPERTS, D), lambda i: (0, 0)),
                 pl.BlockSpec((N_EXPERTS, 1), lambda i: (0, 0))]
    return pl.pallas_call(
        functools.partial(_outproj_kernel, n_mix=n_mix, from_col_major=from_col_major),
        grid=(st.n_tok // TM,),
        in_specs=in_specs,
        out_specs=[pl.BlockSpec((TM, D), lambda i: (i, 0)),
                   pl.BlockSpec((TM, D), lambda i: (i, 0)),
                   pl.BlockSpec((TM, LANES), lambda i: (i, 0))],
        out_shape=[jax.ShapeDtypeStruct((st.n_tok, D), F32),
                   jax.ShapeDtypeStruct((st.n_tok, D), BF16),
                   jax.ShapeDtypeStruct((st.n_tok, LANES), F32)],
        compiler_params=_params("arbitrary"),
    )(*mixes, *w_outs, x, mod_l.reshape(SUBLANES, 1, 6 * D), gain2.reshape(1, D), router_wt,
      router_b.reshape(N_EXPERTS, 1))


def _moe_kernel(h_ref, gates_ref, x1_ref, mod_ref, wg_ref, wu_ref, wd_ref, fg_ref, o_ref, acc, *, final_norm,
                to_grid):
    e = pl.program_id(1)

    @pl.when(e == 0)
    def _():
        acc[...] = jnp.zeros_like(acc)

    h = h_ref[...]
    gates = gates_ref[...]
    lane = lax.broadcasted_iota(jnp.int32, gates.shape, 1)
    total = None
    for k in range(MOE_EXPERTS_PER_STEP):
        gate = jnp.sum(jnp.where(lane == e * MOE_EXPERTS_PER_STEP + k, gates, 0.0), axis=-1, keepdims=True)
        hid = _silu(_dot(h, wg_ref[0, k].astype(BF16))) * _dot(h, wu_ref[0, k].astype(BF16)) * gate
        part = _dot(hid.astype(BF16), wd_ref[0, k].astype(BF16))
        total = part if total is None else total + part
    acc[...] += total

    @pl.when(e == N_EXPERTS // MOE_EXPERTS_PER_STEP - 1)
    def _():
        out = x1_ref[...] + mod_ref[0, :, 5 * D:6 * D] * acc[...]
        if final_norm:
            out = out * lax.rsqrt(jnp.mean(out * out, axis=-1, keepdims=True) + EPS) * fg_ref[...]
        if to_grid:
            for j in range(TM_MOE // TM):
                o_ref[0, :, j * COLS_PER_TILE:(j + 1) * COLS_PER_TILE, :] = (
                    out[j * TM:(j + 1) * TM].reshape(GRID_H, COLS_PER_TILE, D))
        else:
            o_ref[...] = out


def _moe(st, h2, gates, x1, mod_l, w_gate, w_up, w_down, layer, final_gain, final_norm, to_grid=False):
    tm = TM_MOE
    if to_grid:
        tiles_per_seq = st.t // tm
        cols = tm // GRID_H
        out_spec = pl.BlockSpec((1, GRID_H, cols, D), lambda i, e: (i // tiles_per_seq, 0, i % tiles_per_seq, 0))
        out_shape = jax.ShapeDtypeStruct((st.nseq, GRID_H, GRID_W, D), F32)
    else:
        out_spec = pl.BlockSpec((tm, D), lambda i, e: (i, 0))
        out_shape = jax.ShapeDtypeStruct((st.n_tok, D), F32)
    return pl.pallas_call(
        functools.partial(_moe_kernel, final_norm=final_norm, to_grid=to_grid),
        grid=(st.n_tok // tm, N_EXPERTS // MOE_EXPERTS_PER_STEP),
        in_specs=[pl.BlockSpec((tm, D), lambda i, e: (i, 0)),
                  pl.BlockSpec((tm, LANES), lambda i, e: (i, 0)),
                  pl.BlockSpec((tm, D), lambda i, e: (i, 0)),
                  pl.BlockSpec((1, 1, 6 * D), lambda i, e: (st.mod_row(i, tm), 0, 0)),
                  pl.BlockSpec((1, MOE_EXPERTS_PER_STEP, D, D_EXPERT), lambda i, e: (layer, e, 0, 0)),
                  pl.BlockSpec((1, MOE_EXPERTS_PER_STEP, D, D_EXPERT), lambda i, e: (layer, e, 0, 0)),
                  pl.BlockSpec((1, MOE_EXPERTS_PER_STEP, D_EXPERT, D), lambda i, e: (layer, e, 0, 0)),
                  pl.BlockSpec((1, D), lambda i, e: (0, 0))],
        out_specs=out_spec, out_shape=out_shape,
        scratch_shapes=[pltpu.VMEM((tm, D), F32)],
        compiler_params=_params("arbitrary", "arbitrary"),
    )(h2, gates, x1, mod_l.reshape(SUBLANES, 1, 6 * D), w_gate, w_up, w_down, final_gain.reshape(1, D))


def _layer_ab(xs, mod_l, state_lru, state_gdn, norm1_g, w_in, lru_conv_w, lru_conv_b, lru_wa, lru_ba, lru_wx,
              lru_bx, lru_lam, gdn_conv_w, gdn_conv_b, gdn_a_log, gdn_dt_bias, gdn_norm_g):
    w = w_in.astype(BF16)
    n_qkv = 2 * GDN_H * GDN_DK + GDN_H * GDN_DV
    o = 0
    w_y, o = w[:, o:o + D], o + D
    w_x, o = w[:, o:o + D], o + D
    w_qkv, o = w[:, o:o + n_qkv], o + n_qkv
    w_z, o = w[:, o:o + GDN_H * GDN_DV], o + GDN_H * GDN_DV
    w_ab = w[:, o:]
    w_ab_pad = jnp.pad(w_ab, ((0, 0), (0, LANES - w_ab.shape[1])))
    proj = [_inproj(st, x, mod_l, norm1_g, [w_y, w_x, w_qkv, w_z, w_ab_pad], [BF16, BF16, BF16, BF16, F32],
                    w_t=w_ab.T) for st, x in zip((PROMPT, SAMPLE), xs)]

    nb = D // LRU_BLOCK
    wg = jnp.stack([lru_wa[0], lru_wx[0], lru_wa[1], lru_wx[1]], axis=1)
    wg = wg.transpose(0, 2, 1, 3).reshape(nb, LRU_BLOCK, 4 * LRU_BLOCK).astype(BF16)
    bg = jnp.stack([lru_ba[0], lru_bx[0], lru_ba[1], lru_bx[1]], axis=0)
    bg = bg.reshape(4, nb, LRU_BLOCK).transpose(1, 0, 2).reshape(nb, 1, 4 * LRU_BLOCK)
    lam = lru_lam.reshape(2, nb, LRU_BLOCK).transpose(1, 0, 2)
    lru_p = dict(conv_w=lru_conv_w, conv_b=lru_conv_b.reshape(1, D), wg=wg, bg=bg, lam=lam)
    (ybr_p, xbr_p, qkv_p, z_p, ab_p, abt_p), (ybr_s, xbr_s, qkv_s, z_s, ab_s, abt_s) = proj
    lru_p_out, lru_state = _lru_call(PROMPT, xbr_p, ybr_p, lru_p)
    lru_s_out = _lru_call(SAMPLE, xbr_s, ybr_s, lru_p, h0=state_lru)

    pad16 = lambda v: jnp.pad(v.reshape(1, 2 * GDN_H), ((0, 0), (0, LANES - 2 * GDN_H)))
    col32 = lambda v: jnp.pad(v.reshape(2 * GDN_H, 1), ((0, 2 * GDN_H), (0, 0)))
    gdn_p = dict(conv_w=gdn_conv_w, conv_b=gdn_conv_b.reshape(1, n_qkv), alog_row=pad16(gdn_a_log),
                 dtb_row=pad16(gdn_dt_bias), alog_col=col32(gdn_a_log), dtb_col=col32(gdn_dt_bias),
                 norm_g=gdn_norm_g.reshape(1, GDN_DV))
    gdn_p_out, gdn_state = _gdn_call(PROMPT, qkv_p, z_p, ab_p, abt_p, gdn_p)
    gdn_s_out = _gdn_call(SAMPLE, qkv_s, z_s, ab_s, abt_s, gdn_p, s0=state_gdn)
    return [lru_p_out, gdn_p_out], [lru_s_out, gdn_s_out], lru_state, gdn_state


def _layer_c(xs, mod_l, state_gla, norm1_g, w_in, w2, b2, norm_g):
    w = w_in.astype(BF16)
    nk = GLA_H * GLA_DK
    nv = GLA_H * GLA_DV
    w_q, w_k, w_v, w_z = w[:, 0:nk], w[:, nk:2 * nk], w[:, 2 * nk:2 * nk + nv], w[:, 2 * nk + nv:2 * nk + 2 * nv]
    w_lr = jnp.pad(w[:, 2 * nk + 2 * nv:], ((0, 0), (0, LANES - 2 * GLA_RANK)))
    weights, dtypes = [w_q, w_k, w_v, w_z, w_lr], [BF16, BF16, BF16, BF16, F32]
    proj_p = _inproj(PROMPT, xs[0], mod_l, norm1_g, weights, dtypes)
    proj_s = _inproj(SAMPLE, xs[1], mod_l, norm1_g, weights, dtypes, to_col_major=True)
    w2h = w2.reshape(2, GLA_RANK, GLA_H, GLA_DK).transpose(0, 2, 1, 3)
    w2big = jnp.zeros((2, GLA_H, LANES, GLA_DK), F32)
    w2big = w2big.at[0, :, 0:GLA_RANK].set(w2h[0]).at[1, :, GLA_RANK:2 * GLA_RANK].set(w2h[1])
    gla_p = dict(w2=w2big, b2=b2.reshape(2, 1, nk), norm_g=norm_g.reshape(1, GLA_DV))
    gla_p_out, gla_state = _gla_call(PROMPT, *proj_p, gla_p)
    gla_s_out = _gla_call(SAMPLE, *proj_s, gla_p, s0=state_gla)
    return gla_p_out, gla_s_out, gla_state


def kernel(x_prompt, x_sample, state_lru, state_gdn, state_gla, c, c_ctx, ada_w, ada_b, norm1_g, norm2_g,
           final_norm_g, ab_w_in, lru_conv_w, lru_conv_b, lru_wa, lru_ba, lru_wx, lru_bx, lru_lam, gdn_conv_w,
           gdn_conv_b, gdn_a_log, gdn_dt_bias, gdn_norm_g, ab_w_out, gla_w_in, gla_w2, gla_b2, gla_norm_g,
           gla_w_out, router_w, router_b, moe_w_gate, moe_w_up, moe_w_down):
    xp = x_prompt.reshape(N_PROMPT, D)
    xs = x_sample.reshape(N_SAMPLE, D)
    cvec = jnp.concatenate([c_ctx.reshape(1, D), c, jnp.zeros((SUBLANES - 1 - N_SAMPLE_SEQ, D), F32)], axis=0)
    mod = _modulation(cvec, ada_w, ada_b)
    router_wt = router_w.T
    moe_w = (moe_w_gate, moe_w_up, moe_w_down)

    mix_p, mix_s, lru_state, gdn_state = _layer_ab(
        (xp, xs), mod[0], state_lru[:, 0], state_gdn[:, 0], norm1_g[0], ab_w_in[0], lru_conv_w[0], lru_conv_b[0],
        lru_wa[0], lru_ba[0], lru_wx[0], lru_bx[0], lru_lam[0], gdn_conv_w[0], gdn_conv_b[0], gdn_a_log[0],
        gdn_dt_bias[0], gdn_norm_g[0])
    w_out = ab_w_out[0].astype(BF16)
    w_outs = [w_out[:D], w_out[D:]]
    streams = []
    for st, mixes, x in ((PROMPT, mix_p, xp), (SAMPLE, mix_s, xs)):
        x1, h2, gates = _outproj(st, mixes, w_outs, x, mod[0], norm2_g[0], router_wt, router_b)
        streams.append(_moe(st, h2, gates, x1, mod[0], *moe_w, 0, final_norm_g, False))
    xp, xs = streams
    xs_grid = xs.reshape(N_SAMPLE_SEQ, GRID_H, GRID_W, D)

    gla_p_out, gla_s_out, gla_state = _layer_c((xp, xs_grid), mod[1], state_gla[:, 0], norm1_g[1], gla_w_in[0],
                                               gla_w2[0], gla_b2[0], gla_norm_g[0])
    w_outs = [gla_w_out[0].astype(BF16)]
    x1, h2, gates = _outproj(PROMPT, [gla_p_out], w_outs, xp, mod[1], norm2_g[1], router_wt, router_b)
    y_prompt = _moe(PROMPT, h2, gates, x1, mod[1], *moe_w, 1, final_norm_g, True)
    x1, h2, gates = _outproj(SAMPLE, [gla_s_out], w_outs, xs_grid, mod[1], norm2_g[1], router_wt, router_b,
                             from_col_major=True)
    y_sample = _moe(SAMPLE, h2, gates, x1, mod[1], *moe_w, 1, final_norm_g, True, to_grid=True)

    return (y_prompt.reshape(N_PROMPT_SEQ, T_PROMPT, D), y_sample.reshape(N_SAMPLE_SEQ, T_SAMPLE, D),
            lru_state[:, None], gdn_state[:, None], gla_state[:, None])
```

```python
import functools
import math
from typing import NamedTuple

import jax
import jax.numpy as jnp
from jax import lax
from jax.experimental import pallas as pl
from jax.experimental.pallas import tpu as pltpu

F32 = jnp.float32
BF16 = jnp.bfloat16

D = 1024
N_PROMPT_SEQ, T_PROMPT = 16, 256
N_SAMPLE_SEQ, T_SAMPLE = 4, 2048
N_PROMPT = N_PROMPT_SEQ * T_PROMPT
N_SAMPLE = N_SAMPLE_SEQ * T_SAMPLE
GRID_W = 64
GRID_H = T_SAMPLE // GRID_W
EPS = 1e-6
LANES = 128
SUBLANES = 8

LRU_C = 8.0
LRU_BLOCK = 128
CONV_LEFT = 2
CONV_W = 4
GDN_H, GDN_DK, GDN_DV, GDN_CHUNK = 8, 128, 128, 64
GLA_H, GLA_DK, GLA_DV, GLA_CHUNK, GLA_RANK, GLA_TAU = 4, 128, 256, 32, 16, 16.0
N_EXPERTS, N_GROUPS, D_EXPERT = 16, 4, 512
GROUP_SZ = N_EXPERTS // N_GROUPS

TM = 256
TM_MOE = 1024
MOE_SEG = 256
MOE_ROWS = TM_MOE + (N_GROUPS - 1) * MOE_SEG
VMEM_LIMIT = 60 * 1024 * 1024

_NT = (((1,), (1,)), ((), ()))
_TN = (((0,), (0,)), ((), ()))


def _dot(a, b):
    return jnp.dot(a, b, preferred_element_type=F32)


def _dot_nt(a, b):
    return lax.dot_general(a, b, _NT, preferred_element_type=F32)


def _dot_tn(a, b):
    return lax.dot_general(a, b, _TN, preferred_element_type=F32)


def _split2(x):
    hi = x.astype(BF16)
    lo = (x - hi.astype(F32)).astype(BF16)
    return hi, lo


def _split3(x):
    hi = x.astype(BF16)
    r = x - hi.astype(F32)
    mid = r.astype(BF16)
    lo = (r - mid.astype(F32)).astype(BF16)
    return hi, mid, lo


def _dot_sel(sel_bf16, x):
    hi, mid, lo = _split3(x)
    return _dot(sel_bf16, hi) + _dot(sel_bf16, mid) + _dot(sel_bf16, lo)


def _dot_hp(a, b, dot=_dot):
    ah, al = _split2(a)
    bh, bl = _split2(b)
    return dot(ah, bh) + dot(ah, bl) + dot(al, bh)


def _sigmoid(x):
    return 1.0 / (1.0 + jnp.exp(-x))


def _silu(x):
    return x * _sigmoid(x)


def _softplus(x):
    return jnp.maximum(x, 0.0) + jnp.log1p(jnp.exp(-jnp.abs(x)))


def _gelu_tanh(x):
    return 0.5 * x * (1.0 + jnp.tanh(math.sqrt(2.0 / math.pi) * (x + 0.044715 * (x * x * x))))


def _params(*sem):
    return pltpu.CompilerParams(dimension_semantics=sem, vmem_limit_bytes=VMEM_LIMIT)


class _Stream(NamedTuple):
    n_tok: int
    t: int
    nseq: int
    per_request_mod: bool

    def mod_row(self, i, tm):
        return 1 + i // (self.t // tm) if self.per_request_mod else 0


PROMPT = _Stream(N_PROMPT, T_PROMPT, N_PROMPT_SEQ, False)
SAMPLE = _Stream(N_SAMPLE, T_SAMPLE, N_SAMPLE_SEQ, True)
COLS_PER_TILE = TM // GRID_H


def _grid_perm(to_col_major):
    i = lax.broadcasted_iota(jnp.int32, (TM, TM), 0)
    j = lax.broadcasted_iota(jnp.int32, (TM, TM), 1)
    if to_col_major:
        src = (i % GRID_H) * COLS_PER_TILE + i // GRID_H
    else:
        src = (i % COLS_PER_TILE) * GRID_H + i // COLS_PER_TILE
    return jnp.where(j == src, 1.0, 0.0).astype(BF16)


def _mod_kernel(c_ref, w_ref, b_ref, o_ref):
    c = c_ref[...]
    o_ref[0] = _dot_hp(_silu(c), w_ref[0]) + b_ref[0]


def _modulation(cvec, ada_w, ada_b):
    depth = ada_w.shape[0]
    n6 = ada_w.shape[2]
    return pl.pallas_call(
        _mod_kernel,
        grid=(depth, n6 // D),
        in_specs=[pl.BlockSpec((SUBLANES, D), lambda l, j: (0, 0)),
                  pl.BlockSpec((1, D, D), lambda l, j: (l, 0, j)),
                  pl.BlockSpec((1, 1, D), lambda l, j: (l, 0, j))],
        out_specs=pl.BlockSpec((1, SUBLANES, D), lambda l, j: (l, 0, j)),
        out_shape=jax.ShapeDtypeStruct((depth, SUBLANES, n6), F32),
        compiler_params=_params("arbitrary", "arbitrary"),
    )(cvec, ada_w, ada_b.reshape(depth, 1, n6))


def _grid_tile_spec():
    tiles_per_seq = GRID_W // COLS_PER_TILE
    return pl.BlockSpec((1, GRID_H, COLS_PER_TILE, D), lambda i: (i // tiles_per_seq, 0, i % tiles_per_seq, 0))


def _inproj_kernel(x_ref, mod_ref, g_ref, *refs, n_w, has_t, to_col_major):
    w_refs = refs[:n_w]
    o_refs = refs[n_w + has_t:2 * n_w + has_t]
    x = x_ref[0].reshape(TM, D) if to_col_major else x_ref[...]
    y = x * lax.rsqrt(jnp.mean(x * x, axis=-1, keepdims=True) + EPS) * g_ref[...]
    shift = mod_ref[0, :, 0:D]
    scale = mod_ref[0, :, D:2 * D]
    h = (y * (1.0 + scale) + shift).astype(BF16)
    if to_col_major:
        h = _dot(_grid_perm(True), h).astype(BF16)
    for w_ref, o_ref in zip(w_refs, o_refs):
        o_ref[...] = _dot(h, w_ref[...]).astype(o_ref.dtype)
    if has_t:
        wt_ref = refs[n_w]
        ot_ref = refs[2 * n_w + 1]
        ot_ref[...] = _dot_nt(wt_ref[...], h)


def _inproj(st, x, mod_l, gain, weights, out_dtypes, w_t=None, to_col_major=False):
    n_w = len(weights)
    in_specs = [_grid_tile_spec() if to_col_major else pl.BlockSpec((TM, D), lambda i: (i, 0)),
                pl.BlockSpec((1, 1, 6 * D), lambda i: (st.mod_row(i, TM), 0, 0)),
                pl.BlockSpec((1, D), lambda i: (0, 0))]
    in_specs += [pl.BlockSpec(w.shape, lambda i: (0, 0)) for w in weights]
    out_specs = [pl.BlockSpec((TM, w.shape[1]), lambda i: (i, 0)) for w in weights]
    out_shape = [jax.ShapeDtypeStruct((st.n_tok, w.shape[1]), dt) for w, dt in zip(weights, out_dtypes)]
    args = [x, mod_l.reshape(SUBLANES, 1, 6 * D), gain.reshape(1, D)] + list(weights)
    if w_t is not None:
        in_specs.append(pl.BlockSpec(w_t.shape, lambda i: (0, 0)))
        out_specs.append(pl.BlockSpec((w_t.shape[0], TM), lambda i: (0, i)))
        out_shape.append(jax.ShapeDtypeStruct((w_t.shape[0], st.n_tok), F32))
        args.append(w_t)
    return pl.pallas_call(
        functools.partial(_inproj_kernel, n_w=n_w, has_t=int(w_t is not None), to_col_major=to_col_major),
        grid=(st.n_tok // TM,),
        in_specs=in_specs, out_specs=out_specs, out_shape=out_shape,
        compiler_params=_params("arbitrary"),
    )(*args)


def _conv_from_padded(xp_ref, t, cw, cb):
    acc = cb
    for j in range(CONV_W):
        acc = acc + cw[j:j + 1, :] * xp_ref[pl.ds(SUBLANES - CONV_LEFT + j, t), :]
    return acc


def _fill_padded(xp_ref, x, t):
    zeros = jnp.zeros((SUBLANES, xp_ref.shape[1]), F32)
    xp_ref[pl.ds(0, SUBLANES), :] = zeros
    xp_ref[pl.ds(SUBLANES + t, SUBLANES), :] = zeros
    xp_ref[pl.ds(SUBLANES, t), :] = x


def _chunk_cumsum(x, c, rev):
    t = x.shape[0]
    row = lax.broadcasted_iota(jnp.int32, x.shape, 0) % c
    s = 1
    while s < c:
        if rev:
            x = x + jnp.where(row < c - s, pltpu.roll(x, t - s, 0), 0.0)
        else:
            x = x + jnp.where(row >= s, pltpu.roll(x, s, 0), 0.0)
        s *= 2
    return x


_SCAN_ROWS = 32


def _scan_tile(a, b, carry, rev):
    row = lax.broadcasted_iota(jnp.int32, a.shape, 0)
    for s in (1, 2, 4):
        if rev:
            a_s = pltpu.roll(a, SUBLANES - s, 0)
            b_s = pltpu.roll(b, SUBLANES - s, 0)
            m = row < SUBLANES - s
        else:
            a_s = pltpu.roll(a, s, 0)
            b_s = pltpu.roll(b, s, 0)
            m = row >= s
        b = jnp.where(m, a * b_s + b, b)
        a = jnp.where(m, a * a_s, a)
    h = a * carry + b
    return h, (h[0:1] if rev else h[SUBLANES - 1:SUBLANES])


def _lru_kernel(*refs, t, with_h0):
    if with_h0:
        (x_ref, y_ref, cw_ref, cb_ref, wg_ref, bg_ref, lam_ref, h0_ref,
         o_ref, xp, a_f, b_f, a_b, b_b) = refs
    else:
        (x_ref, y_ref, cw_ref, cb_ref, wg_ref, bg_ref, lam_ref,
         o_ref, st_ref, xp, a_f, b_f, a_b, b_b) = refs
    _fill_padded(xp, x_ref[...].astype(F32), t)
    xc = _conv_from_padded(xp, t, cw_ref[...], cb_ref[...])
    gates = _sigmoid(_dot(xc.astype(BF16), wg_ref[0]) + bg_ref[0])
    sp = _softplus(-lam_ref[0])
    for d, (a_ref, b_ref) in enumerate(((a_f, b_f), (a_b, b_b))):
        r = gates[:, (2 * d) * LRU_BLOCK:(2 * d + 1) * LRU_BLOCK]
        ig = gates[:, (2 * d + 1) * LRU_BLOCK:(2 * d + 2) * LRU_BLOCK]
        log_a = -LRU_C * r * sp[d:d + 1, :]
        a = jnp.exp(log_a)
        a_ref[...] = a
        b_ref[...] = jnp.sqrt(1.0 - a * a) * ig * xc

    n_it = t // _SCAN_ROWS
    if with_h0:
        c0 = (h0_ref[0, 0:1, :], h0_ref[0, 1:2, :])
    else:
        c0 = (jnp.zeros((1, LRU_BLOCK), F32), jnp.zeros((1, LRU_BLOCK), F32))

    def body(i, carry):
        cf, cb_ = carry
        base_f = pl.multiple_of(i * _SCAN_ROWS, _SCAN_ROWS)
        base_b = pl.multiple_of((n_it - 1 - i) * _SCAN_ROWS, _SCAN_ROWS)
        for k in range(_SCAN_ROWS // SUBLANES):
            rf = pl.ds(base_f + k * SUBLANES, SUBLANES)
            h, cf = _scan_tile(a_f[rf, :], b_f[rf, :], cf, False)
            b_f[rf, :] = h
            rb = pl.ds(base_b + _SCAN_ROWS - (k + 1) * SUBLANES, SUBLANES)
            h, cb_ = _scan_tile(a_b[rb, :], b_b[rb, :], cb_, True)
            b_b[rb, :] = h
        return cf, cb_

    cf, cb_ = lax.fori_loop(0, n_it, body, c0)
    o_ref[...] = (_gelu_tanh(y_ref[...].astype(F32)) * (b_f[...] + b_b[...])).astype(o_ref.dtype)
    if not with_h0:
        st_ref[0, 0:1, :] = cf
        st_ref[0, 1:2, :] = cb_


def _lru_call(st, xbr, ybr, p, h0=None):
    nb = D // LRU_BLOCK
    t, nseq = st.t, st.nseq
    tok = lambda s, j: (s, j)
    in_specs = [pl.BlockSpec((t, LRU_BLOCK), tok),
                pl.BlockSpec((t, LRU_BLOCK), tok),
                pl.BlockSpec((CONV_W, LRU_BLOCK), lambda s, j: (0, j)),
                pl.BlockSpec((1, LRU_BLOCK), lambda s, j: (0, j)),
                pl.BlockSpec((1, LRU_BLOCK, 4 * LRU_BLOCK), lambda s, j: (j, 0, 0)),
                pl.BlockSpec((1, 1, 4 * LRU_BLOCK), lambda s, j: (j, 0, 0)),
                pl.BlockSpec((1, 2, LRU_BLOCK), lambda s, j: (j, 0, 0))]
    args = [xbr, ybr, p["conv_w"], p["conv_b"], p["wg"], p["bg"], p["lam"]]
    scratch = [pltpu.VMEM((t + 2 * SUBLANES, LRU_BLOCK), F32)] + [pltpu.VMEM((t, LRU_BLOCK), F32)] * 4
    mix_shape = jax.ShapeDtypeStruct((st.n_tok, D), BF16)
    out_mix_spec = pl.BlockSpec((t, LRU_BLOCK), tok)
    if h0 is not None:
        in_specs.append(pl.BlockSpec((1, 2, LRU_BLOCK), lambda s, j: (s, 0, j)))
        args.append(h0)
        return pl.pallas_call(
            functools.partial(_lru_kernel, t=t, with_h0=True),
            grid=(nseq, nb), in_specs=in_specs, out_specs=out_mix_spec, out_shape=mix_shape,
            scratch_shapes=scratch,
            compiler_params=_params("arbitrary", "arbitrary"),
        )(*args)
    return pl.pallas_call(
        functools.partial(_lru_kernel, t=t, with_h0=False),
        grid=(nseq, nb), in_specs=in_specs,
        out_specs=[out_mix_spec, pl.BlockSpec((1, 2, LRU_BLOCK), lambda s, j: (s, 0, j))],
        out_shape=[mix_shape, jax.ShapeDtypeStruct((nseq, 2, D), F32)],
        scratch_shapes=scratch,
        compiler_params=_params("arbitrary", "arbitrary"),
    )(*args)


GDN_GROUP = 4
GDN_SLAB = GDN_GROUP * GDN_CHUNK


def _compact_select(x, c):
    g = x.shape[0] // c
    lane_blk = lax.broadcasted_iota(jnp.int32, (c, g * c), 1) // c
    out = x[0:c]
    for p in range(1, g):
        out = jnp.where(lane_blk == p, x[p * c:(p + 1) * c], out)
    return out


def _block_diag(xc, same_block):
    g = xc.shape[1] // xc.shape[0]
    return jnp.where(same_block, jnp.concatenate([xc] * g, axis=0), jnp.zeros((), xc.dtype))


def _dot_hp_bd(a, bc, same_block):
    ah, al = _split2(a)
    bh, bl = _split2(bc)
    bdh = _block_diag(bh, same_block)
    return _dot(ah, bdh) + _dot(al, bdh) + _dot(ah, _block_diag(bl, same_block))


GDN_INV_BASE = 16
GDN_HEADS_PER_STEP = LANES // GDN_CHUNK


def _tri_inverse_compact(a_cs, eye_c, same_block):
    c = eye_c.shape[0]
    row = lax.broadcasted_iota(jnp.int32, eye_c.shape, 0)
    col = lax.broadcasted_iota(jnp.int32, eye_c.shape, 1) % c
    diag = (row // GDN_INV_BASE) == (col // GDN_INV_BASE)
    p_cs = [jnp.where(diag, -a_c, 0.0) for a_c in a_cs]
    t_cs = [eye_c + p_c for p_c in p_cs]
    p_cs = [_dot_hp_bd(p_c, p_c, same_block) for p_c in p_cs]
    for _ in range(int(math.log2(GDN_INV_BASE)) - 2):
        xs = [_dot_hp_bd(jnp.concatenate([t_c, p_c], axis=0), p_c, same_block) for t_c, p_c in zip(t_cs, p_cs)]
        t_cs = [t_c + x[0:c] for t_c, x in zip(t_cs, xs)]
        p_cs = [x[c:2 * c] for x in xs]
    t_cs = [t_c + _dot_hp_bd(t_c, p_c, same_block) for t_c, p_c in zip(t_cs, p_cs)]
    b = GDN_INV_BASE
    while b < c:
        off_diag = ((row // (2 * b)) == (col // (2 * b))) & ((row // b) != (col // b))
        xs = [_dot_hp_bd(jnp.where(off_diag, a_c, 0.0), t_c, same_block) for a_c, t_c in zip(a_cs, t_cs)]
        t_cs = [t_c - _dot_hp_bd(t_c, x, same_block) for t_c, x in zip(t_cs, xs)]
        b *= 2
    return t_cs


def _gdn_scores(q, k, gcb, gcrow, beta, masks):
    c = GDN_CHUNK
    same_block, eye_c, incl_c, strict_c = masks
    kb = [k * beta[d] for d in range(2)]
    prod = _dot_nt(jnp.concatenate([kb[0], kb[1], q], axis=0).astype(BF16), k.astype(BF16))
    n = q.shape[0]
    qk_c = _compact_select(prod[2 * n:3 * n], c)
    low_half = lax.broadcasted_iota(jnp.int32, (c, LANES), 1) < c
    a_cs, qk_cs = [], []
    for d in range(2):
        gi_c = jnp.concatenate(
            [jnp.where(low_half, gcb[d][(2 * h) * c:(2 * h + 1) * c], gcb[d][(2 * h + 1) * c:(2 * h + 2) * c])
             for h in range(GDN_GROUP // 2)], axis=1)
        diff = gi_c - gcrow[d]
        decay = jnp.where(incl_c[d], jnp.exp(jnp.where(incl_c[d], diff, 0.0)), 0.0)
        a_cs.append(jnp.where(strict_c[d], _compact_select(prod[d * n:(d + 1) * n], c) * decay, 0.0))
        qk_cs.append(qk_c * decay)
    return a_cs, qk_cs


def _gdn_wy(q, k, v, gcb, beta, gl, t_c, same_block):
    eg = jnp.exp(gcb)
    rhs = jnp.concatenate([v * beta, k * beta * eg], axis=1).astype(BF16)
    uw = _dot(_block_diag(t_c.astype(BF16), same_block), rhs)
    k_dec_t = (k * jnp.exp(gl - gcb)).T
    return uw[:, 0:GDN_DV], uw[:, GDN_DV:].astype(BF16), (q * eg).astype(BF16), k_dec_t


def _pair_lanes(blocks, p, c):
    lane = None
    out = None
    for h, x in enumerate(blocks):
        half = x[:, (p // 2) * LANES:(p // 2 + 1) * LANES]
        if p % 2 != h:
            half = pltpu.roll(half, c, 1)
        if out is None:
            lane = lax.broadcasted_iota(jnp.int32, half.shape, 1)
            out = half
        else:
            out = jnp.where(lane >= h * c, half, out)
    return out


def _gdn_kernel(*refs, t, with_s0):
    if with_s0:
        (q_ref, k_ref, v_ref, z_ref, ab_ref, abt_ref, cwq_ref, cwk_ref, cwv_ref, cbq_ref, cbk_ref, cbv_ref,
         alog_ref, dtb_ref, alogc_ref, dtbc_ref, ng_ref, s0_ref,
         o_ref, xp, qs, ks, vs, osum, gl_s, gcrow_s, u_s, wq_s, qkkd_s) = refs
    else:
        (q_ref, k_ref, v_ref, z_ref, ab_ref, abt_ref, cwq_ref, cwk_ref, cwv_ref, cbq_ref, cbk_ref, cbv_ref,
         alog_ref, dtb_ref, alogc_ref, dtbc_ref, ng_ref,
         o_ref, st_ref, xp, qs, ks, vs, osum, gl_s, gcrow_s, u_s, wq_s, qkkd_s) = refs
    c = GDN_CHUNK
    n_chunks = t // c
    hp = GDN_HEADS_PER_STEP
    heads = [pl.program_id(1) * hp + hh for hh in range(hp)]
    head_lanes = [pl.ds(hh * LANES, LANES) for hh in range(hp)]

    for hh in range(hp):
        for x_ref, cw_ref, cb_ref, dst, kind in ((q_ref, cwq_ref, cbq_ref, qs, "q"),
                                                 (k_ref, cwk_ref, cbk_ref, ks, "k"),
                                                 (v_ref, cwv_ref, cbv_ref, vs, "v")):
            _fill_padded(xp, x_ref[:, head_lanes[hh]].astype(F32), t)
            y = _silu(_conv_from_padded(xp, t, cw_ref[:, head_lanes[hh]], cb_ref[:, head_lanes[hh]]))
            if kind != "v":
                y = y * lax.rsqrt(jnp.sum(y * y, axis=-1, keepdims=True) + EPS)
            if kind == "q":
                y = y * (GDN_DK ** -0.5)
            dst[hh] = y

    abt_rows = lambda idx: abt_ref[idx, 0]
    slab = GDN_SLAB
    bi = lax.broadcasted_iota(jnp.int32, (slab, slab), 0)
    bj = lax.broadcasted_iota(jnp.int32, (slab, slab), 1)
    same_chunk = (bi // c) == (bj // c)
    tri = [jnp.where(same_chunk & ((bi <= bj) if d == 1 else (bi >= bj)), 1.0, 0.0).astype(BF16) for d in range(2)]
    for hh, head in enumerate(heads):
        for d in range(2):
            a_row = abt_rows(d * GDN_H + head)
            al = alogc_ref[pl.ds(d * GDN_H + head, 1), :]
            db = dtbc_ref[pl.ds(d * GDN_H + head, 1), :]
            g_rows = -jnp.exp(al) * _softplus(a_row + db)
            gcrow_s[hh, d] = _dot_sel_right(g_rows, tri[1 - d])

    osum[...] = jnp.zeros_like(osum)
    ci = lax.broadcasted_iota(jnp.int32, (c, slab), 0)
    cj = lax.broadcasted_iota(jnp.int32, (c, slab), 1) % c
    masks = (same_chunk, jnp.where(ci == cj, 1.0, 0.0), (ci >= cj, ci <= cj), (ci > cj, ci < cj))
    if with_s0:
        s_init = tuple(jnp.concatenate([s0_ref[0, d, hh] for hh in range(hp)], axis=1) for d in range(2))
    else:
        s_init = (jnp.zeros((GDN_DK, hp * GDN_DV), F32), jnp.zeros((GDN_DK, hp * GDN_DV), F32))

    lane = lax.broadcasted_iota(jnp.int32, (slab, LANES), 1)

    def prepare(i, _):
        base = pl.multiple_of(i * slab, slab)
        rows = pl.ds(base, slab)
        ab = ab_ref[rows, :]
        gall = jnp.where(lane < 2 * GDN_H, -jnp.exp(alog_ref[...]) * _softplus(ab + dtb_ref[...]), _sigmoid(ab))
        pick = lambda idx: jnp.broadcast_to(jnp.sum(jnp.where(lane == idx, gall, 0.0), axis=-1, keepdims=True),
                                            (slab, LANES))
        beta = [[pick(2 * GDN_H + d * GDN_H + head) for d in range(2)] for head in heads]
        gcb = [[_chunk_cumsum(pick(d * GDN_H + head), c, d == 1) for d in range(2)] for head in heads]
        a_cs, qk_cs = [], []
        for hh in range(hp):
            a_hh, qk_hh = _gdn_scores(qs[hh, rows, :], ks[hh, rows, :], gcb[hh],
                                      [gcrow_s[hh, d, pl.ds(i, 1), :] for d in range(2)], beta[hh], masks)
            a_cs += a_hh
            qk_cs.append(qk_hh)
        t_cs = _tri_inverse_compact(a_cs, masks[1], same_chunk)
        for d in range(2):
            k_dec_ts = []
            for hh in range(hp):
                totals = [gcb[hh][d][p * c + (0 if d == 1 else c - 1):p * c + (0 if d == 1 else c - 1) + 1]
                          for p in range(GDN_GROUP)]
                for p in range(GDN_GROUP):
                    gl_s[hh, d, pl.ds(i * GDN_GROUP + p, 1), :] = totals[p]
                gl = jnp.concatenate([jnp.broadcast_to(row, (c, LANES)) for row in totals], axis=0)
                u, w, qg, k_dec_t = _gdn_wy(qs[hh, rows, :], ks[hh, rows, :], vs[hh, rows, :], gcb[hh][d],
                                            beta[hh][d], gl, t_cs[hh * 2 + d], same_chunk)
                k_dec_ts.append(k_dec_t)
                u_s[d, rows, head_lanes[hh]] = u
                for p in range(GDN_GROUP):
                    dst = pl.multiple_of(base * 2 + p * 2 * c, 2 * c)
                    wq_s[d, pl.ds(dst, c), head_lanes[hh]] = w[p * c:(p + 1) * c]
                    wq_s[d, pl.ds(dst + c, c), head_lanes[hh]] = qg[p * c:(p + 1) * c]
            for p in range(GDN_GROUP):
                n = i * GDN_GROUP + p
                qkkd_s[d, n, pl.ds(0, c), :] = _pair_lanes([qk_cs[hh][d] for hh in range(hp)], p, c).astype(BF16)
                qkkd_s[d, n, pl.ds(c, GDN_DK), :] = _pair_lanes(k_dec_ts, p, c).astype(BF16)
        return 0

    lax.fori_loop(0, t // slab, prepare, 0)

    sr = lax.broadcasted_iota(jnp.int32, (hp * GDN_DK, hp * GDN_DV), 0) // GDN_DK
    sc = lax.broadcasted_iota(jnp.int32, (hp * GDN_DK, hp * GDN_DV), 1) // GDN_DV
    vr = lax.broadcasted_iota(jnp.int32, (hp * c, hp * GDN_DV), 0) // c
    vc = lax.broadcasted_iota(jnp.int32, (hp * c, hp * GDN_DV), 1) // GDN_DV

    def body(i, carry):
        out = []
        for d in range(2):
            n = (n_chunks - 1 - i) if d == 1 else i
            base = pl.multiple_of(n * c, c)
            rows = pl.ds(base, c)
            s = carry[d]
            s_bd = jnp.where(sr == sc, jnp.concatenate([s] * hp, axis=0), 0.0).astype(BF16)
            ws_qs = _dot(wq_s[d, pl.ds(pl.multiple_of(n * 2 * c, 2 * c), 2 * c), :], s_bd)
            v_new = u_s[d, rows, :] - ws_qs[0:c]
            v_bd = jnp.where(vr == vc, jnp.concatenate([v_new] * hp, axis=0), 0.0).astype(BF16)
            upd = _dot(qkkd_s[d, n], v_bd)
            gl = jnp.concatenate([gl_s[hh, d, pl.ds(n, 1), :] for hh in range(hp)], axis=1)
            out.append(s * jnp.exp(gl) + upd[c:c + GDN_DK])
            osum[rows, :] = osum[rows, :] + ws_qs[c:2 * c] + upd[0:c]
        return tuple(out)

    s_fin = lax.fori_loop(0, n_chunks, body, s_init)
    for hh in range(hp):
        o = osum[:, head_lanes[hh]]
        y = o * lax.rsqrt(jnp.mean(o * o, axis=-1, keepdims=True) + EPS) * ng_ref[...]
        o_ref[:, head_lanes[hh]] = (y * _silu(z_ref[:, head_lanes[hh]].astype(F32))).astype(o_ref.dtype)
        if not with_s0:
            for d in range(2):
                st_ref[0, d, hh] = s_fin[d][:, hh * GDN_DV:(hh + 1) * GDN_DV]


def _dot_sel_right(x, sel_bf16):
    hi, mid, lo = _split3(x)
    return _dot(hi, sel_bf16) + _dot(mid, sel_bf16) + _dot(lo, sel_bf16)


def _gdn_call(st, qkv, z, ab, abt, p, s0=None):
    c = GDN_SLAB
    t, nseq = st.t, st.nseq
    n_chunks = t // c
    n_pad = max(n_chunks, SUBLANES)
    abt = abt.reshape(4 * GDN_H, nseq, n_chunks, c)
    if n_pad != n_chunks:
        abt = jnp.pad(abt, ((0, 0), (0, 0), (0, n_pad - n_chunks), (0, 0)))
    hp = GDN_HEADS_PER_STEP
    wide = hp * LANES
    nqk = GDN_H // hp
    in_specs = [pl.BlockSpec((t, wide), lambda s, h: (s, h)),
                pl.BlockSpec((t, wide), lambda s, h: (s, nqk + h)),
                pl.BlockSpec((t, wide), lambda s, h: (s, 2 * nqk + h)),
                pl.BlockSpec((t, wide), lambda s, h: (s, h)),
                pl.BlockSpec((t, LANES), lambda s, h: (s, 0)),
                pl.BlockSpec((4 * GDN_H, 1, n_pad, c), lambda s, h: (0, s, 0, 0)),
                pl.BlockSpec((CONV_W, wide), lambda s, h: (0, h)),
                pl.BlockSpec((CONV_W, wide), lambda s, h: (0, nqk + h)),
                pl.BlockSpec((CONV_W, wide), lambda s, h: (0, 2 * nqk + h)),
                pl.BlockSpec((1, wide), lambda s, h: (0, h)),
                pl.BlockSpec((1, wide), lambda s, h: (0, nqk + h)),
                pl.BlockSpec((1, wide), lambda s, h: (0, 2 * nqk + h)),
                pl.BlockSpec((1, LANES), lambda s, h: (0, 0)),
                pl.BlockSpec((1, LANES), lambda s, h: (0, 0)),
                pl.BlockSpec((4 * GDN_H, 1), lambda s, h: (0, 0)),
                pl.BlockSpec((4 * GDN_H, 1), lambda s, h: (0, 0)),
                pl.BlockSpec((1, LANES), lambda s, h: (0, 0))]
    args = [qkv, qkv, qkv, z, ab, abt, p["conv_w"], p["conv_w"], p["conv_w"], p["conv_b"], p["conv_b"], p["conv_b"],
            p["alog_row"], p["dtb_row"], p["alog_col"], p["dtb_col"], p["norm_g"]]
    scratch = [pltpu.VMEM((t + 2 * SUBLANES, LANES), F32)] + [pltpu.VMEM((hp, t, LANES), F32)] * 3 + [
        pltpu.VMEM((t, wide), F32),
        pltpu.VMEM((hp, 2, max(t // GDN_CHUNK, SUBLANES), LANES), F32), pltpu.VMEM((hp, 2, n_pad, c), F32),
        pltpu.VMEM((2, t, wide), F32), pltpu.VMEM((2, 2 * t, wide), BF16),
        pltpu.VMEM((2, t // GDN_CHUNK, GDN_CHUNK + GDN_DK, hp * GDN_CHUNK), BF16)]
    mix_shape = jax.ShapeDtypeStruct((st.n_tok, GDN_H * GDN_DV), BF16)
    out_mix_spec = pl.BlockSpec((t, wide), lambda s, h: (s, h))
    st_spec = pl.BlockSpec((1, 2, hp, GDN_DK, GDN_DV), lambda s, h: (s, 0, h, 0, 0))
    if s0 is not None:
        in_specs.append(st_spec)
        args.append(s0)
        return pl.pallas_call(
            functools.partial(_gdn_kernel, t=t, with_s0=True),
            grid=(nseq, nqk), in_specs=in_specs, out_specs=out_mix_spec, out_shape=mix_shape,
            scratch_shapes=scratch,
            compiler_params=_params("arbitrary", "arbitrary"),
        )(*args)
    return pl.pallas_call(
        functools.partial(_gdn_kernel, t=t, with_s0=False),
        grid=(nseq, nqk), in_specs=in_specs,
        out_specs=[out_mix_spec, st_spec],
        out_shape=[mix_shape, jax.ShapeDtypeStruct((nseq, 2, GDN_H, GDN_DK, GDN_DV), F32)],
        scratch_shapes=scratch,
        compiler_params=_params("arbitrary", "arbitrary"),
    )(*args)


def _gla_kernel(*refs, t, with_s0):
    if with_s0:
        (q_ref, k_ref, v_ref, z_ref, lr_ref, w2_ref, b2_ref, ng_ref, s0_ref,
         o_ref, osum, gc_s) = refs
    else:
        (q_ref, k_ref, v_ref, z_ref, lr_ref, w2_ref, b2_ref, ng_ref,
         o_ref, st_ref, osum, gc_s) = refs
    c = GLA_CHUNK
    n_chunks = t // c
    slab = 256
    bi = lax.broadcasted_iota(jnp.int32, (slab, slab), 0)
    bj = lax.broadcasted_iota(jnp.int32, (slab, slab), 1)
    same_chunk = (bi // c) == (bj // c)
    lr = lr_ref[...]
    for d in range(2):
        pre = _dot_hp(lr, w2_ref[d, 0]) + b2_ref[d]
        glog = (jnp.minimum(pre, 0.0) - jnp.log1p(jnp.exp(-jnp.abs(pre)))) * (1.0 / GLA_TAU)
        gc_s[d] = _chunk_cumsum(glog, c, d == 1)

    osum[...] = jnp.zeros_like(osum)
    n_slabs = t // slab
    per_slab = slab // c
    causal = [same_chunk & ((bi <= bj) if d == 1 else (bi >= bj)) for d in range(2)]
    own_block = (lax.broadcasted_iota(jnp.int32, (slab, per_slab * GLA_DK), 0) // c
                 == lax.broadcasted_iota(jnp.int32, (slab, per_slab * GLA_DK), 1) // GLA_DK)
    if with_s0:
        s_init = (s0_ref[0, 0, 0].T, s0_ref[0, 1, 0].T)
    else:
        s_init = (jnp.zeros((GLA_DV, GLA_DK), F32), jnp.zeros((GLA_DV, GLA_DK), F32))

    def spread(x):
        return jnp.where(own_block, jnp.concatenate([x] * per_slab, axis=1), jnp.zeros((), x.dtype))

    def chunk_row(d, row):
        return gc_s[d, pl.ds(row, 1), :]

    def per_chunk_rows(d, base, offset):
        return jnp.concatenate([jnp.broadcast_to(chunk_row(d, base + p * c + offset), (c, GLA_DK))
                                for p in range(per_slab)], axis=0)

    def body(i, carry):
        stage = []
        for d in range(2):
            rev = d == 1
            base = pl.multiple_of(((n_slabs - 1 - i) if rev else i) * slab, slab)
            rows = pl.ds(base, slab)
            q = q_ref[rows, :].astype(F32) * (GLA_DK ** -0.5)
            k = k_ref[rows, :].astype(F32)
            v = v_ref[rows, :]
            gc = gc_s[d, rows, :]
            gm = per_chunk_rows(d, base, c // 2)
            gl = per_chunk_rows(d, base, 0 if rev else c - 1)
            scores = _dot_nt((q * jnp.exp(gc - gm)).astype(BF16), (k * jnp.exp(gm - gc)).astype(BF16))
            a_mat = jnp.where(causal[d], scores, 0.0).astype(BF16)
            ds_cat = _dot_tn(v, spread((k * jnp.exp(gl - gc)).astype(BF16)))
            stage.append((base, rows, v, a_mat, ds_cat, spread((q * jnp.exp(gc)).astype(BF16))))
        out = []
        for d in range(2):
            rev = d == 1
            base, rows, v, a_mat, ds_cat, qg_spread = stage[d]
            st = carry[d]
            prev = [None] * per_slab
            for p in (range(per_slab - 1, -1, -1) if rev else range(per_slab)):
                prev[p] = st.astype(BF16)
                gl_p = chunk_row(d, base + p * c + (0 if rev else c - 1))
                st = st * jnp.exp(gl_p) + ds_cat[:, p * GLA_DK:(p + 1) * GLA_DK]
            o = _dot(a_mat, v) + _dot_nt(qg_spread, jnp.concatenate(prev, axis=1))
            osum[rows, :] = osum[rows, :] + o
            out.append(st)
        return tuple(out)

    s_f, s_b = lax.fori_loop(0, n_slabs, body, s_init)
    o = osum[...]
    y = o * lax.rsqrt(jnp.mean(o * o, axis=-1, keepdims=True) + EPS) * ng_ref[...]
    o_ref[...] = (y * _silu(z_ref[...].astype(F32))).astype(o_ref.dtype)
    if not with_s0:
        st_ref[0, 0, 0] = s_f.T
        st_ref[0, 1, 0] = s_b.T


def _gla_call(st, q, k, v, z, lr, p, s0=None):
    t, nseq = st.t, st.nseq
    in_specs = [pl.BlockSpec((t, GLA_DK), lambda s, h: (s, h)),
                pl.BlockSpec((t, GLA_DK), lambda s, h: (s, h)),
                pl.BlockSpec((t, GLA_DV), lambda s, h: (s, h)),
                pl.BlockSpec((t, GLA_DV), lambda s, h: (s, h)),
                pl.BlockSpec((t, LANES), lambda s, h: (s, 0)),
                pl.BlockSpec((2, 1, LANES, GLA_DK), lambda s, h: (0, h, 0, 0)),
                pl.BlockSpec((2, 1, GLA_DK), lambda s, h: (0, 0, h)),
                pl.BlockSpec((1, GLA_DV), lambda s, h: (0, 0))]
    args = [q, k, v, z, lr, p["w2"], p["b2"], p["norm_g"]]
    scratch = [pltpu.VMEM((t, GLA_DV), F32), pltpu.VMEM((2, t, GLA_DK), F32)]
    mix_shape = jax.ShapeDtypeStruct((st.n_tok, GLA_H * GLA_DV), BF16)
    out_mix_spec = pl.BlockSpec((t, GLA_DV), lambda s, h: (s, h))
    st_spec = pl.BlockSpec((1, 2, 1, GLA_DK, GLA_DV), lambda s, h: (s, 0, h, 0, 0))
    if s0 is not None:
        in_specs.append(st_spec)
        args.append(s0)
        return pl.pallas_call(
            functools.partial(_gla_kernel, t=t, with_s0=True),
            grid=(nseq, GLA_H), in_specs=in_specs, out_specs=out_mix_spec, out_shape=mix_shape,
            scratch_shapes=scratch,
            compiler_params=_params("arbitrary", "arbitrary"),
        )(*args)
    return pl.pallas_call(
        functools.partial(_gla_kernel, t=t, with_s0=False),
        grid=(nseq, GLA_H), in_specs=in_specs,
        out_specs=[out_mix_spec, st_spec],
        out_shape=[mix_shape, jax.ShapeDtypeStruct((nseq, 2, GLA_H, GLA_DK, GLA_DV), F32)],
        scratch_shapes=scratch,
        compiler_params=_params("arbitrary", "arbitrary"),
    )(*args)


def _route(sel, s):
    scores = []
    for g in range(N_GROUPS):
        m = sel[g * GROUP_SZ:(g + 1) * GROUP_SZ]
        best = None
        for a in range(GROUP_SZ):
            for b in range(a + 1, GROUP_SZ):
                pair = m[a] + m[b]
                best = pair if best is None else jnp.maximum(best, pair)
        scores.append(best)
    gbest = jnp.zeros_like(scores[0], dtype=jnp.int32)
    top = scores[0]
    for g in range(1, N_GROUPS):
        better = scores[g] > top
        gbest = jnp.where(better, g, gbest)
        top = jnp.where(better, scores[g], top)
    picked = []
    for e in range(N_EXPERTS):
        g = e // GROUP_SZ
        rank = jnp.zeros_like(gbest)
        for m in range(g * GROUP_SZ, (g + 1) * GROUP_SZ):
            if m == e:
                continue
            ahead = (sel[m] > sel[e]) | (sel[m] == sel[e]) if m < e else (sel[m] > sel[e])
            rank = rank + ahead.astype(jnp.int32)
        picked.append(jnp.where((gbest == g) & (rank < 2), s[e], 0.0))
    total = picked[0]
    for e in range(1, N_EXPERTS):
        total = total + picked[e]
    return [p / total for p in picked]


def _outproj_kernel(*refs, n_mix, from_col_major):
    mix_refs = refs[:n_mix]
    w_refs = refs[n_mix:2 * n_mix]
    x_ref, mod_ref, g_ref, rwt_ref, rb_ref, x1_ref, h2_ref, gates_ref = refs[2 * n_mix:]
    gate1 = mod_ref[0, :, 2 * D:3 * D]
    delta = None
    for mix_ref, w_ref in zip(mix_refs, w_refs):
        mix = mix_ref[...]
        if from_col_major:
            mix = _dot(_grid_perm(False), mix).astype(BF16)
        part = _dot(mix, w_ref[...])
        delta = part if delta is None else delta + part
    x = x_ref[0].reshape(TM, D) if from_col_major else x_ref[...]
    x1 = x + gate1 * delta
    x1_ref[...] = x1
    y = x1 * lax.rsqrt(jnp.mean(x1 * x1, axis=-1, keepdims=True) + EPS) * g_ref[...]
    h2 = y * (1.0 + mod_ref[0, :, 4 * D:5 * D]) + mod_ref[0, :, 3 * D:4 * D]
    h2_ref[...] = h2.astype(BF16)
    rw_hi, rw_lo = _split2(rwt_ref[...])
    h2_hi, h2_lo = _split2(h2)
    both = _dot_nt(jnp.concatenate([rw_hi, rw_lo], axis=0), h2_hi)
    logits = both[0:N_EXPERTS] + both[N_EXPERTS:2 * N_EXPERTS] + _dot_nt(rw_hi, h2_lo)
    s_all = _sigmoid(logits)
    sel_all = s_all + rb_ref[...]
    s = [s_all[e:e + 1, :] for e in range(N_EXPERTS)]
    sel = [sel_all[e:e + 1, :] for e in range(N_EXPERTS)]
    gate_rows = _route(sel, s) + [jnp.zeros((LANES - N_EXPERTS, TM), F32)]
    gates_ref[...] = jnp.concatenate(gate_rows, axis=0).T


def _outproj(st, mixes, w_outs, x, mod_l, gain2, router_wt, router_b, from_col_major=False):
    n_mix = len(mixes)
    in_specs = [pl.BlockSpec((TM, m.shape[1]), lambda i: (i, 0)) for m in mixes]
    in_specs += [pl.BlockSpec(w.shape, lambda i: (0, 0)) for w in w_outs]
    in_specs += [_grid_tile_spec() if from_col_major else pl.BlockSpec((TM, D), lambda i: (i, 0)),
                 pl.BlockSpec((1, 1, 6 * D), lambda i: (st.mod_row(i, TM), 0, 0)),
                 pl.BlockSpec((1, D), lambda i: (0, 0)),
                 pl.BlockSpec((N_EXPERTS, D), lambda i: (0, 0)),
                 pl.BlockSpec((N_EXPERTS, 1), lambda i: (0, 0))]
    return pl.pallas_call(
        functools.partial(_outproj_kernel, n_mix=n_mix, from_col_major=from_col_major),
        grid=(st.n_tok // TM,),
        in_specs=in_specs,
        out_specs=[pl.BlockSpec((TM, D), lambda i: (i, 0)),
                   pl.BlockSpec((TM, D), lambda i: (i, 0)),
                   pl.BlockSpec((TM, LANES), lambda i: (i, 0))],
        out_shape=[jax.ShapeDtypeStruct((st.n_tok, D), F32),
                   jax.ShapeDtypeStruct((st.n_tok, D), BF16),
                   jax.ShapeDtypeStruct((st.n_tok, LANES), F32)],
        compiler_params=_params("arbitrary"),
    )(*mixes, *w_outs, x, mod_l.reshape(SUBLANES, 1, 6 * D), gain2.reshape(1, D), router_wt,
      router_b.reshape(N_EXPERTS, 1))


def _moe_kernel(h_ref, gates_ref, x1_ref, mod_ref, wg_ref, wu_ref, wd_ref, fg_ref, o_ref,
                xs, acc, gs, pos_s, meta_s, wg_s, wu_s, wd_s, *, final_norm, to_grid):
    e = pl.program_id(1)
    tm = TM_MOE

    @pl.when(e == 0)
    def _():
        gates = gates_ref[...]
        ei = lax.broadcasted_iota(jnp.int32, (LANES, LANES), 0)
        gj = lax.broadcasted_iota(jnp.int32, (LANES, LANES), 1)
        member = jnp.where((ei < N_EXPERTS) & (ei // GROUP_SZ == gj), 1.0, 0.0).astype(BF16)
        onehot = jnp.where(_dot(gates.astype(BF16), member) > 0.0, 1.0, 0.0)
        ti = lax.broadcasted_iota(jnp.int32, (tm, tm), 0)
        tj = lax.broadcasted_iota(jnp.int32, (tm, tm), 1)
        rank = _dot(jnp.where(ti >= tj, 1.0, 0.0).astype(BF16), onehot.astype(BF16))
        count = rank[tm - 1:tm, :]
        blocks = jnp.floor((count + (MOE_SEG - 1.0)) * (1.0 / MOE_SEG))
        before = jnp.where(ei < gj, 1.0, 0.0).astype(BF16)
        base = _dot(jnp.broadcast_to(blocks, (SUBLANES, LANES)).astype(BF16), before)[0:1] * MOE_SEG
        pos = jnp.sum(onehot * (base + rank - 1.0), axis=-1, keepdims=True)
        pos_s[...] = jnp.broadcast_to(pos, (tm, LANES))
        meta_s[...] = jnp.concatenate([base, blocks, jnp.zeros((SUBLANES - 2, LANES), F32)], axis=0)
        pos_row = pos_s[...].T[0:1, :]
        ri = lax.broadcasted_iota(jnp.int32, (MOE_ROWS, tm), 0).astype(F32)
        perm = jnp.where(ri == pos_row, 1.0, 0.0).astype(BF16)
        xs[...] = _dot(perm, h_ref[...]).astype(BF16)
        g_hi, g_lo = _split2(gates)
        gs[...] = _dot(perm, g_hi) + _dot(perm, g_lo)
        acc[...] = jnp.zeros_like(acc)

    group = e // GROUP_SZ
    meta = meta_s[...]
    m_row = lax.broadcasted_iota(jnp.int32, meta.shape, 0)
    m_lane = lax.broadcasted_iota(jnp.int32, meta.shape, 1)
    base = jnp.sum(jnp.where((m_row == 0) & (m_lane == group), meta, 0.0)).astype(jnp.int32)
    n_blocks = jnp.sum(jnp.where((m_row == 1) & (m_lane == group), meta, 0.0)).astype(jnp.int32)
    wg_s[...] = wg_ref[0, 0].astype(BF16)
    wu_s[...] = wu_ref[0, 0].astype(BF16)
    wd_s[...] = wd_ref[0, 0].astype(BF16)
    lane = lax.broadcasted_iota(jnp.int32, (MOE_SEG, LANES), 1)

    def block(b, _):
        rows = pl.ds(pl.multiple_of(base + b * MOE_SEG, MOE_SEG), MOE_SEG)
        x = xs[rows, :]
        gate = jnp.sum(jnp.where(lane == e, gs[rows, :], 0.0), axis=-1, keepdims=True)
        hid = _silu(_dot(x, wg_s[...])) * _dot(x, wu_s[...]) * gate
        acc[rows, :] += _dot(hid.astype(BF16), wd_s[...])
        return 0

    lax.fori_loop(0, n_blocks, block, 0)

    @pl.when(e == N_EXPERTS - 1)
    def _():
        li = lax.broadcasted_iota(jnp.int32, (tm, MOE_ROWS), 1).astype(F32)
        unperm = jnp.where(li == pos_s[:, 0:1], 1.0, 0.0).astype(BF16)
        moe = _dot(unperm, acc[...].astype(BF16))
        out = x1_ref[...] + mod_ref[0, :, 5 * D:6 * D] * moe
        if final_norm:
            out = out * lax.rsqrt(jnp.mean(out * out, axis=-1, keepdims=True) + EPS) * fg_ref[...]
        if to_grid:
            for j in range(TM_MOE // TM):
                o_ref[0, :, j * COLS_PER_TILE:(j + 1) * COLS_PER_TILE, :] = (
                    out[j * TM:(j + 1) * TM].reshape(GRID_H, COLS_PER_TILE, D))
        else:
            o_ref[...] = out


def _moe(st, h2, gates, x1, mod_l, w_gate, w_up, w_down, layer, final_gain, final_norm, to_grid=False):
    tm = TM_MOE
    if to_grid:
        tiles_per_seq = st.t // tm
        cols = tm // GRID_H
        out_spec = pl.BlockSpec((1, GRID_H, cols, D), lambda i, e: (i // tiles_per_seq, 0, i % tiles_per_seq, 0))
        out_shape = jax.ShapeDtypeStruct((st.nseq, GRID_H, GRID_W, D), F32)
    else:
        out_spec = pl.BlockSpec((tm, D), lambda i, e: (i, 0))
        out_shape = jax.ShapeDtypeStruct((st.n_tok, D), F32)
    return pl.pallas_call(
        functools.partial(_moe_kernel, final_norm=final_norm, to_grid=to_grid),
        grid=(st.n_tok // tm, N_EXPERTS),
        in_specs=[pl.BlockSpec((tm, D), lambda i, e: (i, 0)),
                  pl.BlockSpec((tm, LANES), lambda i, e: (i, 0)),
                  pl.BlockSpec((tm, D), lambda i, e: (i, 0)),
                  pl.BlockSpec((1, 1, 6 * D), lambda i, e: (st.mod_row(i, tm), 0, 0)),
                  pl.BlockSpec((1, 1, D, D_EXPERT), lambda i, e: (layer, e, 0, 0)),
                  pl.BlockSpec((1, 1, D, D_EXPERT), lambda i, e: (layer, e, 0, 0)),
                  pl.BlockSpec((1, 1, D_EXPERT, D), lambda i, e: (layer, e, 0, 0)),
                  pl.BlockSpec((1, D), lambda i, e: (0, 0))],
        out_specs=out_spec, out_shape=out_shape,
        scratch_shapes=[pltpu.VMEM((MOE_ROWS, D), BF16), pltpu.VMEM((MOE_ROWS, D), F32),
                        pltpu.VMEM((MOE_ROWS, LANES), F32), pltpu.VMEM((tm, LANES), F32),
                        pltpu.VMEM((SUBLANES, LANES), F32), pltpu.VMEM((D, D_EXPERT), BF16),
                        pltpu.VMEM((D, D_EXPERT), BF16), pltpu.VMEM((D_EXPERT, D), BF16)],
        compiler_params=_params("arbitrary", "arbitrary"),
    )(h2, gates, x1, mod_l.reshape(SUBLANES, 1, 6 * D), w_gate, w_up, w_down, final_gain.reshape(1, D))


def _layer_ab(xs, mod_l, state_lru, state_gdn, norm1_g, w_in, lru_conv_w, lru_conv_b, lru_wa, lru_ba, lru_wx,
              lru_bx, lru_lam, gdn_conv_w, gdn_conv_b, gdn_a_log, gdn_dt_bias, gdn_norm_g):
    w = w_in.astype(BF16)
    n_qkv = 2 * GDN_H * GDN_DK + GDN_H * GDN_DV
    o = 0
    w_y, o = w[:, o:o + D], o + D
    w_x, o = w[:, o:o + D], o + D
    w_qkv, o = w[:, o:o + n_qkv], o + n_qkv
    w_z, o = w[:, o:o + GDN_H * GDN_DV], o + GDN_H * GDN_DV
    w_ab = w[:, o:]
    w_ab_pad = jnp.pad(w_ab, ((0, 0), (0, LANES - w_ab.shape[1])))
    proj = [_inproj(st, x, mod_l, norm1_g, [w_y, w_x, w_qkv, w_z, w_ab_pad], [BF16, BF16, BF16, BF16, F32],
                    w_t=w_ab.T) for st, x in zip((PROMPT, SAMPLE), xs)]

    nb = D // LRU_BLOCK
    wg = jnp.stack([lru_wa[0], lru_wx[0], lru_wa[1], lru_wx[1]], axis=1)
    wg = wg.transpose(0, 2, 1, 3).reshape(nb, LRU_BLOCK, 4 * LRU_BLOCK).astype(BF16)
    bg = jnp.stack([lru_ba[0], lru_bx[0], lru_ba[1], lru_bx[1]], axis=0)
    bg = bg.reshape(4, nb, LRU_BLOCK).transpose(1, 0, 2).reshape(nb, 1, 4 * LRU_BLOCK)
    lam = lru_lam.reshape(2, nb, LRU_BLOCK).transpose(1, 0, 2)
    lru_p = dict(conv_w=lru_conv_w, conv_b=lru_conv_b.reshape(1, D), wg=wg, bg=bg, lam=lam)
    (ybr_p, xbr_p, qkv_p, z_p, ab_p, abt_p), (ybr_s, xbr_s, qkv_s, z_s, ab_s, abt_s) = proj
    lru_p_out, lru_state = _lru_call(PROMPT, xbr_p, ybr_p, lru_p)
    lru_s_out = _lru_call(SAMPLE, xbr_s, ybr_s, lru_p, h0=state_lru)

    pad16 = lambda v: jnp.pad(v.reshape(1, 2 * GDN_H), ((0, 0), (0, LANES - 2 * GDN_H)))
    col32 = lambda v: jnp.pad(v.reshape(2 * GDN_H, 1), ((0, 2 * GDN_H), (0, 0)))
    gdn_p = dict(conv_w=gdn_conv_w, conv_b=gdn_conv_b.reshape(1, n_qkv), alog_row=pad16(gdn_a_log),
                 dtb_row=pad16(gdn_dt_bias), alog_col=col32(gdn_a_log), dtb_col=col32(gdn_dt_bias),
                 norm_g=gdn_norm_g.reshape(1, GDN_DV))
    gdn_p_out, gdn_state = _gdn_call(PROMPT, qkv_p, z_p, ab_p, abt_p, gdn_p)
    gdn_s_out = _gdn_call(SAMPLE, qkv_s, z_s, ab_s, abt_s, gdn_p, s0=state_gdn)
    return [lru_p_out, gdn_p_out], [lru_s_out, gdn_s_out], lru_state, gdn_state


def _layer_c(xs, mod_l, state_gla, norm1_g, w_in, w2, b2, norm_g):
    w = w_in.astype(BF16)
    nk = GLA_H * GLA_DK
    nv = GLA_H * GLA_DV
    w_q, w_k, w_v, w_z = w[:, 0:nk], w[:, nk:2 * nk], w[:, 2 * nk:2 * nk + nv], w[:, 2 * nk + nv:2 * nk + 2 * nv]
    w_lr = jnp.pad(w[:, 2 * nk + 2 * nv:], ((0, 0), (0, LANES - 2 * GLA_RANK)))
    weights, dtypes = [w_q, w_k, w_v, w_z, w_lr], [BF16, BF16, BF16, BF16, F32]
    proj_p = _inproj(PROMPT, xs[0], mod_l, norm1_g, weights, dtypes)
    proj_s = _inproj(SAMPLE, xs[1], mod_l, norm1_g, weights, dtypes, to_col_major=True)
    w2h = w2.reshape(2, GLA_RANK, GLA_H, GLA_DK).transpose(0, 2, 1, 3)
    w2big = jnp.zeros((2, GLA_H, LANES, GLA_DK), F32)
    w2big = w2big.at[0, :, 0:GLA_RANK].set(w2h[0]).at[1, :, GLA_RANK:2 * GLA_RANK].set(w2h[1])
    gla_p = dict(w2=w2big, b2=b2.reshape(2, 1, nk), norm_g=norm_g.reshape(1, GLA_DV))
    gla_p_out, gla_state = _gla_call(PROMPT, *proj_p, gla_p)
    gla_s_out = _gla_call(SAMPLE, *proj_s, gla_p, s0=state_gla)
    return gla_p_out, gla_s_out, gla_state


def kernel(x_prompt, x_sample, state_lru, state_gdn, state_gla, c, c_ctx, ada_w, ada_b, norm1_g, norm2_g,
           final_norm_g, ab_w_in, lru_conv_w, lru_conv_b, lru_wa, lru_ba, lru_wx, lru_bx, lru_lam, gdn_conv_w,
           gdn_conv_b, gdn_a_log, gdn_dt_bias, gdn_norm_g, ab_w_out, gla_w_in, gla_w2, gla_b2, gla_norm_g,
           gla_w_out, router_w, router_b, moe_w_gate, moe_w_up, moe_w_down):
    xp = x_prompt.reshape(N_PROMPT, D)
    xs = x_sample.reshape(N_SAMPLE, D)
    cvec = jnp.concatenate([c_ctx.reshape(1, D), c, jnp.zeros((SUBLANES - 1 - N_SAMPLE_SEQ, D), F32)], axis=0)
    mod = _modulation(cvec, ada_w, ada_b)
    router_wt = router_w.T
    moe_w = (moe_w_gate, moe_w_up, moe_w_down)

    mix_p, mix_s, lru_state, gdn_state = _layer_ab(
        (xp, xs), mod[0], state_lru[:, 0], state_gdn[:, 0], norm1_g[0], ab_w_in[0], lru_conv_w[0], lru_conv_b[0],
        lru_wa[0], lru_ba[0], lru_wx[0], lru_bx[0], lru_lam[0], gdn_conv_w[0], gdn_conv_b[0], gdn_a_log[0],
        gdn_dt_bias[0], gdn_norm_g[0])
    w_out = ab_w_out[0].astype(BF16)
    w_outs = [w_out[:D], w_out[D:]]
    streams = []
    for st, mixes, x in ((PROMPT, mix_p, xp), (SAMPLE, mix_s, xs)):
        x1, h2, gates = _outproj(st, mixes, w_outs, x, mod[0], norm2_g[0], router_wt, router_b)
        streams.append(_moe(st, h2, gates, x1, mod[0], *moe_w, 0, final_norm_g, False))
    xp, xs = streams
    xs_grid = xs.reshape(N_SAMPLE_SEQ, GRID_H, GRID_W, D)

    gla_p_out, gla_s_out, gla_state = _layer_c((xp, xs_grid), mod[1], state_gla[:, 0], norm1_g[1], gla_w_in[0],
                                               gla_w2[0], gla_b2[0], gla_norm_g[0])
    w_outs = [gla_w_out[0].astype(BF16)]
    x1, h2, gates = _outproj(PROMPT, [gla_p_out], w_outs, xp, mod[1], norm2_g[1], router_wt, router_b)
    y_prompt = _moe(PROMPT, h2, gates, x1, mod[1], *moe_w, 1, final_norm_g, True)
    x1, h2, gates = _outproj(SAMPLE, [gla_s_out], w_outs, xs_grid, mod[1], norm2_g[1], router_wt, router_b,
                             from_col_major=True)
    y_sample = _moe(SAMPLE, h2, gates, x1, mod[1], *moe_w, 1, final_norm_g, True, to_grid=True)

    return (y_prompt.reshape(N_PROMPT_SEQ, T_PROMPT, D), y_sample.reshape(N_SAMPLE_SEQ, T_SAMPLE, D),
            lru_state[:, None], gdn_state[:, None], gla_state[:, None])
```

```python
import functools
import math
from typing import NamedTuple

import jax
import jax.numpy as jnp
from jax import lax
from jax.experimental import pallas as pl
from jax.experimental.pallas import tpu as pltpu

F32 = jnp.float32
BF16 = jnp.bfloat16

D = 1024
N_PROMPT_SEQ, T_PROMPT = 16, 256
N_SAMPLE_SEQ, T_SAMPLE = 4, 2048
N_PROMPT = N_PROMPT_SEQ * T_PROMPT
N_SAMPLE = N_SAMPLE_SEQ * T_SAMPLE
GRID_W = 64
GRID_H = T_SAMPLE // GRID_W
EPS = 1e-6
LANES = 128
SUBLANES = 8

LRU_C = 8.0
LRU_BLOCK = 128
CONV_LEFT = 2
CONV_W = 4
GDN_H, GDN_DK, GDN_DV, GDN_CHUNK = 8, 128, 128, 64
GLA_H, GLA_DK, GLA_DV, GLA_CHUNK, GLA_RANK, GLA_TAU = 4, 128, 256, 32, 16, 16.0
N_EXPERTS, N_GROUPS, D_EXPERT = 16, 4, 512
GROUP_SZ = N_EXPERTS // N_GROUPS

TM = 256
TM_MOE = 1024
MOE_EXPERTS_PER_STEP = 2
MOE_SEG = 256
MOE_ROWS = TM_MOE + (N_GROUPS - 1) * MOE_SEG
VMEM_LIMIT = 60 * 1024 * 1024

_NT = (((1,), (1,)), ((), ()))
_TN = (((0,), (0,)), ((), ()))


def _dot(a, b):
    return jnp.dot(a, b, preferred_element_type=F32)


def _dot_nt(a, b):
    return lax.dot_general(a, b, _NT, preferred_element_type=F32)


def _dot_tn(a, b):
    return lax.dot_general(a, b, _TN, preferred_element_type=F32)


def _split2(x):
    hi = x.astype(BF16)
    lo = (x - hi.astype(F32)).astype(BF16)
    return hi, lo


def _split3(x):
    hi = x.astype(BF16)
    r = x - hi.astype(F32)
    mid = r.astype(BF16)
    lo = (r - mid.astype(F32)).astype(BF16)
    return hi, mid, lo


def _dot_sel(sel_bf16, x):
    hi, mid, lo = _split3(x)
    return _dot(sel_bf16, hi) + _dot(sel_bf16, mid) + _dot(sel_bf16, lo)


def _dot_hp(a, b, dot=_dot):
    ah, al = _split2(a)
    bh, bl = _split2(b)
    return dot(ah, bh) + dot(ah, bl) + dot(al, bh)


def _sigmoid(x):
    return 1.0 / (1.0 + jnp.exp(-x))


def _silu(x):
    return x * _sigmoid(x)


def _softplus(x):
    return jnp.maximum(x, 0.0) + jnp.log1p(jnp.exp(-jnp.abs(x)))


def _gelu_tanh(x):
    return 0.5 * x * (1.0 + jnp.tanh(math.sqrt(2.0 / math.pi) * (x + 0.044715 * (x * x * x))))


def _params(*sem):
    return pltpu.CompilerParams(dimension_semantics=sem, vmem_limit_bytes=VMEM_LIMIT)


class _Stream(NamedTuple):
    n_tok: int
    t: int
    nseq: int
    per_request_mod: bool

    def mod_row(self, i, tm):
        return 1 + i // (self.t // tm) if self.per_request_mod else 0


PROMPT = _Stream(N_PROMPT, T_PROMPT, N_PROMPT_SEQ, False)
SAMPLE = _Stream(N_SAMPLE, T_SAMPLE, N_SAMPLE_SEQ, True)
COLS_PER_TILE = TM // GRID_H


def _grid_perm(to_col_major):
    i = lax.broadcasted_iota(jnp.int32, (TM, TM), 0)
    j = lax.broadcasted_iota(jnp.int32, (TM, TM), 1)
    if to_col_major:
        src = (i % GRID_H) * COLS_PER_TILE + i // GRID_H
    else:
        src = (i % COLS_PER_TILE) * GRID_H + i // COLS_PER_TILE
    return jnp.where(j == src, 1.0, 0.0).astype(BF16)


def _mod_kernel(c_ref, w_ref, b_ref, o_ref):
    c = c_ref[...]
    o_ref[0] = _dot_hp(_silu(c), w_ref[0]) + b_ref[0]


def _modulation(cvec, ada_w, ada_b):
    depth = ada_w.shape[0]
    n6 = ada_w.shape[2]
    return pl.pallas_call(
        _mod_kernel,
        grid=(depth, n6 // D),
        in_specs=[pl.BlockSpec((SUBLANES, D), lambda l, j: (0, 0)),
                  pl.BlockSpec((1, D, D), lambda l, j: (l, 0, j)),
                  pl.BlockSpec((1, 1, D), lambda l, j: (l, 0, j))],
        out_specs=pl.BlockSpec((1, SUBLANES, D), lambda l, j: (l, 0, j)),
        out_shape=jax.ShapeDtypeStruct((depth, SUBLANES, n6), F32),
        compiler_params=_params("arbitrary", "arbitrary"),
    )(cvec, ada_w, ada_b.reshape(depth, 1, n6))


def _grid_tile_spec():
    tiles_per_seq = GRID_W // COLS_PER_TILE
    return pl.BlockSpec((1, GRID_H, COLS_PER_TILE, D), lambda i: (i // tiles_per_seq, 0, i % tiles_per_seq, 0))


def _inproj_kernel(x_ref, mod_ref, g_ref, *refs, n_w, has_t, to_col_major):
    w_refs = refs[:n_w]
    o_refs = refs[n_w + has_t:2 * n_w + has_t]
    x = x_ref[0].reshape(TM, D) if to_col_major else x_ref[...]
    y = x * lax.rsqrt(jnp.mean(x * x, axis=-1, keepdims=True) + EPS) * g_ref[...]
    shift = mod_ref[0, :, 0:D]
    scale = mod_ref[0, :, D:2 * D]
    h = (y * (1.0 + scale) + shift).astype(BF16)
    if to_col_major:
        h = _dot(_grid_perm(True), h).astype(BF16)
    for w_ref, o_ref in zip(w_refs, o_refs):
        o_ref[...] = _dot(h, w_ref[...]).astype(o_ref.dtype)
    if has_t:
        wt_ref = refs[n_w]
        ot_ref = refs[2 * n_w + 1]
        ot_ref[...] = _dot_nt(wt_ref[...], h)


def _inproj(st, x, mod_l, gain, weights, out_dtypes, w_t=None, to_col_major=False):
    n_w = len(weights)
    in_specs = [_grid_tile_spec() if to_col_major else pl.BlockSpec((TM, D), lambda i: (i, 0)),
                pl.BlockSpec((1, 1, 6 * D), lambda i: (st.mod_row(i, TM), 0, 0)),
                pl.BlockSpec((1, D), lambda i: (0, 0))]
    in_specs += [pl.BlockSpec(w.shape, lambda i: (0, 0)) for w in weights]
    out_specs = [pl.BlockSpec((TM, w.shape[1]), lambda i: (i, 0)) for w in weights]
    out_shape = [jax.ShapeDtypeStruct((st.n_tok, w.shape[1]), dt) for w, dt in zip(weights, out_dtypes)]
    args = [x, mod_l.reshape(SUBLANES, 1, 6 * D), gain.reshape(1, D)] + list(weights)
    if w_t is not None:
        in_specs.append(pl.BlockSpec(w_t.shape, lambda i: (0, 0)))
        out_specs.append(pl.BlockSpec((w_t.shape[0], TM), lambda i: (0, i)))
        out_shape.append(jax.ShapeDtypeStruct((w_t.shape[0], st.n_tok), F32))
        args.append(w_t)
    return pl.pallas_call(
        functools.partial(_inproj_kernel, n_w=n_w, has_t=int(w_t is not None), to_col_major=to_col_major),
        grid=(st.n_tok // TM,),
        in_specs=in_specs, out_specs=out_specs, out_shape=out_shape,
        compiler_params=_params("arbitrary"),
    )(*args)


def _conv_from_padded(xp_ref, t, cw, cb):
    acc = cb
    for j in range(CONV_W):
        acc = acc + cw[j:j + 1, :] * xp_ref[pl.ds(SUBLANES - CONV_LEFT + j, t), :]
    return acc


def _fill_padded(xp_ref, x, t):
    zeros = jnp.zeros((SUBLANES, xp_ref.shape[1]), F32)
    xp_ref[pl.ds(0, SUBLANES), :] = zeros
    xp_ref[pl.ds(SUBLANES + t, SUBLANES), :] = zeros
    xp_ref[pl.ds(SUBLANES, t), :] = x


def _chunk_cumsum(x, c, rev):
    t = x.shape[0]
    row = lax.broadcasted_iota(jnp.int32, x.shape, 0) % c
    s = 1
    while s < c:
        if rev:
            x = x + jnp.where(row < c - s, pltpu.roll(x, t - s, 0), 0.0)
        else:
            x = x + jnp.where(row >= s, pltpu.roll(x, s, 0), 0.0)
        s *= 2
    return x


_SCAN_ROWS = 32


def _scan_tile(a, b, carry, rev):
    row = lax.broadcasted_iota(jnp.int32, a.shape, 0)
    for s in (1, 2, 4):
        if rev:
            a_s = pltpu.roll(a, SUBLANES - s, 0)
            b_s = pltpu.roll(b, SUBLANES - s, 0)
            m = row < SUBLANES - s
        else:
            a_s = pltpu.roll(a, s, 0)
            b_s = pltpu.roll(b, s, 0)
            m = row >= s
        b = jnp.where(m, a * b_s + b, b)
        a = jnp.where(m, a * a_s, a)
    h = a * carry + b
    return h, (h[0:1] if rev else h[SUBLANES - 1:SUBLANES])


def _lru_kernel(*refs, t, with_h0):
    if with_h0:
        (x_ref, y_ref, cw_ref, cb_ref, wg_ref, bg_ref, lam_ref, h0_ref,
         o_ref, xp, a_f, b_f, a_b, b_b) = refs
    else:
        (x_ref, y_ref, cw_ref, cb_ref, wg_ref, bg_ref, lam_ref,
         o_ref, st_ref, xp, a_f, b_f, a_b, b_b) = refs
    _fill_padded(xp, x_ref[...].astype(F32), t)
    xc = _conv_from_padded(xp, t, cw_ref[...], cb_ref[...])
    gates = _sigmoid(_dot(xc.astype(BF16), wg_ref[0]) + bg_ref[0])
    sp = _softplus(-lam_ref[0])
    for d, (a_ref, b_ref) in enumerate(((a_f, b_f), (a_b, b_b))):
        r = gates[:, (2 * d) * LRU_BLOCK:(2 * d + 1) * LRU_BLOCK]
        ig = gates[:, (2 * d + 1) * LRU_BLOCK:(2 * d + 2) * LRU_BLOCK]
        log_a = -LRU_C * r * sp[d:d + 1, :]
        a = jnp.exp(log_a)
        a_ref[...] = a
        b_ref[...] = jnp.sqrt(1.0 - a * a) * ig * xc

    n_it = t // _SCAN_ROWS
    if with_h0:
        c0 = (h0_ref[0, 0:1, :], h0_ref[0, 1:2, :])
    else:
        c0 = (jnp.zeros((1, LRU_BLOCK), F32), jnp.zeros((1, LRU_BLOCK), F32))

    def body(i, carry):
        cf, cb_ = carry
        base_f = pl.multiple_of(i * _SCAN_ROWS, _SCAN_ROWS)
        base_b = pl.multiple_of((n_it - 1 - i) * _SCAN_ROWS, _SCAN_ROWS)
        for k in range(_SCAN_ROWS // SUBLANES):
            rf = pl.ds(base_f + k * SUBLANES, SUBLANES)
            h, cf = _scan_tile(a_f[rf, :], b_f[rf, :], cf, False)
            b_f[rf, :] = h
            rb = pl.ds(base_b + _SCAN_ROWS - (k + 1) * SUBLANES, SUBLANES)
            h, cb_ = _scan_tile(a_b[rb, :], b_b[rb, :], cb_, True)
            b_b[rb, :] = h
        return cf, cb_

    cf, cb_ = lax.fori_loop(0, n_it, body, c0)
    o_ref[...] = (_gelu_tanh(y_ref[...].astype(F32)) * (b_f[...] + b_b[...])).astype(o_ref.dtype)
    if not with_h0:
        st_ref[0, 0:1, :] = cf
        st_ref[0, 1:2, :] = cb_


def _lru_call(st, xbr, ybr, p, h0=None):
    nb = D // LRU_BLOCK
    t, nseq = st.t, st.nseq
    tok = lambda s, j: (s, j)
    in_specs = [pl.BlockSpec((t, LRU_BLOCK), tok),
                pl.BlockSpec((t, LRU_BLOCK), tok),
                pl.BlockSpec((CONV_W, LRU_BLOCK), lambda s, j: (0, j)),
                pl.BlockSpec((1, LRU_BLOCK), lambda s, j: (0, j)),
                pl.BlockSpec((1, LRU_BLOCK, 4 * LRU_BLOCK), lambda s, j: (j, 0, 0)),
                pl.BlockSpec((1, 1, 4 * LRU_BLOCK), lambda s, j: (j, 0, 0)),
                pl.BlockSpec((1, 2, LRU_BLOCK), lambda s, j: (j, 0, 0))]
    args = [xbr, ybr, p["conv_w"], p["conv_b"], p["wg"], p["bg"], p["lam"]]
    scratch = [pltpu.VMEM((t + 2 * SUBLANES, LRU_BLOCK), F32)] + [pltpu.VMEM((t, LRU_BLOCK), F32)] * 4
    mix_shape = jax.ShapeDtypeStruct((st.n_tok, D), BF16)
    out_mix_spec = pl.BlockSpec((t, LRU_BLOCK), tok)
    if h0 is not None:
        in_specs.append(pl.BlockSpec((1, 2, LRU_BLOCK), lambda s, j: (s, 0, j)))
        args.append(h0)
        return pl.pallas_call(
            functools.partial(_lru_kernel, t=t, with_h0=True),
            grid=(nseq, nb), in_specs=in_specs, out_specs=out_mix_spec, out_shape=mix_shape,
            scratch_shapes=scratch,
            compiler_params=_params("arbitrary", "arbitrary"),
        )(*args)
    return pl.pallas_call(
        functools.partial(_lru_kernel, t=t, with_h0=False),
        grid=(nseq, nb), in_specs=in_specs,
        out_specs=[out_mix_spec, pl.BlockSpec((1, 2, LRU_BLOCK), lambda s, j: (s, 0, j))],
        out_shape=[mix_shape, jax.ShapeDtypeStruct((nseq, 2, D), F32)],
        scratch_shapes=scratch,
        compiler_params=_params("arbitrary", "arbitrary"),
    )(*args)


GDN_GROUP = 4
GDN_SLAB = GDN_GROUP * GDN_CHUNK


def _compact_select(x, c):
    g = x.shape[0] // c
    lane_blk = lax.broadcasted_iota(jnp.int32, (c, g * c), 1) // c
    out = x[0:c]
    for p in range(1, g):
        out = jnp.where(lane_blk == p, x[p * c:(p + 1) * c], out)
    return out


def _block_diag(xc, same_block):
    g = xc.shape[1] // xc.shape[0]
    return jnp.where(same_block, jnp.concatenate([xc] * g, axis=0), jnp.zeros((), xc.dtype))


def _dot_hp_bd(a, bc, same_block):
    ah, al = _split2(a)
    bh, bl = _split2(bc)
    bdh = _block_diag(bh, same_block)
    return _dot(ah, bdh) + _dot(al, bdh) + _dot(ah, _block_diag(bl, same_block))


GDN_INV_BASE = 16
GDN_HEADS_PER_STEP = LANES // GDN_CHUNK


def _tri_inverse_compact(a_cs, eye_c, same_block):
    c = eye_c.shape[0]
    row = lax.broadcasted_iota(jnp.int32, eye_c.shape, 0)
    col = lax.broadcasted_iota(jnp.int32, eye_c.shape, 1) % c
    diag = (row // GDN_INV_BASE) == (col // GDN_INV_BASE)
    p_cs = [jnp.where(diag, -a_c, 0.0) for a_c in a_cs]
    t_cs = [eye_c + p_c for p_c in p_cs]
    p_cs = [_dot_hp_bd(p_c, p_c, same_block) for p_c in p_cs]
    for _ in range(int(math.log2(GDN_INV_BASE)) - 2):
        xs = [_dot_hp_bd(jnp.concatenate([t_c, p_c], axis=0), p_c, same_block) for t_c, p_c in zip(t_cs, p_cs)]
        t_cs = [t_c + x[0:c] for t_c, x in zip(t_cs, xs)]
        p_cs = [x[c:2 * c] for x in xs]
    t_cs = [t_c + _dot_hp_bd(t_c, p_c, same_block) for t_c, p_c in zip(t_cs, p_cs)]
    b = GDN_INV_BASE
    while b < c:
        off_diag = ((row // (2 * b)) == (col // (2 * b))) & ((row // b) != (col // b))
        xs = [_dot_hp_bd(jnp.where(off_diag, a_c, 0.0), t_c, same_block) for a_c, t_c in zip(a_cs, t_cs)]
        t_cs = [t_c - _dot_hp_bd(t_c, x, same_block) for t_c, x in zip(t_cs, xs)]
        b *= 2
    return t_cs


def _gdn_scores(q, k, gcb, gcrow, beta, masks):
    c = GDN_CHUNK
    same_block, eye_c, incl_c, strict_c = masks
    kb = [k * beta[d] for d in range(2)]
    prod = _dot_nt(jnp.concatenate([kb[0], kb[1], q], axis=0).astype(BF16), k.astype(BF16))
    n = q.shape[0]
    qk_c = _compact_select(prod[2 * n:3 * n], c)
    low_half = lax.broadcasted_iota(jnp.int32, (c, LANES), 1) < c
    a_cs, qk_cs = [], []
    for d in range(2):
        gi_c = jnp.concatenate(
            [jnp.where(low_half, gcb[d][(2 * h) * c:(2 * h + 1) * c], gcb[d][(2 * h + 1) * c:(2 * h + 2) * c])
             for h in range(GDN_GROUP // 2)], axis=1)
        diff = gi_c - gcrow[d]
        decay = jnp.where(incl_c[d], jnp.exp(jnp.where(incl_c[d], diff, 0.0)), 0.0)
        a_cs.append(jnp.where(strict_c[d], _compact_select(prod[d * n:(d + 1) * n], c) * decay, 0.0))
        qk_cs.append(qk_c * decay)
    return a_cs, qk_cs


def _gdn_wy(q, k, v, gcb, beta, gl, t_c, same_block):
    eg = jnp.exp(gcb)
    rhs = jnp.concatenate([v * beta, k * beta * eg], axis=1).astype(BF16)
    uw = _dot(_block_diag(t_c.astype(BF16), same_block), rhs)
    k_dec_t = (k * jnp.exp(gl - gcb)).T
    return uw[:, 0:GDN_DV], uw[:, GDN_DV:].astype(BF16), (q * eg).astype(BF16), k_dec_t


def _pair_lanes(blocks, p, c):
    lane = None
    out = None
    for h, x in enumerate(blocks):
        half = x[:, (p // 2) * LANES:(p // 2 + 1) * LANES]
        if p % 2 != h:
            half = pltpu.roll(half, c, 1)
        if out is None:
            lane = lax.broadcasted_iota(jnp.int32, half.shape, 1)
            out = half
        else:
            out = jnp.where(lane >= h * c, half, out)
    return out


def _gdn_kernel(*refs, t, with_s0):
    if with_s0:
        (q_ref, k_ref, v_ref, z_ref, ab_ref, abt_ref, cwq_ref, cwk_ref, cwv_ref, cbq_ref, cbk_ref, cbv_ref,
         alog_ref, dtb_ref, alogc_ref, dtbc_ref, ng_ref, s0_ref,
         o_ref, xp, qs, ks, vs, osum, gl_s, gcrow_s, u_s, wq_s, qkkd_s) = refs
    else:
        (q_ref, k_ref, v_ref, z_ref, ab_ref, abt_ref, cwq_ref, cwk_ref, cwv_ref, cbq_ref, cbk_ref, cbv_ref,
         alog_ref, dtb_ref, alogc_ref, dtbc_ref, ng_ref,
         o_ref, st_ref, xp, qs, ks, vs, osum, gl_s, gcrow_s, u_s, wq_s, qkkd_s) = refs
    c = GDN_CHUNK
    n_chunks = t // c
    hp = GDN_HEADS_PER_STEP
    heads = [pl.program_id(1) * hp + hh for hh in range(hp)]
    head_lanes = [pl.ds(hh * LANES, LANES) for hh in range(hp)]

    for hh in range(hp):
        for x_ref, cw_ref, cb_ref, dst, kind in ((q_ref, cwq_ref, cbq_ref, qs, "q"),
                                                 (k_ref, cwk_ref, cbk_ref, ks, "k"),
                                                 (v_ref, cwv_ref, cbv_ref, vs, "v")):
            _fill_padded(xp, x_ref[:, head_lanes[hh]].astype(F32), t)
            y = _silu(_conv_from_padded(xp, t, cw_ref[:, head_lanes[hh]], cb_ref[:, head_lanes[hh]]))
            if kind != "v":
                y = y * lax.rsqrt(jnp.sum(y * y, axis=-1, keepdims=True) + EPS)
            if kind == "q":
                y = y * (GDN_DK ** -0.5)
            dst[hh] = y

    abt_rows = lambda idx: abt_ref[idx, 0]
    slab = GDN_SLAB
    bi = lax.broadcasted_iota(jnp.int32, (slab, slab), 0)
    bj = lax.broadcasted_iota(jnp.int32, (slab, slab), 1)
    same_chunk = (bi // c) == (bj // c)
    tri = [jnp.where(same_chunk & ((bi <= bj) if d == 1 else (bi >= bj)), 1.0, 0.0).astype(BF16) for d in range(2)]
    for hh, head in enumerate(heads):
        for d in range(2):
            a_row = abt_rows(d * GDN_H + head)
            al = alogc_ref[pl.ds(d * GDN_H + head, 1), :]
            db = dtbc_ref[pl.ds(d * GDN_H + head, 1), :]
            g_rows = -jnp.exp(al) * _softplus(a_row + db)
            gcrow_s[hh, d] = _dot_sel_right(g_rows, tri[1 - d])

    osum[...] = jnp.zeros_like(osum)
    ci = lax.broadcasted_iota(jnp.int32, (c, slab), 0)
    cj = lax.broadcasted_iota(jnp.int32, (c, slab), 1) % c
    masks = (same_chunk, jnp.where(ci == cj, 1.0, 0.0), (ci >= cj, ci <= cj), (ci > cj, ci < cj))
    if with_s0:
        s_init = tuple(jnp.concatenate([s0_ref[0, d, hh] for hh in range(hp)], axis=1) for d in range(2))
    else:
        s_init = (jnp.zeros((GDN_DK, hp * GDN_DV), F32), jnp.zeros((GDN_DK, hp * GDN_DV), F32))

    lane = lax.broadcasted_iota(jnp.int32, (slab, LANES), 1)

    def prepare(i, _):
        base = pl.multiple_of(i * slab, slab)
        rows = pl.ds(base, slab)
        ab = ab_ref[rows, :]
        gall = jnp.where(lane < 2 * GDN_H, -jnp.exp(alog_ref[...]) * _softplus(ab + dtb_ref[...]), _sigmoid(ab))
        pick = lambda idx: jnp.broadcast_to(jnp.sum(jnp.where(lane == idx, gall, 0.0), axis=-1, keepdims=True),
                                            (slab, LANES))
        beta = [[pick(2 * GDN_H + d * GDN_H + head) for d in range(2)] for head in heads]
        gcb = [[_chunk_cumsum(pick(d * GDN_H + head), c, d == 1) for d in range(2)] for head in heads]
        a_cs, qk_cs = [], []
        for hh in range(hp):
            a_hh, qk_hh = _gdn_scores(qs[hh, rows, :], ks[hh, rows, :], gcb[hh],
                                      [gcrow_s[hh, d, pl.ds(i, 1), :] for d in range(2)], beta[hh], masks)
            a_cs += a_hh
            qk_cs.append(qk_hh)
        t_cs = _tri_inverse_compact(a_cs, masks[1], same_chunk)
        for d in range(2):
            k_dec_ts = []
            for hh in range(hp):
                totals = [gcb[hh][d][p * c + (0 if d == 1 else c - 1):p * c + (0 if d == 1 else c - 1) + 1]
                          for p in range(GDN_GROUP)]
                for p in range(GDN_GROUP):
                    gl_s[hh, d, pl.ds(i * GDN_GROUP + p, 1), :] = totals[p]
                gl = jnp.concatenate([jnp.broadcast_to(row, (c, LANES)) for row in totals], axis=0)
                u, w, qg, k_dec_t = _gdn_wy(qs[hh, rows, :], ks[hh, rows, :], vs[hh, rows, :], gcb[hh][d],
                                            beta[hh][d], gl, t_cs[hh * 2 + d], same_chunk)
                k_dec_ts.append(k_dec_t)
                u_s[d, rows, head_lanes[hh]] = u
                for p in range(GDN_GROUP):
                    dst = pl.multiple_of(base * 2 + p * 2 * c, 2 * c)
                    wq_s[d, pl.ds(dst, c), head_lanes[hh]] = w[p * c:(p + 1) * c]
                    wq_s[d, pl.ds(dst + c, c), head_lanes[hh]] = qg[p * c:(p + 1) * c]
            for p in range(GDN_GROUP):
                n = i * GDN_GROUP + p
                qkkd_s[d, n, pl.ds(0, c), :] = _pair_lanes([qk_cs[hh][d] for hh in range(hp)], p, c).astype(BF16)
                qkkd_s[d, n, pl.ds(c, GDN_DK), :] = _pair_lanes(k_dec_ts, p, c).astype(BF16)
        return 0

    lax.fori_loop(0, t // slab, prepare, 0)

    sr = lax.broadcasted_iota(jnp.int32, (hp * GDN_DK, hp * GDN_DV), 0) // GDN_DK
    sc = lax.broadcasted_iota(jnp.int32, (hp * GDN_DK, hp * GDN_DV), 1) // GDN_DV
    vr = lax.broadcasted_iota(jnp.int32, (hp * c, hp * GDN_DV), 0) // c
    vc = lax.broadcasted_iota(jnp.int32, (hp * c, hp * GDN_DV), 1) // GDN_DV

    def body(i, carry):
        out = []
        for d in range(2):
            n = (n_chunks - 1 - i) if d == 1 else i
            base = pl.multiple_of(n * c, c)
            rows = pl.ds(base, c)
            s = carry[d]
            s_bd = jnp.where(sr == sc, jnp.concatenate([s] * hp, axis=0), 0.0).astype(BF16)
            ws_qs = _dot(wq_s[d, pl.ds(pl.multiple_of(n * 2 * c, 2 * c), 2 * c), :], s_bd)
            v_new = u_s[d, rows, :] - ws_qs[0:c]
            v_bd = jnp.where(vr == vc, jnp.concatenate([v_new] * hp, axis=0), 0.0).astype(BF16)
            upd = _dot(qkkd_s[d, n], v_bd)
            gl = jnp.concatenate([gl_s[hh, d, pl.ds(n, 1), :] for hh in range(hp)], axis=1)
            out.append(s * jnp.exp(gl) + upd[c:c + GDN_DK])
            osum[rows, :] = osum[rows, :] + ws_qs[c:2 * c] + upd[0:c]
        return tuple(out)

    s_fin = lax.fori_loop(0, n_chunks, body, s_init)
    for hh in range(hp):
        o = osum[:, head_lanes[hh]]
        y = o * lax.rsqrt(jnp.mean(o * o, axis=-1, keepdims=True) + EPS) * ng_ref[...]
        o_ref[:, head_lanes[hh]] = (y * _silu(z_ref[:, head_lanes[hh]].astype(F32))).astype(o_ref.dtype)
        if not with_s0:
            for d in range(2):
                st_ref[0, d, hh] = s_fin[d][:, hh * GDN_DV:(hh + 1) * GDN_DV]


def _dot_sel_right(x, sel_bf16):
    hi, mid, lo = _split3(x)
    return _dot(hi, sel_bf16) + _dot(mid, sel_bf16) + _dot(lo, sel_bf16)


def _gdn_call(st, qkv, z, ab, abt, p, s0=None):
    c = GDN_SLAB
    t, nseq = st.t, st.nseq
    n_chunks = t // c
    n_pad = max(n_chunks, SUBLANES)
    abt = abt.reshape(4 * GDN_H, nseq, n_chunks, c)
    if n_pad != n_chunks:
        abt = jnp.pad(abt, ((0, 0), (0, 0), (0, n_pad - n_chunks), (0, 0)))
    hp = GDN_HEADS_PER_STEP
    wide = hp * LANES
    nqk = GDN_H // hp
    in_specs = [pl.BlockSpec((t, wide), lambda s, h: (s, h)),
                pl.BlockSpec((t, wide), lambda s, h: (s, nqk + h)),
                pl.BlockSpec((t, wide), lambda s, h: (s, 2 * nqk + h)),
                pl.BlockSpec((t, wide), lambda s, h: (s, h)),
                pl.BlockSpec((t, LANES), lambda s, h: (s, 0)),
                pl.BlockSpec((4 * GDN_H, 1, n_pad, c), lambda s, h: (0, s, 0, 0)),
                pl.BlockSpec((CONV_W, wide), lambda s, h: (0, h)),
                pl.BlockSpec((CONV_W, wide), lambda s, h: (0, nqk + h)),
                pl.BlockSpec((CONV_W, wide), lambda s, h: (0, 2 * nqk + h)),
                pl.BlockSpec((1, wide), lambda s, h: (0, h)),
                pl.BlockSpec((1, wide), lambda s, h: (0, nqk + h)),
                pl.BlockSpec((1, wide), lambda s, h: (0, 2 * nqk + h)),
                pl.BlockSpec((1, LANES), lambda s, h: (0, 0)),
                pl.BlockSpec((1, LANES), lambda s, h: (0, 0)),
                pl.BlockSpec((4 * GDN_H, 1), lambda s, h: (0, 0)),
                pl.BlockSpec((4 * GDN_H, 1), lambda s, h: (0, 0)),
                pl.BlockSpec((1, LANES), lambda s, h: (0, 0))]
    args = [qkv, qkv, qkv, z, ab, abt, p["conv_w"], p["conv_w"], p["conv_w"], p["conv_b"], p["conv_b"], p["conv_b"],
            p["alog_row"], p["dtb_row"], p["alog_col"], p["dtb_col"], p["norm_g"]]
    scratch = [pltpu.VMEM((t + 2 * SUBLANES, LANES), F32)] + [pltpu.VMEM((hp, t, LANES), F32)] * 3 + [
        pltpu.VMEM((t, wide), F32),
        pltpu.VMEM((hp, 2, max(t // GDN_CHUNK, SUBLANES), LANES), F32), pltpu.VMEM((hp, 2, n_pad, c), F32),
        pltpu.VMEM((2, t, wide), F32), pltpu.VMEM((2, 2 * t, wide), BF16),
        pltpu.VMEM((2, t // GDN_CHUNK, GDN_CHUNK + GDN_DK, hp * GDN_CHUNK), BF16)]
    mix_shape = jax.ShapeDtypeStruct((st.n_tok, GDN_H * GDN_DV), BF16)
    out_mix_spec = pl.BlockSpec((t, wide), lambda s, h: (s, h))
    st_spec = pl.BlockSpec((1, 2, hp, GDN_DK, GDN_DV), lambda s, h: (s, 0, h, 0, 0))
    if s0 is not None:
        in_specs.append(st_spec)
        args.append(s0)
        return pl.pallas_call(
            functools.partial(_gdn_kernel, t=t, with_s0=True),
            grid=(nseq, nqk), in_specs=in_specs, out_specs=out_mix_spec, out_shape=mix_shape,
            scratch_shapes=scratch,
            compiler_params=_params("arbitrary", "arbitrary"),
        )(*args)
    return pl.pallas_call(
        functools.partial(_gdn_kernel, t=t, with_s0=False),
        grid=(nseq, nqk), in_specs=in_specs,
        out_specs=[out_mix_spec, st_spec],
        out_shape=[mix_shape, jax.ShapeDtypeStruct((nseq, 2, GDN_H, GDN_DK, GDN_DV), F32)],
        scratch_shapes=scratch,
        compiler_params=_params("arbitrary", "arbitrary"),
    )(*args)


def _gla_kernel(*refs, t, with_s0):
    if with_s0:
        (q_ref, k_ref, v_ref, z_ref, lr_ref, w2_ref, b2_ref, ng_ref, s0_ref,
         o_ref, osum, gc_s) = refs
    else:
        (q_ref, k_ref, v_ref, z_ref, lr_ref, w2_ref, b2_ref, ng_ref,
         o_ref, st_ref, osum, gc_s) = refs
    c = GLA_CHUNK
    n_chunks = t // c
    slab = 256
    bi = lax.broadcasted_iota(jnp.int32, (slab, slab), 0)
    bj = lax.broadcasted_iota(jnp.int32, (slab, slab), 1)
    same_chunk = (bi // c) == (bj // c)
    lr = lr_ref[...]
    for d in range(2):
        pre = _dot_hp(lr, w2_ref[d, 0]) + b2_ref[d]
        glog = (jnp.minimum(pre, 0.0) - jnp.log1p(jnp.exp(-jnp.abs(pre)))) * (1.0 / GLA_TAU)
        gc_s[d] = _chunk_cumsum(glog, c, d == 1)

    osum[...] = jnp.zeros_like(osum)
    n_slabs = t // slab
    per_slab = slab // c
    causal = [same_chunk & ((bi <= bj) if d == 1 else (bi >= bj)) for d in range(2)]
    own_block = (lax.broadcasted_iota(jnp.int32, (slab, per_slab * GLA_DK), 0) // c
                 == lax.broadcasted_iota(jnp.int32, (slab, per_slab * GLA_DK), 1) // GLA_DK)
    if with_s0:
        s_init = (s0_ref[0, 0, 0].T, s0_ref[0, 1, 0].T)
    else:
        s_init = (jnp.zeros((GLA_DV, GLA_DK), F32), jnp.zeros((GLA_DV, GLA_DK), F32))

    def spread(x):
        return jnp.where(own_block, jnp.concatenate([x] * per_slab, axis=1), jnp.zeros((), x.dtype))

    def chunk_row(d, row):
        return gc_s[d, pl.ds(row, 1), :]

    def per_chunk_rows(d, base, offset):
        return jnp.concatenate([jnp.broadcast_to(chunk_row(d, base + p * c + offset), (c, GLA_DK))
                                for p in range(per_slab)], axis=0)

    def body(i, carry):
        stage = []
        for d in range(2):
            rev = d == 1
            base = pl.multiple_of(((n_slabs - 1 - i) if rev else i) * slab, slab)
            rows = pl.ds(base, slab)
            q = q_ref[rows, :].astype(F32) * (GLA_DK ** -0.5)
            k = k_ref[rows, :].astype(F32)
            v = v_ref[rows, :]
            gc = gc_s[d, rows, :]
            gm = per_chunk_rows(d, base, c // 2)
            gl = per_chunk_rows(d, base, 0 if rev else c - 1)
            scores = _dot_nt((q * jnp.exp(gc - gm)).astype(BF16), (k * jnp.exp(gm - gc)).astype(BF16))
            a_mat = jnp.where(causal[d], scores, 0.0).astype(BF16)
            ds_cat = _dot_tn(v, spread((k * jnp.exp(gl - gc)).astype(BF16)))
            stage.append((base, rows, v, a_mat, ds_cat, spread((q * jnp.exp(gc)).astype(BF16))))
        out = []
        for d in range(2):
            rev = d == 1
            base, rows, v, a_mat, ds_cat, qg_spread = stage[d]
            st = carry[d]
            prev = [None] * per_slab
            for p in (range(per_slab - 1, -1, -1) if rev else range(per_slab)):
                prev[p] = st.astype(BF16)
                gl_p = chunk_row(d, base + p * c + (0 if rev else c - 1))
                st = st * jnp.exp(gl_p) + ds_cat[:, p * GLA_DK:(p + 1) * GLA_DK]
            o = _dot(a_mat, v) + _dot_nt(qg_spread, jnp.concatenate(prev, axis=1))
            osum[rows, :] = osum[rows, :] + o
            out.append(st)
        return tuple(out)

    s_f, s_b = lax.fori_loop(0, n_slabs, body, s_init)
    o = osum[...]
    y = o * lax.rsqrt(jnp.mean(o * o, axis=-1, keepdims=True) + EPS) * ng_ref[...]
    o_ref[...] = (y * _silu(z_ref[...].astype(F32))).astype(o_ref.dtype)
    if not with_s0:
        st_ref[0, 0, 0] = s_f.T
        st_ref[0, 1, 0] = s_b.T


def _gla_call(st, q, k, v, z, lr, p, s0=None):
    t, nseq = st.t, st.nseq
    in_specs = [pl.BlockSpec((t, GLA_DK), lambda s, h: (s, h)),
                pl.BlockSpec((t, GLA_DK), lambda s, h: (s, h)),
                pl.BlockSpec((t, GLA_DV), lambda s, h: (s, h)),
                pl.BlockSpec((t, GLA_DV), lambda s, h: (s, h)),
                pl.BlockSpec((t, LANES), lambda s, h: (s, 0)),
                pl.BlockSpec((2, 1, LANES, GLA_DK), lambda s, h: (0, h, 0, 0)),
                pl.BlockSpec((2, 1, GLA_DK), lambda s, h: (0, 0, h)),
                pl.BlockSpec((1, GLA_DV), lambda s, h: (0, 0))]
    args = [q, k, v, z, lr, p["w2"], p["b2"], p["norm_g"]]
    scratch = [pltpu.VMEM((t, GLA_DV), F32), pltpu.VMEM((2, t, GLA_DK), F32)]
    mix_shape = jax.ShapeDtypeStruct((st.n_tok, GLA_H * GLA_DV), BF16)
    out_mix_spec = pl.BlockSpec((t, GLA_DV), lambda s, h: (s, h))
    st_spec = pl.BlockSpec((1, 2, 1, GLA_DK, GLA_DV), lambda s, h: (s, 0, h, 0, 0))
    if s0 is not None:
        in_specs.append(st_spec)
        args.append(s0)
        return pl.pallas_call(
            functools.partial(_gla_kernel, t=t, with_s0=True),
            grid=(nseq, GLA_H), in_specs=in_specs, out_specs=out_mix_spec, out_shape=mix_shape,
            scratch_shapes=scratch,
            compiler_params=_params("arbitrary", "arbitrary"),
        )(*args)
    return pl.pallas_call(
        functools.partial(_gla_kernel, t=t, with_s0=False),
        grid=(nseq, GLA_H), in_specs=in_specs,
        out_specs=[out_mix_spec, st_spec],
        out_shape=[mix_shape, jax.ShapeDtypeStruct((nseq, 2, GLA_H, GLA_DK, GLA_DV), F32)],
        scratch_shapes=scratch,
        compiler_params=_params("arbitrary", "arbitrary"),
    )(*args)


def _route(sel, s):
    scores = []
    for g in range(N_GROUPS):
        m = sel[g * GROUP_SZ:(g + 1) * GROUP_SZ]
        best = None
        for a in range(GROUP_SZ):
            for b in range(a + 1, GROUP_SZ):
                pair = m[a] + m[b]
                best = pair if best is None else jnp.maximum(best, pair)
        scores.append(best)
    gbest = jnp.zeros_like(scores[0], dtype=jnp.int32)
    top = scores[0]
    for g in range(1, N_GROUPS):
        better = scores[g] > top
        gbest = jnp.where(better, g, gbest)
        top = jnp.where(better, scores[g], top)
    picked = []
    for e in range(N_EXPERTS):
        g = e // GROUP_SZ
        rank = jnp.zeros_like(gbest)
        for m in range(g * GROUP_SZ, (g + 1) * GROUP_SZ):
            if m == e:
                continue
            ahead = (sel[m] > sel[e]) | (sel[m] == sel[e]) if m < e else (sel[m] > sel[e])
            rank = rank + ahead.astype(jnp.int32)
        picked.append(jnp.where((gbest == g) & (rank < 2), s[e], 0.0))
    total = picked[0]
    for e in range(1, N_EXPERTS):
        total = total + picked[e]
    return [p / total for p in picked]


def _outproj_kernel(*refs, n_mix, from_col_major):
    mix_refs = refs[:n_mix]
    w_refs = refs[n_mix:2 * n_mix]
    x_ref, mod_ref, g_ref, rwt_ref, rb_ref, x1_ref, h2_ref, gates_ref = refs[2 * n_mix:]
    gate1 = mod_ref[0, :, 2 * D:3 * D]
    delta = None
    for mix_ref, w_ref in zip(mix_refs, w_refs):
        mix = mix_ref[...]
        if from_col_major:
            mix = _dot(_grid_perm(False), mix).astype(BF16)
        part = _dot(mix, w_ref[...])
        delta = part if delta is None else delta + part
    x = x_ref[0].reshape(TM, D) if from_col_major else x_ref[...]
    x1 = x + gate1 * delta
    x1_ref[...] = x1
    y = x1 * lax.rsqrt(jnp.mean(x1 * x1, axis=-1, keepdims=True) + EPS) * g_ref[...]
    h2 = y * (1.0 + mod_ref[0, :, 4 * D:5 * D]) + mod_ref[0, :, 3 * D:4 * D]
    h2_ref[...] = h2.astype(BF16)
    rw_hi, rw_lo = _split2(rwt_ref[...])
    h2_hi, h2_lo = _split2(h2)
    both = _dot_nt(jnp.concatenate([rw_hi, rw_lo], axis=0), h2_hi)
    logits = both[0:N_EXPERTS] + both[N_EXPERTS:2 * N_EXPERTS] + _dot_nt(rw_hi, h2_lo)
    s_all = _sigmoid(logits)
    sel_all = s_all + rb_ref[...]
    s = [s_all[e:e + 1, :] for e in range(N_EXPERTS)]
    sel = [sel_all[e:e + 1, :] for e in range(N_EXPERTS)]
    gate_rows = _route(sel, s) + [jnp.zeros((LANES - N_EXPERTS, TM), F32)]
    gates_ref[...] = jnp.concatenate(gate_rows, axis=0).T


def _outproj(st, mixes, w_outs, x, mod_l, gain2, router_wt, router_b, from_col_major=False):
    n_mix = len(mixes)
    in_specs = [pl.BlockSpec((TM, m.shape[1]), lambda i: (i, 0)) for m in mixes]
    in_specs += [pl.BlockSpec(w.shape, lambda i: (0, 0)) for w in w_outs]
    in_specs += [_grid_tile_spec() if from_col_major else pl.BlockSpec((TM, D), lambda i: (i, 0)),
                 pl.BlockSpec((1, 1, 6 * D), lambda i: (st.mod_row(i, TM), 0, 0)),
                 pl.BlockSpec((1, D), lambda i: (0, 0)),
                 pl.BlockSpec((N_EXPERTS, D), lambda i: (0, 0)),
                 pl.BlockSpec((N_EXPERTS, 1), lambda i: (0, 0))]
    return pl.pallas_call(
        functools.partial(_outproj_kernel, n_mix=n_mix, from_col_major=from_col_major),
        grid=(st.n_tok // TM,),
        in_specs=in_specs,
        out_specs=[pl.BlockSpec((TM, D), lambda i: (i, 0)),
                   pl.BlockSpec((TM, D), lambda i: (i, 0)),
                   pl.BlockSpec((TM, LANES), lambda i: (i, 0))],
        out_shape=[jax.ShapeDtypeStruct((st.n_tok, D), F32),
                   jax.ShapeDtypeStruct((st.n_tok, D), BF16),
                   jax.ShapeDtypeStruct((st.n_tok, LANES), F32)],
        compiler_params=_params("arbitrary"),
    )(*mixes, *w_outs, x, mod_l.reshape(SUBLANES, 1, 6 * D), gain2.reshape(1, D), router_wt,
      router_b.reshape(N_EXPERTS, 1))


def _moe_kernel(h_ref, gates_ref, x1_ref, mod_ref, wg_ref, wu_ref, wd_ref, fg_ref, o_ref,
                xs, acc, gs, pos_s, meta_s, *, final_norm, to_grid):
    e = pl.program_id(1)
    tm = TM_MOE

    @pl.when(e == 0)
    def _():
        gates = gates_ref[...]
        ei = lax.broadcasted_iota(jnp.int32, (LANES, LANES), 0)
        gj = lax.broadcasted_iota(jnp.int32, (LANES, LANES), 1)
        member = jnp.where((ei < N_EXPERTS) & (ei // GROUP_SZ == gj), 1.0, 0.0).astype(BF16)
        onehot = jnp.where(_dot(gates.astype(BF16), member) > 0.0, 1.0, 0.0)
        ti = lax.broadcasted_iota(jnp.int32, (tm, tm), 0)
        tj = lax.broadcasted_iota(jnp.int32, (tm, tm), 1)
        rank = _dot(jnp.where(ti >= tj, 1.0, 0.0).astype(BF16), onehot.astype(BF16))
        count = rank[tm - 1:tm, :]
        blocks = jnp.floor((count + (MOE_SEG - 1.0)) * (1.0 / MOE_SEG))
        before = jnp.where(ei < gj, 1.0, 0.0).astype(BF16)
        base = _dot(jnp.broadcast_to(blocks, (SUBLANES, LANES)).astype(BF16), before)[0:1] * MOE_SEG
        pos = jnp.sum(onehot * (base + rank - 1.0), axis=-1, keepdims=True)
        pos_s[...] = jnp.broadcast_to(pos, (tm, LANES))
        meta_s[...] = jnp.concatenate([base, blocks, jnp.zeros((SUBLANES - 2, LANES), F32)], axis=0)
        pos_row = pos_s[...].T[0:1, :]
        ri = lax.broadcasted_iota(jnp.int32, (MOE_ROWS, tm), 0).astype(F32)
        perm = jnp.where(ri == pos_row, 1.0, 0.0).astype(BF16)
        xs[...] = _dot(perm, h_ref[...]).astype(BF16)
        g_hi, g_lo = _split2(gates)
        gs[...] = _dot(perm, g_hi) + _dot(perm, g_lo)
        acc[...] = jnp.zeros_like(acc)

    first = e * MOE_EXPERTS_PER_STEP
    group = first // GROUP_SZ
    meta = meta_s[...]
    m_row = lax.broadcasted_iota(jnp.int32, meta.shape, 0)
    m_lane = lax.broadcasted_iota(jnp.int32, meta.shape, 1)
    base = jnp.sum(jnp.where((m_row == 0) & (m_lane == group), meta, 0.0)).astype(jnp.int32)
    n_blocks = jnp.sum(jnp.where((m_row == 1) & (m_lane == group), meta, 0.0)).astype(jnp.int32)
    lane = lax.broadcasted_iota(jnp.int32, (MOE_SEG, LANES), 1)

    def block(b, _):
        rows = pl.ds(pl.multiple_of(base + b * MOE_SEG, MOE_SEG), MOE_SEG)
        x = xs[rows, :]
        g_rows = gs[rows, :]
        total = None
        for k in range(MOE_EXPERTS_PER_STEP):
            gate = jnp.sum(jnp.where(lane == first + k, g_rows, 0.0), axis=-1, keepdims=True)
            hid = _silu(_dot(x, wg_ref[0, k])) * _dot(x, wu_ref[0, k]) * gate
            part = _dot(hid.astype(BF16), wd_ref[0, k])
            total = part if total is None else total + part
        acc[rows, :] += total
        return 0

    lax.fori_loop(0, n_blocks, block, 0)

    @pl.when(e == N_EXPERTS // MOE_EXPERTS_PER_STEP - 1)
    def _():
        li = lax.broadcasted_iota(jnp.int32, (tm, MOE_ROWS), 1).astype(F32)
        unperm = jnp.where(li == pos_s[:, 0:1], 1.0, 0.0).astype(BF16)
        moe = _dot(unperm, acc[...].astype(BF16))
        out = x1_ref[...] + mod_ref[0, :, 5 * D:6 * D] * moe
        if final_norm:
            out = out * lax.rsqrt(jnp.mean(out * out, axis=-1, keepdims=True) + EPS) * fg_ref[...]
        if to_grid:
            for j in range(TM_MOE // TM):
                o_ref[0, :, j * COLS_PER_TILE:(j + 1) * COLS_PER_TILE, :] = (
                    out[j * TM:(j + 1) * TM].reshape(GRID_H, COLS_PER_TILE, D))
        else:
            o_ref[...] = out


def _moe(st, h2, gates, x1, mod_l, w_gate, w_up, w_down, layer, final_gain, final_norm, to_grid=False):
    tm = TM_MOE
    if to_grid:
        tiles_per_seq = st.t // tm
        cols = tm // GRID_H
        out_spec = pl.BlockSpec((1, GRID_H, cols, D), lambda i, e: (i // tiles_per_seq, 0, i % tiles_per_seq, 0))
        out_shape = jax.ShapeDtypeStruct((st.nseq, GRID_H, GRID_W, D), F32)
    else:
        out_spec = pl.BlockSpec((tm, D), lambda i, e: (i, 0))
        out_shape = jax.ShapeDtypeStruct((st.n_tok, D), F32)
    return pl.pallas_call(
        functools.partial(_moe_kernel, final_norm=final_norm, to_grid=to_grid),
        grid=(st.n_tok // tm, N_EXPERTS // MOE_EXPERTS_PER_STEP),
        in_specs=[pl.BlockSpec((tm, D), lambda i, e: (i, 0)),
                  pl.BlockSpec((tm, LANES), lambda i, e: (i, 0)),
                  pl.BlockSpec((tm, D), lambda i, e: (i, 0)),
                  pl.BlockSpec((1, 1, 6 * D), lambda i, e: (st.mod_row(i, tm), 0, 0)),
                  pl.BlockSpec((1, MOE_EXPERTS_PER_STEP, D, D_EXPERT), lambda i, e: (layer, e, 0, 0)),
                  pl.BlockSpec((1, MOE_EXPERTS_PER_STEP, D, D_EXPERT), lambda i, e: (layer, e, 0, 0)),
                  pl.BlockSpec((1, MOE_EXPERTS_PER_STEP, D_EXPERT, D), lambda i, e: (layer, e, 0, 0)),
                  pl.BlockSpec((1, D), lambda i, e: (0, 0))],
        out_specs=out_spec, out_shape=out_shape,
        scratch_shapes=[pltpu.VMEM((MOE_ROWS, D), BF16), pltpu.VMEM((MOE_ROWS, D), F32),
                        pltpu.VMEM((MOE_ROWS, LANES), F32), pltpu.VMEM((tm, LANES), F32),
                        pltpu.VMEM((SUBLANES, LANES), F32)],
        compiler_params=_params("arbitrary", "arbitrary"),
    )(h2, gates, x1, mod_l.reshape(SUBLANES, 1, 6 * D), w_gate, w_up, w_down, final_gain.reshape(1, D))


def _layer_ab(xs, mod_l, state_lru, state_gdn, norm1_g, w_in, lru_conv_w, lru_conv_b, lru_wa, lru_ba, lru_wx,
              lru_bx, lru_lam, gdn_conv_w, gdn_conv_b, gdn_a_log, gdn_dt_bias, gdn_norm_g):
    w = w_in.astype(BF16)
    n_qkv = 2 * GDN_H * GDN_DK + GDN_H * GDN_DV
    o = 0
    w_y, o = w[:, o:o + D], o + D
    w_x, o = w[:, o:o + D], o + D
    w_qkv, o = w[:, o:o + n_qkv], o + n_qkv
    w_z, o = w[:, o:o + GDN_H * GDN_DV], o + GDN_H * GDN_DV
    w_ab = w[:, o:]
    w_ab_pad = jnp.pad(w_ab, ((0, 0), (0, LANES - w_ab.shape[1])))
    proj = [_inproj(st, x, mod_l, norm1_g, [w_y, w_x, w_qkv, w_z, w_ab_pad], [BF16, BF16, BF16, BF16, F32],
                    w_t=w_ab.T) for st, x in zip((PROMPT, SAMPLE), xs)]

    nb = D // LRU_BLOCK
    wg = jnp.stack([lru_wa[0], lru_wx[0], lru_wa[1], lru_wx[1]], axis=1)
    wg = wg.transpose(0, 2, 1, 3).reshape(nb, LRU_BLOCK, 4 * LRU_BLOCK).astype(BF16)
    bg = jnp.stack([lru_ba[0], lru_bx[0], lru_ba[1], lru_bx[1]], axis=0)
    bg = bg.reshape(4, nb, LRU_BLOCK).transpose(1, 0, 2).reshape(nb, 1, 4 * LRU_BLOCK)
    lam = lru_lam.reshape(2, nb, LRU_BLOCK).transpose(1, 0, 2)
    lru_p = dict(conv_w=lru_conv_w, conv_b=lru_conv_b.reshape(1, D), wg=wg, bg=bg, lam=lam)
    (ybr_p, xbr_p, qkv_p, z_p, ab_p, abt_p), (ybr_s, xbr_s, qkv_s, z_s, ab_s, abt_s) = proj
    lru_p_out, lru_state = _lru_call(PROMPT, xbr_p, ybr_p, lru_p)
    lru_s_out = _lru_call(SAMPLE, xbr_s, ybr_s, lru_p, h0=state_lru)

    pad16 = lambda v: jnp.pad(v.reshape(1, 2 * GDN_H), ((0, 0), (0, LANES - 2 * GDN_H)))
    col32 = lambda v: jnp.pad(v.reshape(2 * GDN_H, 1), ((0, 2 * GDN_H), (0, 0)))
    gdn_p = dict(conv_w=gdn_conv_w, conv_b=gdn_conv_b.reshape(1, n_qkv), alog_row=pad16(gdn_a_log),
                 dtb_row=pad16(gdn_dt_bias), alog_col=col32(gdn_a_log), dtb_col=col32(gdn_dt_bias),
                 norm_g=gdn_norm_g.reshape(1, GDN_DV))
    gdn_p_out, gdn_state = _gdn_call(PROMPT, qkv_p, z_p, ab_p, abt_p, gdn_p)
    gdn_s_out = _gdn_call(SAMPLE, qkv_s, z_s, ab_s, abt_s, gdn_p, s0=state_gdn)
    return [lru_p_out, gdn_p_out], [lru_s_out, gdn_s_out], lru_state, gdn_state


def _layer_c(xs, mod_l, state_gla, norm1_g, w_in, w2, b2, norm_g):
    w = w_in.astype(BF16)
    nk = GLA_H * GLA_DK
    nv = GLA_H * GLA_DV
    w_q, w_k, w_v, w_z = w[:, 0:nk], w[:, nk:2 * nk], w[:, 2 * nk:2 * nk + nv], w[:, 2 * nk + nv:2 * nk + 2 * nv]
    w_lr = jnp.pad(w[:, 2 * nk + 2 * nv:], ((0, 0), (0, LANES - 2 * GLA_RANK)))
    weights, dtypes = [w_q, w_k, w_v, w_z, w_lr], [BF16, BF16, BF16, BF16, F32]
    proj_p = _inproj(PROMPT, xs[0], mod_l, norm1_g, weights, dtypes)
    proj_s = _inproj(SAMPLE, xs[1], mod_l, norm1_g, weights, dtypes, to_col_major=True)
    w2h = w2.reshape(2, GLA_RANK, GLA_H, GLA_DK).transpose(0, 2, 1, 3)
    w2big = jnp.zeros((2, GLA_H, LANES, GLA_DK), F32)
    w2big = w2big.at[0, :, 0:GLA_RANK].set(w2h[0]).at[1, :, GLA_RANK:2 * GLA_RANK].set(w2h[1])
    gla_p = dict(w2=w2big, b2=b2.reshape(2, 1, nk), norm_g=norm_g.reshape(1, GLA_DV))
    gla_p_out, gla_state = _gla_call(PROMPT, *proj_p, gla_p)
    gla_s_out = _gla_call(SAMPLE, *proj_s, gla_p, s0=state_gla)
    return gla_p_out, gla_s_out, gla_state


def kernel(x_prompt, x_sample, state_lru, state_gdn, state_gla, c, c_ctx, ada_w, ada_b, norm1_g, norm2_g,
           final_norm_g, ab_w_in, lru_conv_w, lru_conv_b, lru_wa, lru_ba, lru_wx, lru_bx, lru_lam, gdn_conv_w,
           gdn_conv_b, gdn_a_log, gdn_dt_bias, gdn_norm_g, ab_w_out, gla_w_in, gla_w2, gla_b2, gla_norm_g,
           gla_w_out, router_w, router_b, moe_w_gate, moe_w_up, moe_w_down):
    xp = x_prompt.reshape(N_PROMPT, D)
    xs = x_sample.reshape(N_SAMPLE, D)
    cvec = jnp.concatenate([c_ctx.reshape(1, D), c, jnp.zeros((SUBLANES - 1 - N_SAMPLE_SEQ, D), F32)], axis=0)
    mod = _modulation(cvec, ada_w, ada_b)
    router_wt = router_w.T
    moe_w = (moe_w_gate.astype(BF16), moe_w_up.astype(BF16), moe_w_down.astype(BF16))

    mix_p, mix_s, lru_state, gdn_state = _layer_ab(
        (xp, xs), mod[0], state_lru[:, 0], state_gdn[:, 0], norm1_g[0], ab_w_in[0], lru_conv_w[0], lru_conv_b[0],
        lru_wa[0], lru_ba[0], lru_wx[0], lru_bx[0], lru_lam[0], gdn_conv_w[0], gdn_conv_b[0], gdn_a_log[0],
        gdn_dt_bias[0], gdn_norm_g[0])
    w_out = ab_w_out[0].astype(BF16)
    w_outs = [w_out[:D], w_out[D:]]
    streams = []
    for st, mixes, x in ((PROMPT, mix_p, xp), (SAMPLE, mix_s, xs)):
        x1, h2, gates = _outproj(st, mixes, w_outs, x, mod[0], norm2_g[0], router_wt, router_b)
        streams.append(_moe(st, h2, gates, x1, mod[0], *moe_w, 0, final_norm_g, False))
    xp, xs = streams
    xs_grid = xs.reshape(N_SAMPLE_SEQ, GRID_H, GRID_W, D)

    gla_p_out, gla_s_out, gla_state = _layer_c((xp, xs_grid), mod[1], state_gla[:, 0], norm1_g[1], gla_w_in[0],
                                               gla_w2[0], gla_b2[0], gla_norm_g[0])
    w_outs = [gla_w_out[0].astype(BF16)]
    x1, h2, gates = _outproj(PROMPT, [gla_p_out], w_outs, xp, mod[1], norm2_g[1], router_wt, router_b)
    y_prompt = _moe(PROMPT, h2, gates, x1, mod[1], *moe_w, 1, final_norm_g, True)
    x1, h2, gates = _outproj(SAMPLE, [gla_s_out], w_outs, xs_grid, mod[1], norm2_g[1], router_wt, router_b,
                             from_col_major=True)
    y_sample = _moe(SAMPLE, h2, gates, x1, mod[1], *moe_w, 1, final_norm_g, True, to_grid=True)

    return (y_prompt.reshape(N_PROMPT_SEQ, T_PROMPT, D), y_sample.reshape(N_SAMPLE_SEQ, T_SAMPLE, D),
            lru_state[:, None], gdn_state[:, None], gla_state[:, None])
```

```python
import functools
import math
from typing import NamedTuple

import jax
import jax.numpy as jnp
from jax import lax
from jax.experimental import pallas as pl
from jax.experimental.pallas import tpu as pltpu

F32 = jnp.float32
BF16 = jnp.bfloat16

D = 1024
N_PROMPT_SEQ, T_PROMPT = 16, 256
N_SAMPLE_SEQ, T_SAMPLE = 4, 2048
N_PROMPT = N_PROMPT_SEQ * T_PROMPT
N_SAMPLE = N_SAMPLE_SEQ * T_SAMPLE
GRID_W = 64
GRID_H = T_SAMPLE // GRID_W
EPS = 1e-6
LANES = 128
SUBLANES = 8

LRU_C = 8.0
LRU_BLOCK = 128
CONV_LEFT = 2
CONV_W = 4
GDN_H, GDN_DK, GDN_DV, GDN_CHUNK = 8, 128, 128, 64
GLA_H, GLA_DK, GLA_DV, GLA_CHUNK, GLA_RANK, GLA_TAU = 4, 128, 256, 32, 16, 16.0
N_EXPERTS, N_GROUPS, D_EXPERT = 16, 4, 512
GROUP_SZ = N_EXPERTS // N_GROUPS

TM = 256
TM_MOE = 1024
MOE_EXPERTS_PER_STEP = 2
MOE_SEG = 128
MOE_ROWS = TM_MOE + (N_GROUPS - 1) * MOE_SEG
VMEM_LIMIT = 60 * 1024 * 1024

_NT = (((1,), (1,)), ((), ()))
_TN = (((0,), (0,)), ((), ()))


def _dot(a, b):
    return jnp.dot(a, b, preferred_element_type=F32)


def _dot_nt(a, b):
    return lax.dot_general(a, b, _NT, preferred_element_type=F32)


def _dot_tn(a, b):
    return lax.dot_general(a, b, _TN, preferred_element_type=F32)


def _split2(x):
    hi = x.astype(BF16)
    lo = (x - hi.astype(F32)).astype(BF16)
    return hi, lo


def _split3(x):
    hi = x.astype(BF16)
    r = x - hi.astype(F32)
    mid = r.astype(BF16)
    lo = (r - mid.astype(F32)).astype(BF16)
    return hi, mid, lo


def _dot_hp(a, b, dot=_dot):
    ah, al = _split2(a)
    bh, bl = _split2(b)
    return dot(ah, bh) + dot(ah, bl) + dot(al, bh)


def _sigmoid(x):
    return 1.0 / (1.0 + jnp.exp(-x))


def _silu(x):
    return x * _sigmoid(x)


def _softplus(x):
    return jnp.maximum(x, 0.0) + jnp.log1p(jnp.exp(-jnp.abs(x)))


def _gelu_tanh(x):
    return 0.5 * x * (1.0 + jnp.tanh(math.sqrt(2.0 / math.pi) * (x + 0.044715 * (x * x * x))))


def _params(*sem):
    return pltpu.CompilerParams(dimension_semantics=sem, vmem_limit_bytes=VMEM_LIMIT)


class _Stream(NamedTuple):
    n_tok: int
    t: int
    nseq: int
    per_request_mod: bool

    def mod_row(self, i, tm):
        return 1 + i // (self.t // tm) if self.per_request_mod else 0


PROMPT = _Stream(N_PROMPT, T_PROMPT, N_PROMPT_SEQ, False)
SAMPLE = _Stream(N_SAMPLE, T_SAMPLE, N_SAMPLE_SEQ, True)
COLS_PER_TILE = TM // GRID_H


def _grid_perm(to_col_major):
    i = lax.broadcasted_iota(jnp.int32, (TM, TM), 0)
    j = lax.broadcasted_iota(jnp.int32, (TM, TM), 1)
    if to_col_major:
        src = (i % GRID_H) * COLS_PER_TILE + i // GRID_H
    else:
        src = (i % COLS_PER_TILE) * GRID_H + i // COLS_PER_TILE
    return jnp.where(j == src, 1.0, 0.0).astype(BF16)


def _mod_kernel(c_ref, w_ref, b_ref, o_ref):
    c = c_ref[...]
    o_ref[0] = _dot_hp(_silu(c), w_ref[0]) + b_ref[0]


def _modulation(cvec, ada_w, ada_b):
    depth = ada_w.shape[0]
    n6 = ada_w.shape[2]
    return pl.pallas_call(
        _mod_kernel,
        grid=(depth, n6 // D),
        in_specs=[pl.BlockSpec((SUBLANES, D), lambda l, j: (0, 0)),
                  pl.BlockSpec((1, D, D), lambda l, j: (l, 0, j)),
                  pl.BlockSpec((1, 1, D), lambda l, j: (l, 0, j))],
        out_specs=pl.BlockSpec((1, SUBLANES, D), lambda l, j: (l, 0, j)),
        out_shape=jax.ShapeDtypeStruct((depth, SUBLANES, n6), F32),
        compiler_params=_params("arbitrary", "arbitrary"),
    )(cvec, ada_w, ada_b.reshape(depth, 1, n6))


def _grid_tile_spec():
    tiles_per_seq = GRID_W // COLS_PER_TILE
    return pl.BlockSpec((1, GRID_H, COLS_PER_TILE, D), lambda i: (i // tiles_per_seq, 0, i % tiles_per_seq, 0))


def _inproj_kernel(x_ref, mod_ref, g_ref, *refs, n_w, has_t, to_col_major):
    w_refs = refs[:n_w]
    o_refs = refs[n_w + has_t:2 * n_w + has_t]
    x = x_ref[0].reshape(TM, D) if to_col_major else x_ref[...]
    y = x * lax.rsqrt(jnp.mean(x * x, axis=-1, keepdims=True) + EPS) * g_ref[...]
    shift = mod_ref[0, :, 0:D]
    scale = mod_ref[0, :, D:2 * D]
    h = (y * (1.0 + scale) + shift).astype(BF16)
    if to_col_major:
        h = _dot(_grid_perm(True), h).astype(BF16)
    for w_ref, o_ref in zip(w_refs, o_refs):
        o_ref[...] = _dot(h, w_ref[...]).astype(o_ref.dtype)
    if has_t:
        wt_ref = refs[n_w]
        ot_ref = refs[2 * n_w + 1]
        ot_ref[...] = _dot_nt(wt_ref[...], h)


def _inproj(st, x, mod_l, gain, weights, out_dtypes, w_t=None, to_col_major=False):
    n_w = len(weights)
    in_specs = [_grid_tile_spec() if to_col_major else pl.BlockSpec((TM, D), lambda i: (i, 0)),
                pl.BlockSpec((1, 1, 6 * D), lambda i: (st.mod_row(i, TM), 0, 0)),
                pl.BlockSpec((1, D), lambda i: (0, 0))]
    in_specs += [pl.BlockSpec(w.shape, lambda i: (0, 0)) for w in weights]
    out_specs = [pl.BlockSpec((TM, w.shape[1]), lambda i: (i, 0)) for w in weights]
    out_shape = [jax.ShapeDtypeStruct((st.n_tok, w.shape[1]), dt) for w, dt in zip(weights, out_dtypes)]
    args = [x, mod_l.reshape(SUBLANES, 1, 6 * D), gain.reshape(1, D)] + list(weights)
    if w_t is not None:
        in_specs.append(pl.BlockSpec(w_t.shape, lambda i: (0, 0)))
        out_specs.append(pl.BlockSpec((w_t.shape[0], TM), lambda i: (0, i)))
        out_shape.append(jax.ShapeDtypeStruct((w_t.shape[0], st.n_tok), F32))
        args.append(w_t)
    return pl.pallas_call(
        functools.partial(_inproj_kernel, n_w=n_w, has_t=int(w_t is not None), to_col_major=to_col_major),
        grid=(st.n_tok // TM,),
        in_specs=in_specs, out_specs=out_specs, out_shape=out_shape,
        compiler_params=_params("arbitrary"),
    )(*args)


def _conv_from_padded(xp_ref, t, cw, cb):
    acc = cb
    for j in range(CONV_W):
        acc = acc + cw[j:j + 1, :] * xp_ref[pl.ds(SUBLANES - CONV_LEFT + j, t), :]
    return acc


def _fill_padded(xp_ref, x, t):
    zeros = jnp.zeros((SUBLANES, xp_ref.shape[1]), F32)
    xp_ref[pl.ds(0, SUBLANES), :] = zeros
    xp_ref[pl.ds(SUBLANES + t, SUBLANES), :] = zeros
    xp_ref[pl.ds(SUBLANES, t), :] = x


def _chunk_cumsum(x, c, rev):
    t = x.shape[0]
    row = lax.broadcasted_iota(jnp.int32, x.shape, 0) % c
    s = 1
    while s < c:
        if rev:
            x = x + jnp.where(row < c - s, pltpu.roll(x, t - s, 0), 0.0)
        else:
            x = x + jnp.where(row >= s, pltpu.roll(x, s, 0), 0.0)
        s *= 2
    return x


_SCAN_ROWS = 32


def _scan_tile(a, b, carry, rev):
    row = lax.broadcasted_iota(jnp.int32, a.shape, 0)
    for s in (1, 2, 4):
        if rev:
            a_s = pltpu.roll(a, SUBLANES - s, 0)
            b_s = pltpu.roll(b, SUBLANES - s, 0)
            m = row < SUBLANES - s
        else:
            a_s = pltpu.roll(a, s, 0)
            b_s = pltpu.roll(b, s, 0)
            m = row >= s
        b = jnp.where(m, a * b_s + b, b)
        a = jnp.where(m, a * a_s, a)
    h = a * carry + b
    return h, (h[0:1] if rev else h[SUBLANES - 1:SUBLANES])


def _lru_kernel(*refs, t, with_h0):
    if with_h0:
        (x_ref, y_ref, cw_ref, cb_ref, wg_ref, bg_ref, lam_ref, h0_ref,
         o_ref, xp, a_f, b_f, a_b, b_b) = refs
    else:
        (x_ref, y_ref, cw_ref, cb_ref, wg_ref, bg_ref, lam_ref,
         o_ref, st_ref, xp, a_f, b_f, a_b, b_b) = refs
    _fill_padded(xp, x_ref[...].astype(F32), t)
    xc = _conv_from_padded(xp, t, cw_ref[...], cb_ref[...])
    gates = _sigmoid(_dot(xc.astype(BF16), wg_ref[0]) + bg_ref[0])
    sp = _softplus(-lam_ref[0])
    for d, (a_ref, b_ref) in enumerate(((a_f, b_f), (a_b, b_b))):
        r = gates[:, (2 * d) * LRU_BLOCK:(2 * d + 1) * LRU_BLOCK]
        ig = gates[:, (2 * d + 1) * LRU_BLOCK:(2 * d + 2) * LRU_BLOCK]
        log_a = -LRU_C * r * sp[d:d + 1, :]
        a = jnp.exp(log_a)
        a_ref[...] = a
        b_ref[...] = jnp.sqrt(1.0 - a * a) * ig * xc

    n_it = t // _SCAN_ROWS
    if with_h0:
        c0 = (h0_ref[0, 0:1, :], h0_ref[0, 1:2, :])
    else:
        c0 = (jnp.zeros((1, LRU_BLOCK), F32), jnp.zeros((1, LRU_BLOCK), F32))

    def body(i, carry):
        cf, cb_ = carry
        base_f = pl.multiple_of(i * _SCAN_ROWS, _SCAN_ROWS)
        base_b = pl.multiple_of((n_it - 1 - i) * _SCAN_ROWS, _SCAN_ROWS)
        for k in range(_SCAN_ROWS // SUBLANES):
            rf = pl.ds(base_f + k * SUBLANES, SUBLANES)
            h, cf = _scan_tile(a_f[rf, :], b_f[rf, :], cf, False)
            b_f[rf, :] = h
            rb = pl.ds(base_b + _SCAN_ROWS - (k + 1) * SUBLANES, SUBLANES)
            h, cb_ = _scan_tile(a_b[rb, :], b_b[rb, :], cb_, True)
            b_b[rb, :] = h
        return cf, cb_

    cf, cb_ = lax.fori_loop(0, n_it, body, c0)
    o_ref[...] = (_gelu_tanh(y_ref[...].astype(F32)) * (b_f[...] + b_b[...])).astype(o_ref.dtype)
    if not with_h0:
        st_ref[0, 0:1, :] = cf
        st_ref[0, 1:2, :] = cb_


def _lru_call(st, xbr, ybr, p, h0=None):
    nb = D // LRU_BLOCK
    t, nseq = st.t, st.nseq
    tok = lambda s, j: (s, j)
    in_specs = [pl.BlockSpec((t, LRU_BLOCK), tok),
                pl.BlockSpec((t, LRU_BLOCK), tok),
                pl.BlockSpec((CONV_W, LRU_BLOCK), lambda s, j: (0, j)),
                pl.BlockSpec((1, LRU_BLOCK), lambda s, j: (0, j)),
                pl.BlockSpec((1, LRU_BLOCK, 4 * LRU_BLOCK), lambda s, j: (j, 0, 0)),
                pl.BlockSpec((1, 1, 4 * LRU_BLOCK), lambda s, j: (j, 0, 0)),
                pl.BlockSpec((1, 2, LRU_BLOCK), lambda s, j: (j, 0, 0))]
    args = [xbr, ybr, p["conv_w"], p["conv_b"], p["wg"], p["bg"], p["lam"]]
    scratch = [pltpu.VMEM((t + 2 * SUBLANES, LRU_BLOCK), F32)] + [pltpu.VMEM((t, LRU_BLOCK), F32)] * 4
    mix_shape = jax.ShapeDtypeStruct((st.n_tok, D), BF16)
    out_mix_spec = pl.BlockSpec((t, LRU_BLOCK), tok)
    if h0 is not None:
        in_specs.append(pl.BlockSpec((1, 2, LRU_BLOCK), lambda s, j: (s, 0, j)))
        args.append(h0)
        return pl.pallas_call(
            functools.partial(_lru_kernel, t=t, with_h0=True),
            grid=(nseq, nb), in_specs=in_specs, out_specs=out_mix_spec, out_shape=mix_shape,
            scratch_shapes=scratch,
            compiler_params=_params("arbitrary", "arbitrary"),
        )(*args)
    return pl.pallas_call(
        functools.partial(_lru_kernel, t=t, with_h0=False),
        grid=(nseq, nb), in_specs=in_specs,
        out_specs=[out_mix_spec, pl.BlockSpec((1, 2, LRU_BLOCK), lambda s, j: (s, 0, j))],
        out_shape=[mix_shape, jax.ShapeDtypeStruct((nseq, 2, D), F32)],
        scratch_shapes=scratch,
        compiler_params=_params("arbitrary", "arbitrary"),
    )(*args)


GDN_GROUP = 4
GDN_SLAB = GDN_GROUP * GDN_CHUNK


def _compact_select(x, c):
    g = x.shape[0] // c
    lane_blk = lax.broadcasted_iota(jnp.int32, (c, g * c), 1) // c
    out = x[0:c]
    for p in range(1, g):
        out = jnp.where(lane_blk == p, x[p * c:(p + 1) * c], out)
    return out


def _block_diag(xc, same_block):
    g = xc.shape[1] // xc.shape[0]
    return jnp.where(same_block, jnp.concatenate([xc] * g, axis=0), jnp.zeros((), xc.dtype))


def _dot_hp_bd(a, bc, same_block):
    ah, al = _split2(a)
    bh, bl = _split2(bc)
    bdh = _block_diag(bh, same_block)
    return _dot(ah, bdh) + _dot(al, bdh) + _dot(ah, _block_diag(bl, same_block))


GDN_INV_BASE = 16
GDN_HEADS_PER_STEP = LANES // GDN_CHUNK


def _tri_inverse_compact(a_cs, eye_c, same_block):
    c = eye_c.shape[0]
    row = lax.broadcasted_iota(jnp.int32, eye_c.shape, 0)
    col = lax.broadcasted_iota(jnp.int32, eye_c.shape, 1) % c
    diag = (row // GDN_INV_BASE) == (col // GDN_INV_BASE)
    p_cs = [jnp.where(diag, -a_c, 0.0) for a_c in a_cs]
    t_cs = [eye_c + p_c for p_c in p_cs]
    p_cs = [_dot_hp_bd(p_c, p_c, same_block) for p_c in p_cs]
    for _ in range(int(math.log2(GDN_INV_BASE)) - 2):
        xs = [_dot_hp_bd(jnp.concatenate([t_c, p_c], axis=0), p_c, same_block) for t_c, p_c in zip(t_cs, p_cs)]
        t_cs = [t_c + x[0:c] for t_c, x in zip(t_cs, xs)]
        p_cs = [x[c:2 * c] for x in xs]
    t_cs = [t_c + _dot_hp_bd(t_c, p_c, same_block) for t_c, p_c in zip(t_cs, p_cs)]
    b = GDN_INV_BASE
    while b < c:
        off_diag = ((row // (2 * b)) == (col // (2 * b))) & ((row // b) != (col // b))
        xs = [_dot_hp_bd(jnp.where(off_diag, a_c, 0.0), t_c, same_block) for a_c, t_c in zip(a_cs, t_cs)]
        t_cs = [t_c - _dot_hp_bd(t_c, x, same_block) for t_c, x in zip(t_cs, xs)]
        b *= 2
    return t_cs


def _gdn_scores(q, k, gcb, gcrow, beta, masks):
    c = GDN_CHUNK
    same_block, eye_c, incl_c, strict_c = masks
    kb = [k * beta[d] for d in range(2)]
    prod = _dot_nt(jnp.concatenate([kb[0], kb[1], q], axis=0).astype(BF16), k.astype(BF16))
    n = q.shape[0]
    qk_c = _compact_select(prod[2 * n:3 * n], c)
    low_half = lax.broadcasted_iota(jnp.int32, (c, LANES), 1) < c
    a_cs, qk_cs = [], []
    for d in range(2):
        gi_c = jnp.concatenate(
            [jnp.where(low_half, gcb[d][(2 * h) * c:(2 * h + 1) * c], gcb[d][(2 * h + 1) * c:(2 * h + 2) * c])
             for h in range(GDN_GROUP // 2)], axis=1)
        diff = gi_c - gcrow[d]
        decay = jnp.where(incl_c[d], jnp.exp(jnp.where(incl_c[d], diff, 0.0)), 0.0)
        a_cs.append(jnp.where(strict_c[d], _compact_select(prod[d * n:(d + 1) * n], c) * decay, 0.0))
        qk_cs.append(qk_c * decay)
    return a_cs, qk_cs


def _gdn_wy(q, k, v, gcb, beta, gl, t_c, same_block):
    eg = jnp.exp(gcb)
    rhs = jnp.concatenate([v * beta, k * beta * eg], axis=1).astype(BF16)
    uw = _dot(_block_diag(t_c.astype(BF16), same_block), rhs)
    k_dec_t = (k * jnp.exp(gl - gcb)).T
    return uw[:, 0:GDN_DV], uw[:, GDN_DV:].astype(BF16), (q * eg).astype(BF16), k_dec_t


def _pair_lanes(blocks, p, c):
    lane = None
    out = None
    for h, x in enumerate(blocks):
        half = x[:, (p // 2) * LANES:(p // 2 + 1) * LANES]
        if p % 2 != h:
            half = pltpu.roll(half, c, 1)
        if out is None:
            lane = lax.broadcasted_iota(jnp.int32, half.shape, 1)
            out = half
        else:
            out = jnp.where(lane >= h * c, half, out)
    return out


def _gdn_kernel(*refs, t, with_s0):
    if with_s0:
        (q_ref, k_ref, v_ref, z_ref, ab_ref, abt_ref, cwq_ref, cwk_ref, cwv_ref, cbq_ref, cbk_ref, cbv_ref,
         alog_ref, dtb_ref, alogc_ref, dtbc_ref, ng_ref, s0_ref,
         o_ref, xp, qs, ks, vs, osum, gl_s, gcrow_s, u_s, wq_s, qkkd_s) = refs
    else:
        (q_ref, k_ref, v_ref, z_ref, ab_ref, abt_ref, cwq_ref, cwk_ref, cwv_ref, cbq_ref, cbk_ref, cbv_ref,
         alog_ref, dtb_ref, alogc_ref, dtbc_ref, ng_ref,
         o_ref, st_ref, xp, qs, ks, vs, osum, gl_s, gcrow_s, u_s, wq_s, qkkd_s) = refs
    c = GDN_CHUNK
    n_chunks = t // c
    hp = GDN_HEADS_PER_STEP
    heads = [pl.program_id(1) * hp + hh for hh in range(hp)]
    head_lanes = [pl.ds(hh * LANES, LANES) for hh in range(hp)]

    for hh in range(hp):
        for x_ref, cw_ref, cb_ref, dst, kind in ((q_ref, cwq_ref, cbq_ref, qs, "q"),
                                                 (k_ref, cwk_ref, cbk_ref, ks, "k"),
                                                 (v_ref, cwv_ref, cbv_ref, vs, "v")):
            _fill_padded(xp, x_ref[:, head_lanes[hh]].astype(F32), t)
            y = _silu(_conv_from_padded(xp, t, cw_ref[:, head_lanes[hh]], cb_ref[:, head_lanes[hh]]))
            if kind != "v":
                y = y * lax.rsqrt(jnp.sum(y * y, axis=-1, keepdims=True) + EPS)
            if kind == "q":
                y = y * (GDN_DK ** -0.5)
            dst[hh] = y

    abt_rows = lambda idx: abt_ref[idx, 0]
    slab = GDN_SLAB
    bi = lax.broadcasted_iota(jnp.int32, (slab, slab), 0)
    bj = lax.broadcasted_iota(jnp.int32, (slab, slab), 1)
    same_chunk = (bi // c) == (bj // c)
    tri = [jnp.where(same_chunk & ((bi <= bj) if d == 1 else (bi >= bj)), 1.0, 0.0).astype(BF16) for d in range(2)]
    for hh, head in enumerate(heads):
        for d in range(2):
            a_row = abt_rows(d * GDN_H + head)
            al = alogc_ref[pl.ds(d * GDN_H + head, 1), :]
            db = dtbc_ref[pl.ds(d * GDN_H + head, 1), :]
            g_rows = -jnp.exp(al) * _softplus(a_row + db)
            gcrow_s[hh, d] = _dot_sel_right(g_rows, tri[1 - d])

    osum[...] = jnp.zeros_like(osum)
    ci = lax.broadcasted_iota(jnp.int32, (c, slab), 0)
    cj = lax.broadcasted_iota(jnp.int32, (c, slab), 1) % c
    masks = (same_chunk, jnp.where(ci == cj, 1.0, 0.0), (ci >= cj, ci <= cj), (ci > cj, ci < cj))
    if with_s0:
        s_init = tuple(jnp.concatenate([s0_ref[0, d, hh] for hh in range(hp)], axis=1) for d in range(2))
    else:
        s_init = (jnp.zeros((GDN_DK, hp * GDN_DV), F32), jnp.zeros((GDN_DK, hp * GDN_DV), F32))

    lane = lax.broadcasted_iota(jnp.int32, (slab, LANES), 1)

    def prepare(i, _):
        base = pl.multiple_of(i * slab, slab)
        rows = pl.ds(base, slab)
        ab = ab_ref[rows, :]
        gall = jnp.where(lane < 2 * GDN_H, -jnp.exp(alog_ref[...]) * _softplus(ab + dtb_ref[...]), _sigmoid(ab))
        pick = lambda idx: jnp.broadcast_to(jnp.sum(jnp.where(lane == idx, gall, 0.0), axis=-1, keepdims=True),
                                            (slab, LANES))
        beta = [[pick(2 * GDN_H + d * GDN_H + head) for d in range(2)] for head in heads]
        gcb = [[_chunk_cumsum(pick(d * GDN_H + head), c, d == 1) for d in range(2)] for head in heads]
        a_cs, qk_cs = [], []
        for hh in range(hp):
            a_hh, qk_hh = _gdn_scores(qs[hh, rows, :], ks[hh, rows, :], gcb[hh],
                                      [gcrow_s[hh, d, pl.ds(i, 1), :] for d in range(2)], beta[hh], masks)
            a_cs += a_hh
            qk_cs.append(qk_hh)
        t_cs = _tri_inverse_compact(a_cs, masks[1], same_chunk)
        for d in range(2):
            k_dec_ts = []
            for hh in range(hp):
                totals = [gcb[hh][d][p * c + (0 if d == 1 else c - 1):p * c + (0 if d == 1 else c - 1) + 1]
                          for p in range(GDN_GROUP)]
                for p in range(GDN_GROUP):
                    gl_s[hh, d, pl.ds(i * GDN_GROUP + p, 1), :] = totals[p]
                gl = jnp.concatenate([jnp.broadcast_to(row, (c, LANES)) for row in totals], axis=0)
                u, w, qg, k_dec_t = _gdn_wy(qs[hh, rows, :], ks[hh, rows, :], vs[hh, rows, :], gcb[hh][d],
                                            beta[hh][d], gl, t_cs[hh * 2 + d], same_chunk)
                k_dec_ts.append(k_dec_t)
                u_s[d, rows, head_lanes[hh]] = u
                for p in range(GDN_GROUP):
                    dst = pl.multiple_of(base * 2 + p * 2 * c, 2 * c)
                    wq_s[d, pl.ds(dst, c), head_lanes[hh]] = w[p * c:(p + 1) * c]
                    wq_s[d, pl.ds(dst + c, c), head_lanes[hh]] = qg[p * c:(p + 1) * c]
            for p in range(GDN_GROUP):
                n = i * GDN_GROUP + p
                qkkd_s[d, n, pl.ds(0, c), :] = _pair_lanes([qk_cs[hh][d] for hh in range(hp)], p, c).astype(BF16)
                qkkd_s[d, n, pl.ds(c, GDN_DK), :] = _pair_lanes(k_dec_ts, p, c).astype(BF16)
        return 0

    lax.fori_loop(0, t // slab, prepare, 0)

    sr = lax.broadcasted_iota(jnp.int32, (hp * GDN_DK, hp * GDN_DV), 0) // GDN_DK
    sc = lax.broadcasted_iota(jnp.int32, (hp * GDN_DK, hp * GDN_DV), 1) // GDN_DV
    vr = lax.broadcasted_iota(jnp.int32, (hp * c, hp * GDN_DV), 0) // c
    vc = lax.broadcasted_iota(jnp.int32, (hp * c, hp * GDN_DV), 1) // GDN_DV

    def body(i, carry):
        out = []
        for d in range(2):
            n = (n_chunks - 1 - i) if d == 1 else i
            base = pl.multiple_of(n * c, c)
            rows = pl.ds(base, c)
            s = carry[d]
            s_bd = jnp.where(sr == sc, jnp.concatenate([s] * hp, axis=0), 0.0).astype(BF16)
            ws_qs = _dot(wq_s[d, pl.ds(pl.multiple_of(n * 2 * c, 2 * c), 2 * c), :], s_bd)
            v_new = u_s[d, rows, :] - ws_qs[0:c]
            v_bd = jnp.where(vr == vc, jnp.concatenate([v_new] * hp, axis=0), 0.0).astype(BF16)
            upd = _dot(qkkd_s[d, n], v_bd)
            gl = jnp.concatenate([gl_s[hh, d, pl.ds(n, 1), :] for hh in range(hp)], axis=1)
            out.append(s * jnp.exp(gl) + upd[c:c + GDN_DK])
            osum[rows, :] = osum[rows, :] + ws_qs[c:2 * c] + upd[0:c]
        return tuple(out)

    s_fin = lax.fori_loop(0, n_chunks, body, s_init)
    for hh in range(hp):
        o = osum[:, head_lanes[hh]]
        y = o * lax.rsqrt(jnp.mean(o * o, axis=-1, keepdims=True) + EPS) * ng_ref[...]
        o_ref[:, head_lanes[hh]] = (y * _silu(z_ref[:, head_lanes[hh]].astype(F32))).astype(o_ref.dtype)
        if not with_s0:
            for d in range(2):
                st_ref[0, d, hh] = s_fin[d][:, hh * GDN_DV:(hh + 1) * GDN_DV]


def _dot_sel_right(x, sel_bf16):
    hi, mid, lo = _split3(x)
    return _dot(hi, sel_bf16) + _dot(mid, sel_bf16) + _dot(lo, sel_bf16)


def _gdn_call(st, qkv, z, ab, abt, p, s0=None):
    c = GDN_SLAB
    t, nseq = st.t, st.nseq
    n_chunks = t // c
    n_pad = max(n_chunks, SUBLANES)
    abt = abt.reshape(4 * GDN_H, nseq, n_chunks, c)
    if n_pad != n_chunks:
        abt = jnp.pad(abt, ((0, 0), (0, 0), (0, n_pad - n_chunks), (0, 0)))
    hp = GDN_HEADS_PER_STEP
    wide = hp * LANES
    nqk = GDN_H // hp
    in_specs = [pl.BlockSpec((t, wide), lambda s, h: (s, h)),
                pl.BlockSpec((t, wide), lambda s, h: (s, nqk + h)),
                pl.BlockSpec((t, wide), lambda s, h: (s, 2 * nqk + h)),
                pl.BlockSpec((t, wide), lambda s, h: (s, h)),
                pl.BlockSpec((t, LANES), lambda s, h: (s, 0)),
                pl.BlockSpec((4 * GDN_H, 1, n_pad, c), lambda s, h: (0, s, 0, 0)),
                pl.BlockSpec((CONV_W, wide), lambda s, h: (0, h)),
                pl.BlockSpec((CONV_W, wide), lambda s, h: (0, nqk + h)),
                pl.BlockSpec((CONV_W, wide), lambda s, h: (0, 2 * nqk + h)),
                pl.BlockSpec((1, wide), lambda s, h: (0, h)),
                pl.BlockSpec((1, wide), lambda s, h: (0, nqk + h)),
                pl.BlockSpec((1, wide), lambda s, h: (0, 2 * nqk + h)),
                pl.BlockSpec((1, LANES), lambda s, h: (0, 0)),
                pl.BlockSpec((1, LANES), lambda s, h: (0, 0)),
                pl.BlockSpec((4 * GDN_H, 1), lambda s, h: (0, 0)),
                pl.BlockSpec((4 * GDN_H, 1), lambda s, h: (0, 0)),
                pl.BlockSpec((1, LANES), lambda s, h: (0, 0))]
    args = [qkv, qkv, qkv, z, ab, abt, p["conv_w"], p["conv_w"], p["conv_w"], p["conv_b"], p["conv_b"], p["conv_b"],
            p["alog_row"], p["dtb_row"], p["alog_col"], p["dtb_col"], p["norm_g"]]
    scratch = [pltpu.VMEM((t + 2 * SUBLANES, LANES), F32)] + [pltpu.VMEM((hp, t, LANES), F32)] * 3 + [
        pltpu.VMEM((t, wide), F32),
        pltpu.VMEM((hp, 2, max(t // GDN_CHUNK, SUBLANES), LANES), F32), pltpu.VMEM((hp, 2, n_pad, c), F32),
        pltpu.VMEM((2, t, wide), F32), pltpu.VMEM((2, 2 * t, wide), BF16),
        pltpu.VMEM((2, t // GDN_CHUNK, GDN_CHUNK + GDN_DK, hp * GDN_CHUNK), BF16)]
    mix_shape = jax.ShapeDtypeStruct((st.n_tok, GDN_H * GDN_DV), BF16)
    out_mix_spec = pl.BlockSpec((t, wide), lambda s, h: (s, h))
    st_spec = pl.BlockSpec((1, 2, hp, GDN_DK, GDN_DV), lambda s, h: (s, 0, h, 0, 0))
    if s0 is not None:
        in_specs.append(st_spec)
        args.append(s0)
        return pl.pallas_call(
            functools.partial(_gdn_kernel, t=t, with_s0=True),
            grid=(nseq, nqk), in_specs=in_specs, out_specs=out_mix_spec, out_shape=mix_shape,
            scratch_shapes=scratch,
            compiler_params=_params("arbitrary", "arbitrary"),
        )(*args)
    return pl.pallas_call(
        functools.partial(_gdn_kernel, t=t, with_s0=False),
        grid=(nseq, nqk), in_specs=in_specs,
        out_specs=[out_mix_spec, st_spec],
        out_shape=[mix_shape, jax.ShapeDtypeStruct((nseq, 2, GDN_H, GDN_DK, GDN_DV), F32)],
        scratch_shapes=scratch,
        compiler_params=_params("arbitrary", "arbitrary"),
    )(*args)


def _gla_kernel(*refs, t, with_s0):
    if with_s0:
        (q_ref, k_ref, v_ref, z_ref, lr_ref, w2_ref, b2_ref, ng_ref, s0_ref,
         o_ref, osum, gc_s) = refs
    else:
        (q_ref, k_ref, v_ref, z_ref, lr_ref, w2_ref, b2_ref, ng_ref,
         o_ref, st_ref, osum, gc_s) = refs
    c = GLA_CHUNK
    n_chunks = t // c
    slab = 256
    bi = lax.broadcasted_iota(jnp.int32, (slab, slab), 0)
    bj = lax.broadcasted_iota(jnp.int32, (slab, slab), 1)
    same_chunk = (bi // c) == (bj // c)
    lr = lr_ref[...]
    for d in range(2):
        pre = _dot_hp(lr, w2_ref[d, 0]) + b2_ref[d]
        glog = (jnp.minimum(pre, 0.0) - jnp.log1p(jnp.exp(-jnp.abs(pre)))) * (1.0 / GLA_TAU)
        gc_s[d] = _chunk_cumsum(glog, c, d == 1)

    osum[...] = jnp.zeros_like(osum)
    n_slabs = t // slab
    per_slab = slab // c
    causal = [same_chunk & ((bi <= bj) if d == 1 else (bi >= bj)) for d in range(2)]
    own_block = (lax.broadcasted_iota(jnp.int32, (slab, per_slab * GLA_DK), 0) // c
                 == lax.broadcasted_iota(jnp.int32, (slab, per_slab * GLA_DK), 1) // GLA_DK)
    if with_s0:
        s_init = (s0_ref[0, 0, 0].T, s0_ref[0, 1, 0].T)
    else:
        s_init = (jnp.zeros((GLA_DV, GLA_DK), F32), jnp.zeros((GLA_DV, GLA_DK), F32))

    def spread(x):
        return jnp.where(own_block, jnp.concatenate([x] * per_slab, axis=1), jnp.zeros((), x.dtype))

    def chunk_row(d, row):
        return gc_s[d, pl.ds(row, 1), :]

    def per_chunk_rows(d, base, offset):
        return jnp.concatenate([jnp.broadcast_to(chunk_row(d, base + p * c + offset), (c, GLA_DK))
                                for p in range(per_slab)], axis=0)

    def body(i, carry):
        stage = []
        for d in range(2):
            rev = d == 1
            base = pl.multiple_of(((n_slabs - 1 - i) if rev else i) * slab, slab)
            rows = pl.ds(base, slab)
            q = q_ref[rows, :].astype(F32) * (GLA_DK ** -0.5)
            k = k_ref[rows, :].astype(F32)
            v = v_ref[rows, :]
            gc = gc_s[d, rows, :]
            gm = per_chunk_rows(d, base, c // 2)
            gl = per_chunk_rows(d, base, 0 if rev else c - 1)
            scores = _dot_nt((q * jnp.exp(gc - gm)).astype(BF16), (k * jnp.exp(gm - gc)).astype(BF16))
            a_mat = jnp.where(causal[d], scores, 0.0).astype(BF16)
            ds_cat = _dot_tn(v, spread((k * jnp.exp(gl - gc)).astype(BF16)))
            stage.append((base, rows, v, a_mat, ds_cat, spread((q * jnp.exp(gc)).astype(BF16))))
        out = []
        for d in range(2):
            rev = d == 1
            base, rows, v, a_mat, ds_cat, qg_spread = stage[d]
            st = carry[d]
            prev = [None] * per_slab
            for p in (range(per_slab - 1, -1, -1) if rev else range(per_slab)):
                prev[p] = st.astype(BF16)
                gl_p = chunk_row(d, base + p * c + (0 if rev else c - 1))
                st = st * jnp.exp(gl_p) + ds_cat[:, p * GLA_DK:(p + 1) * GLA_DK]
            o = _dot(a_mat, v) + _dot_nt(qg_spread, jnp.concatenate(prev, axis=1))
            osum[rows, :] = osum[rows, :] + o
            out.append(st)
        return tuple(out)

    s_f, s_b = lax.fori_loop(0, n_slabs, body, s_init)
    o = osum[...]
    y = o * lax.rsqrt(jnp.mean(o * o, axis=-1, keepdims=True) + EPS) * ng_ref[...]
    o_ref[...] = (y * _silu(z_ref[...].astype(F32))).astype(o_ref.dtype)
    if not with_s0:
        st_ref[0, 0, 0] = s_f.T
        st_ref[0, 1, 0] = s_b.T


def _gla_call(st, q, k, v, z, lr, p, s0=None):
    t, nseq = st.t, st.nseq
    in_specs = [pl.BlockSpec((t, GLA_DK), lambda s, h: (s, h)),
                pl.BlockSpec((t, GLA_DK), lambda s, h: (s, h)),
                pl.BlockSpec((t, GLA_DV), lambda s, h: (s, h)),
                pl.BlockSpec((t, GLA_DV), lambda s, h: (s, h)),
                pl.BlockSpec((t, LANES), lambda s, h: (s, 0)),
                pl.BlockSpec((2, 1, LANES, GLA_DK), lambda s, h: (0, h, 0, 0)),
                pl.BlockSpec((2, 1, GLA_DK), lambda s, h: (0, 0, h)),
                pl.BlockSpec((1, GLA_DV), lambda s, h: (0, 0))]
    args = [q, k, v, z, lr, p["w2"], p["b2"], p["norm_g"]]
    scratch = [pltpu.VMEM((t, GLA_DV), F32), pltpu.VMEM((2, t, GLA_DK), F32)]
    mix_shape = jax.ShapeDtypeStruct((st.n_tok, GLA_H * GLA_DV), BF16)
    out_mix_spec = pl.BlockSpec((t, GLA_DV), lambda s, h: (s, h))
    st_spec = pl.BlockSpec((1, 2, 1, GLA_DK, GLA_DV), lambda s, h: (s, 0, h, 0, 0))
    if s0 is not None:
        in_specs.append(st_spec)
        args.append(s0)
        return pl.pallas_call(
            functools.partial(_gla_kernel, t=t, with_s0=True),
            grid=(nseq, GLA_H), in_specs=in_specs, out_specs=out_mix_spec, out_shape=mix_shape,
            scratch_shapes=scratch,
            compiler_params=_params("arbitrary", "arbitrary"),
        )(*args)
    return pl.pallas_call(
        functools.partial(_gla_kernel, t=t, with_s0=False),
        grid=(nseq, GLA_H), in_specs=in_specs,
        out_specs=[out_mix_spec, st_spec],
        out_shape=[mix_shape, jax.ShapeDtypeStruct((nseq, 2, GLA_H, GLA_DK, GLA_DV), F32)],
        scratch_shapes=scratch,
        compiler_params=_params("arbitrary", "arbitrary"),
    )(*args)


def _route(sel, s):
    scores = []
    for g in range(N_GROUPS):
        m = sel[g * GROUP_SZ:(g + 1) * GROUP_SZ]
        best = None
        for a in range(GROUP_SZ):
            for b in range(a + 1, GROUP_SZ):
                pair = m[a] + m[b]
                best = pair if best is None else jnp.maximum(best, pair)
        scores.append(best)
    gbest = jnp.zeros_like(scores[0], dtype=jnp.int32)
    top = scores[0]
    for g in range(1, N_GROUPS):
        better = scores[g] > top
        gbest = jnp.where(better, g, gbest)
        top = jnp.where(better, scores[g], top)
    picked = []
    for e in range(N_EXPERTS):
        g = e // GROUP_SZ
        rank = jnp.zeros_like(gbest)
        for m in range(g * GROUP_SZ, (g + 1) * GROUP_SZ):
            if m == e:
                continue
            ahead = (sel[m] > sel[e]) | (sel[m] == sel[e]) if m < e else (sel[m] > sel[e])
            rank = rank + ahead.astype(jnp.int32)
        picked.append(jnp.where((gbest == g) & (rank < 2), s[e], 0.0))
    total = picked[0]
    for e in range(1, N_EXPERTS):
        total = total + picked[e]
    return [p / total for p in picked]


def _outproj_kernel(*refs, n_mix, from_col_major):
    mix_refs = refs[:n_mix]
    w_refs = refs[n_mix:2 * n_mix]
    x_ref, mod_ref, g_ref, rwt_ref, rb_ref, x1_ref, h2_ref, gates_ref = refs[2 * n_mix:]
    gate1 = mod_ref[0, :, 2 * D:3 * D]
    delta = None
    for mix_ref, w_ref in zip(mix_refs, w_refs):
        mix = mix_ref[...]
        if from_col_major:
            mix = _dot(_grid_perm(False), mix).astype(BF16)
        part = _dot(mix, w_ref[...])
        delta = part if delta is None else delta + part
    x = x_ref[0].reshape(TM, D) if from_col_major else x_ref[...]
    x1 = x + gate1 * delta
    x1_ref[...] = x1
    y = x1 * lax.rsqrt(jnp.mean(x1 * x1, axis=-1, keepdims=True) + EPS) * g_ref[...]
    h2 = y * (1.0 + mod_ref[0, :, 4 * D:5 * D]) + mod_ref[0, :, 3 * D:4 * D]
    h2_ref[...] = h2.astype(BF16)
    rw_hi, rw_lo = _split2(rwt_ref[...])
    h2_hi, h2_lo = _split2(h2)
    both = _dot_nt(jnp.concatenate([rw_hi, rw_lo], axis=0), h2_hi)
    logits = both[0:N_EXPERTS] + both[N_EXPERTS:2 * N_EXPERTS] + _dot_nt(rw_hi, h2_lo)
    s_all = _sigmoid(logits)
    sel_all = s_all + rb_ref[...]
    s = [s_all[e:e + 1, :] for e in range(N_EXPERTS)]
    sel = [sel_all[e:e + 1, :] for e in range(N_EXPERTS)]
    gate_rows = _route(sel, s) + [jnp.zeros((LANES - N_EXPERTS, TM), F32)]
    gates_ref[...] = jnp.concatenate(gate_rows, axis=0).T


def _outproj(st, mixes, w_outs, x, mod_l, gain2, router_wt, router_b, from_col_major=False):
    n_mix = len(mixes)
    in_specs = [pl.BlockSpec((TM, m.shape[1]), lambda i: (i, 0)) for m in mixes]
    in_specs += [pl.BlockSpec(w.shape, lambda i: (0, 0)) for w in w_outs]
    in_specs += [_grid_tile_spec() if from_col_major else pl.BlockSpec((TM, D), lambda i: (i, 0)),
                 pl.BlockSpec((1, 1, 6 * D), lambda i: (st.mod_row(i, TM), 0, 0)),
                 pl.BlockSpec((1, D), lambda i: (0, 0)),
                 pl.BlockSpec((N_EXPERTS, D), lambda i: (0, 0)),
                 pl.BlockSpec((N_EXPERTS, 1), lambda i: (0, 0))]
    return pl.pallas_call(
        functools.partial(_outproj_kernel, n_mix=n_mix, from_col_major=from_col_major),
        grid=(st.n_tok // TM,),
        in_specs=in_specs,
        out_specs=[pl.BlockSpec((TM, D), lambda i: (i, 0)),
                   pl.BlockSpec((TM, D), lambda i: (i, 0)),
                   pl.BlockSpec((TM, LANES), lambda i: (i, 0))],
        out_shape=[jax.ShapeDtypeStruct((st.n_tok, D), F32),
                   jax.ShapeDtypeStruct((st.n_tok, D), BF16),
                   jax.ShapeDtypeStruct((st.n_tok, LANES), F32)],
        compiler_params=_params("arbitrary"),
    )(*mixes, *w_outs, x, mod_l.reshape(SUBLANES, 1, 6 * D), gain2.reshape(1, D), router_wt,
      router_b.reshape(N_EXPERTS, 1))


def _moe_kernel(h_ref, gates_ref, x1_ref, mod_ref, wg_ref, wu_ref, wd_ref, fg_ref, o_ref,
                xs, acc, gs, pos_s, meta_s, *, final_norm, to_grid):
    e = pl.program_id(1)
    tm = TM_MOE

    @pl.when(e == 0)
    def _():
        gates = gates_ref[...]
        ei = lax.broadcasted_iota(jnp.int32, (LANES, LANES), 0)
        gj = lax.broadcasted_iota(jnp.int32, (LANES, LANES), 1)
        member = jnp.where((ei < N_EXPERTS) & (ei // GROUP_SZ == gj), 1.0, 0.0).astype(BF16)
        onehot = jnp.where(_dot(gates.astype(BF16), member) > 0.0, 1.0, 0.0)
        ti = lax.broadcasted_iota(jnp.int32, (tm, tm), 0)
        tj = lax.broadcasted_iota(jnp.int32, (tm, tm), 1)
        rank = _dot(jnp.where(ti >= tj, 1.0, 0.0).astype(BF16), onehot.astype(BF16))
        count = rank[tm - 1:tm, :]
        blocks = jnp.floor((count + (MOE_SEG - 1.0)) * (1.0 / MOE_SEG))
        before = jnp.where(ei < gj, 1.0, 0.0).astype(BF16)
        base = _dot(jnp.broadcast_to(blocks, (SUBLANES, LANES)).astype(BF16), before)[0:1] * MOE_SEG
        pos = jnp.sum(onehot * (base + rank - 1.0), axis=-1, keepdims=True)
        pos_s[...] = jnp.broadcast_to(pos, (tm, LANES))
        meta_s[...] = jnp.concatenate([base, blocks, jnp.zeros((SUBLANES - 2, LANES), F32)], axis=0)
        pos_row = pos_s[...].T[0:1, :]
        ri = lax.broadcasted_iota(jnp.int32, (MOE_ROWS, tm), 0).astype(F32)
        perm = jnp.where(ri == pos_row, 1.0, 0.0).astype(BF16)
        xs[...] = _dot(perm, h_ref[...]).astype(BF16)
        g_hi, g_lo = _split2(gates)
        gs[...] = _dot(perm, g_hi) + _dot(perm, g_lo)
        acc[...] = jnp.zeros_like(acc)

    first = e * MOE_EXPERTS_PER_STEP
    group = first // GROUP_SZ
    meta = meta_s[...]
    m_row = lax.broadcasted_iota(jnp.int32, meta.shape, 0)
    m_lane = lax.broadcasted_iota(jnp.int32, meta.shape, 1)
    base = jnp.sum(jnp.where((m_row == 0) & (m_lane == group), meta, 0.0)).astype(jnp.int32)
    n_blocks = jnp.sum(jnp.where((m_row == 1) & (m_lane == group), meta, 0.0)).astype(jnp.int32)
    lane = lax.broadcasted_iota(jnp.int32, (MOE_SEG, LANES), 1)

    def block(b, _):
        rows = pl.ds(pl.multiple_of(base + b * MOE_SEG, MOE_SEG), MOE_SEG)
        x = xs[rows, :]
        g_rows = gs[rows, :]
        total = None
        for k in range(MOE_EXPERTS_PER_STEP):
            gate = jnp.sum(jnp.where(lane == first + k, g_rows, 0.0), axis=-1, keepdims=True)
            hid = _silu(_dot(x, wg_ref[0, k])) * _dot(x, wu_ref[0, k]) * gate
            part = _dot(hid.astype(BF16), wd_ref[0, k])
            total = part if total is None else total + part
        acc[rows, :] += total
        return 0

    lax.fori_loop(0, n_blocks, block, 0)

    @pl.when(e == N_EXPERTS // MOE_EXPERTS_PER_STEP - 1)
    def _():
        li = lax.broadcasted_iota(jnp.int32, (tm, MOE_ROWS), 1).astype(F32)
        unperm = jnp.where(li == pos_s[:, 0:1], 1.0, 0.0).astype(BF16)
        moe = _dot(unperm, acc[...].astype(BF16))
        out = x1_ref[...] + mod_ref[0, :, 5 * D:6 * D] * moe
        if final_norm:
            out = out * lax.rsqrt(jnp.mean(out * out, axis=-1, keepdims=True) + EPS) * fg_ref[...]
        if to_grid:
            for j in range(TM_MOE // TM):
                o_ref[0, :, j * COLS_PER_TILE:(j + 1) * COLS_PER_TILE, :] = (
                    out[j * TM:(j + 1) * TM].reshape(GRID_H, COLS_PER_TILE, D))
        else:
            o_ref[...] = out


def _moe(st, h2, gates, x1, mod_l, w_gate, w_up, w_down, layer, final_gain, final_norm, to_grid=False):
    tm = TM_MOE
    if to_grid:
        tiles_per_seq = st.t // tm
        cols = tm // GRID_H
        out_spec = pl.BlockSpec((1, GRID_H, cols, D), lambda i, e: (i // tiles_per_seq, 0, i % tiles_per_seq, 0))
        out_shape = jax.ShapeDtypeStruct((st.nseq, GRID_H, GRID_W, D), F32)
    else:
        out_spec = pl.BlockSpec((tm, D), lambda i, e: (i, 0))
        out_shape = jax.ShapeDtypeStruct((st.n_tok, D), F32)
    return pl.pallas_call(
        functools.partial(_moe_kernel, final_norm=final_norm, to_grid=to_grid),
        grid=(st.n_tok // tm, N_EXPERTS // MOE_EXPERTS_PER_STEP),
        in_specs=[pl.BlockSpec((tm, D), lambda i, e: (i, 0)),
                  pl.BlockSpec((tm, LANES), lambda i, e: (i, 0)),
                  pl.BlockSpec((tm, D), lambda i, e: (i, 0)),
                  pl.BlockSpec((1, 1, 6 * D), lambda i, e: (st.mod_row(i, tm), 0, 0)),
                  pl.BlockSpec((1, MOE_EXPERTS_PER_STEP, D, D_EXPERT), lambda i, e: (layer, e, 0, 0)),
                  pl.BlockSpec((1, MOE_EXPERTS_PER_STEP, D, D_EXPERT), lambda i, e: (layer, e, 0, 0)),
                  pl.BlockSpec((1, MOE_EXPERTS_PER_STEP, D_EXPERT, D), lambda i, e: (layer, e, 0, 0)),
                  pl.BlockSpec((1, D), lambda i, e: (0, 0))],
        out_specs=out_spec, out_shape=out_shape,
        scratch_shapes=[pltpu.VMEM((MOE_ROWS, D), BF16), pltpu.VMEM((MOE_ROWS, D), F32),
                        pltpu.VMEM((MOE_ROWS, LANES), F32), pltpu.VMEM((tm, LANES), F32),
                        pltpu.VMEM((SUBLANES, LANES), F32)],
        compiler_params=_params("arbitrary", "arbitrary"),
    )(h2, gates, x1, mod_l.reshape(SUBLANES, 1, 6 * D), w_gate, w_up, w_down, final_gain.reshape(1, D))


def _layer_ab(xs, mod_l, state_lru, state_gdn, norm1_g, w_in, lru_conv_w, lru_conv_b, lru_wa, lru_ba, lru_wx,
              lru_bx, lru_lam, gdn_conv_w, gdn_conv_b, gdn_a_log, gdn_dt_bias, gdn_norm_g):
    w = w_in.astype(BF16)
    n_qkv = 2 * GDN_H * GDN_DK + GDN_H * GDN_DV
    o = 0
    w_y, o = w[:, o:o + D], o + D
    w_x, o = w[:, o:o + D], o + D
    w_qkv, o = w[:, o:o + n_qkv], o + n_qkv
    w_z, o = w[:, o:o + GDN_H * GDN_DV], o + GDN_H * GDN_DV
    w_ab = w[:, o:]
    w_ab_pad = jnp.pad(w_ab, ((0, 0), (0, LANES - w_ab.shape[1])))
    proj = [_inproj(st, x, mod_l, norm1_g, [w_y, w_x, w_qkv, w_z, w_ab_pad], [BF16, BF16, BF16, BF16, F32],
                    w_t=w_ab.T) for st, x in zip((PROMPT, SAMPLE), xs)]

    nb = D // LRU_BLOCK
    wg = jnp.stack([lru_wa[0], lru_wx[0], lru_wa[1], lru_wx[1]], axis=1)
    wg = wg.transpose(0, 2, 1, 3).reshape(nb, LRU_BLOCK, 4 * LRU_BLOCK).astype(BF16)
    bg = jnp.stack([lru_ba[0], lru_bx[0], lru_ba[1], lru_bx[1]], axis=0)
    bg = bg.reshape(4, nb, LRU_BLOCK).transpose(1, 0, 2).reshape(nb, 1, 4 * LRU_BLOCK)
    lam = lru_lam.reshape(2, nb, LRU_BLOCK).transpose(1, 0, 2)
    lru_p = dict(conv_w=lru_conv_w, conv_b=lru_conv_b.reshape(1, D), wg=wg, bg=bg, lam=lam)
    (ybr_p, xbr_p, qkv_p, z_p, ab_p, abt_p), (ybr_s, xbr_s, qkv_s, z_s, ab_s, abt_s) = proj
    lru_p_out, lru_state = _lru_call(PROMPT, xbr_p, ybr_p, lru_p)
    lru_s_out = _lru_call(SAMPLE, xbr_s, ybr_s, lru_p, h0=state_lru)

    pad16 = lambda v: jnp.pad(v.reshape(1, 2 * GDN_H), ((0, 0), (0, LANES - 2 * GDN_H)))
    col32 = lambda v: jnp.pad(v.reshape(2 * GDN_H, 1), ((0, 2 * GDN_H), (0, 0)))
    gdn_p = dict(conv_w=gdn_conv_w, conv_b=gdn_conv_b.reshape(1, n_qkv), alog_row=pad16(gdn_a_log),
                 dtb_row=pad16(gdn_dt_bias), alog_col=col32(gdn_a_log), dtb_col=col32(gdn_dt_bias),
                 norm_g=gdn_norm_g.reshape(1, GDN_DV))
    gdn_p_out, gdn_state = _gdn_call(PROMPT, qkv_p, z_p, ab_p, abt_p, gdn_p)
    gdn_s_out = _gdn_call(SAMPLE, qkv_s, z_s, ab_s, abt_s, gdn_p, s0=state_gdn)
    return [lru_p_out, gdn_p_out], [lru_s_out, gdn_s_out], lru_state, gdn_state


def _layer_c(xs, mod_l, state_gla, norm1_g, w_in, w2, b2, norm_g):
    w = w_in.astype(BF16)
    nk = GLA_H * GLA_DK
    nv = GLA_H * GLA_DV
    w_q, w_k, w_v, w_z = w[:, 0:nk], w[:, nk:2 * nk], w[:, 2 * nk:2 * nk + nv], w[:, 2 * nk + nv:2 * nk + 2 * nv]
    w_lr = jnp.pad(w[:, 2 * nk + 2 * nv:], ((0, 0), (0, LANES - 2 * GLA_RANK)))
    weights, dtypes = [w_q, w_k, w_v, w_z, w_lr], [BF16, BF16, BF16, BF16, F32]
    proj_p = _inproj(PROMPT, xs[0], mod_l, norm1_g, weights, dtypes)
    proj_s = _inproj(SAMPLE, xs[1], mod_l, norm1_g, weights, dtypes, to_col_major=True)
    w2h = w2.reshape(2, GLA_RANK, GLA_H, GLA_DK).transpose(0, 2, 1, 3)
    w2big = jnp.zeros((2, GLA_H, LANES, GLA_DK), F32)
    w2big = w2big.at[0, :, 0:GLA_RANK].set(w2h[0]).at[1, :, GLA_RANK:2 * GLA_RANK].set(w2h[1])
    gla_p = dict(w2=w2big, b2=b2.reshape(2, 1, nk), norm_g=norm_g.reshape(1, GLA_DV))
    gla_p_out, gla_state = _gla_call(PROMPT, *proj_p, gla_p)
    gla_s_out = _gla_call(SAMPLE, *proj_s, gla_p, s0=state_gla)
    return gla_p_out, gla_s_out, gla_state


def kernel(x_prompt, x_sample, state_lru, state_gdn, state_gla, c, c_ctx, ada_w, ada_b, norm1_g, norm2_g,
           final_norm_g, ab_w_in, lru_conv_w, lru_conv_b, lru_wa, lru_ba, lru_wx, lru_bx, lru_lam, gdn_conv_w,
           gdn_conv_b, gdn_a_log, gdn_dt_bias, gdn_norm_g, ab_w_out, gla_w_in, gla_w2, gla_b2, gla_norm_g,
           gla_w_out, router_w, router_b, moe_w_gate, moe_w_up, moe_w_down):
    xp = x_prompt.reshape(N_PROMPT, D)
    xs = x_sample.reshape(N_SAMPLE, D)
    cvec = jnp.concatenate([c_ctx.reshape(1, D), c, jnp.zeros((SUBLANES - 1 - N_SAMPLE_SEQ, D), F32)], axis=0)
    mod = _modulation(cvec, ada_w, ada_b)
    router_wt = router_w.T
    moe_w = (moe_w_gate.astype(BF16), moe_w_up.astype(BF16), moe_w_down.astype(BF16))

    mix_p, mix_s, lru_state, gdn_state = _layer_ab(
        (xp, xs), mod[0], state_lru[:, 0], state_gdn[:, 0], norm1_g[0], ab_w_in[0], lru_conv_w[0], lru_conv_b[0],
        lru_wa[0], lru_ba[0], lru_wx[0], lru_bx[0], lru_lam[0], gdn_conv_w[0], gdn_conv_b[0], gdn_a_log[0],
        gdn_dt_bias[0], gdn_norm_g[0])
    w_out = ab_w_out[0].astype(BF16)
    w_outs = [w_out[:D], w_out[D:]]
    streams = []
    for st, mixes, x in ((PROMPT, mix_p, xp), (SAMPLE, mix_s, xs)):
        x1, h2, gates = _outproj(st, mixes, w_outs, x, mod[0], norm2_g[0], router_wt, router_b)
        streams.append(_moe(st, h2, gates, x1, mod[0], *moe_w, 0, final_norm_g, False))
    xp, xs = streams
    xs_grid = xs.reshape(N_SAMPLE_SEQ, GRID_H, GRID_W, D)

    gla_p_out, gla_s_out, gla_state = _layer_c((xp, xs_grid), mod[1], state_gla[:, 0], norm1_g[1], gla_w_in[0],
                                               gla_w2[0], gla_b2[0], gla_norm_g[0])
    w_outs = [gla_w_out[0].astype(BF16)]
    x1, h2, gates = _outproj(PROMPT, [gla_p_out], w_outs, xp, mod[1], norm2_g[1], router_wt, router_b)
    y_prompt = _moe(PROMPT, h2, gates, x1, mod[1], *moe_w, 1, final_norm_g, True)
    x1, h2, gates = _outproj(SAMPLE, [gla_s_out], w_outs, xs_grid, mod[1], norm2_g[1], router_wt, router_b,
                             from_col_major=True)
    y_sample = _moe(SAMPLE, h2, gates, x1, mod[1], *moe_w, 1, final_norm_g, True, to_grid=True)

    return (y_prompt.reshape(N_PROMPT_SEQ, T_PROMPT, D), y_sample.reshape(N_SAMPLE_SEQ, T_SAMPLE, D),
            lru_state[:, None], gdn_state[:, None], gla_state[:, None])
```
